```python
import jax
import jax.numpy as jnp
from jax import lax
import numpy as np

D_MODEL = 2048
BATCH = 1
SEQ = 8192
DEPTH = 1

MLA_HEADS = 8
QK_NOPE = 128
QK_ROPE = 64
V_HEAD = 128
Q_LORA = 512
KV_LORA = 256
MLA_WIDTH = MLA_HEADS * V_HEAD
MOBA_HEADS = 8
MOBA_HEAD_DIM = 128
MOBA_WIDTH = MOBA_HEADS * MOBA_HEAD_DIM
MOBA_BLOCK = 256
MOBA_TOPK = 3
MOBA_QCHUNK = 32
MIX_WIDTH = MLA_WIDTH + MOBA_WIDTH
IN_COLS = Q_LORA + KV_LORA + QK_ROPE + 3 * MOBA_WIDTH
ROPE_THETA = 10000.0
ATTN_QBLOCK = 128
N_EXPERTS = 32
TOP_K = 4
D_FF = D_MODEL
SWIGLU_LIMIT = 7.0
SWIGLU_ALPHA = 1.702
MOE_BLOCK = 256
DEEPNORM_ALPHA = float((2 * DEPTH) ** 0.25)
DEEPNORM_BETA = float((8 * DEPTH) ** -0.25)
RMS_EPS = 1e-6
LN_EPS = 1e-5

kernel_name = "hybrid_mla_moba_moe_deepnorm"


def rms_norm(x, g):
    xf = x.astype(jnp.float32)
    y = xf * lax.rsqrt(jnp.mean(xf * xf, axis=-1, keepdims=True) + RMS_EPS)
    return (y * g.astype(jnp.float32)).astype(x.dtype)


def layer_norm(x, g, b):
    xf = x.astype(jnp.float32)
    mu = jnp.mean(xf, axis=-1, keepdims=True)
    var = jnp.mean(jnp.square(xf - mu), axis=-1, keepdims=True)
    y = (xf - mu) * lax.rsqrt(var + LN_EPS)
    return (y * g.astype(jnp.float32) + b.astype(jnp.float32)).astype(x.dtype)


def rope(x, pos):
    d = x.shape[-1]
    half = d // 2
    inv_freq = ROPE_THETA ** (-jnp.arange(half, dtype=jnp.float32) * 2.0 / d)
    ang = pos.astype(jnp.float32)[:, :, None] * inv_freq
    cos = jnp.cos(ang)[:, :, None, :]
    sin = jnp.sin(ang)[:, :, None, :]
    xf = x.astype(jnp.float32)
    x1, x2 = xf[..., :half], xf[..., half:]
    return jnp.concatenate([x1 * cos - x2 * sin, x2 * cos + x1 * sin], axis=-1).astype(x.dtype)


def causal_dense_attention(q, k, v, scale):
    B, S, H, Dq = q.shape
    nq = S // ATTN_QBLOCK
    qb = q.reshape(B, nq, ATTN_QBLOCK, H, Dq).transpose(1, 0, 2, 3, 4)
    k_pos = jnp.arange(S)

    def step(args):
        qc, i = args
        s = jnp.einsum('bqhd,bkhd->bhqk', qc, k).astype(jnp.float32) * scale
        q_pos = i * ATTN_QBLOCK + jnp.arange(ATTN_QBLOCK)
        s = jnp.where(k_pos[None, :] <= q_pos[:, None], s, -jnp.inf)
        p = jax.nn.softmax(s, axis=-1).astype(v.dtype)
        return jnp.einsum('bhqk,bkhd->bqhd', p, v)

    o = lax.map(step, (qb, jnp.arange(nq)))
    return o.transpose(1, 0, 2, 3, 4).reshape(B, S, H, v.shape[-1])


def mla_attention(h_q, h_kv, h_kr, positions, g_q_a, w_q_b, g_kv_a, w_kv_b):
    B, S, _ = h_q.shape
    q = (rms_norm(h_q, g_q_a) @ w_q_b).reshape(B, S, MLA_HEADS, QK_NOPE + QK_ROPE)
    q = jnp.concatenate([q[..., :QK_NOPE], rope(q[..., QK_NOPE:], positions)], axis=-1)
    kv = (rms_norm(h_kv, g_kv_a) @ w_kv_b).reshape(B, S, MLA_HEADS, QK_NOPE + V_HEAD)
    k_pe = rope(h_kr[:, :, None, :], positions)
    k = jnp.concatenate([kv[..., :QK_NOPE],
                         jnp.broadcast_to(k_pe, (B, S, MLA_HEADS, QK_ROPE))], axis=-1)
    v = kv[..., QK_NOPE:]
    o = causal_dense_attention(q, k, v, (QK_NOPE + QK_ROPE) ** -0.5)
    return o.reshape(B, S, MLA_WIDTH)


def moba_attention(q, k, v):
    B, S, H, D = q.shape
    nb = -(-S // MOBA_BLOCK)
    s_pad = nb * MOBA_BLOCK
    pad = ((0, 0), (0, s_pad - S), (0, 0), (0, 0))
    qp, kp, vp = jnp.pad(q, pad), jnp.pad(k, pad), jnp.pad(v, pad)
    kb = kp.reshape(B, nb, MOBA_BLOCK, H, D).transpose(0, 3, 1, 2, 4)
    vb = vp.reshape(B, nb, MOBA_BLOCK, H, D).transpose(0, 3, 1, 2, 4)
    kbar = jnp.mean(kb.astype(jnp.float32), axis=3)
    gate = jnp.einsum('bshd,bhnd->bshn', qp.astype(jnp.float32), kbar)
    q_blk = jnp.arange(s_pad) // MOBA_BLOCK
    past = jnp.arange(nb)[None, :] < q_blk[:, None]
    gate = jnp.where(past[None, :, None, :], gate, -jnp.inf)
    kt = min(MOBA_TOPK, nb)
    _, sel = lax.top_k(gate, kt)

    nc = s_pad // MOBA_QCHUNK
    qc_all = qp.reshape(B, nc, MOBA_QCHUNK, H, D).transpose(1, 0, 2, 3, 4)
    sel_all = sel.reshape(B, nc, MOBA_QCHUNK, H, kt).transpose(1, 0, 2, 3, 4)
    bi = jnp.arange(B)[:, None, None, None]
    hi = jnp.arange(H)[None, None, :, None]
    scale = D ** -0.5
    chunks_per_block = MOBA_BLOCK // MOBA_QCHUNK

    def step(args):
        qc, ic, c = args
        blk = c // chunks_per_block
        kg = kb[bi, hi, ic]
        vg = vb[bi, hi, ic]
        s_sel = jnp.einsum('bqhd,bqhjkd->bqhjk', qc, kg).astype(jnp.float32) * scale
        valid = jnp.arange(kt) < blk
        s_sel = jnp.where(valid[:, None], s_sel, -jnp.inf)
        k_own = lax.dynamic_index_in_dim(kb, blk, axis=2, keepdims=False)
        v_own = lax.dynamic_index_in_dim(vb, blk, axis=2, keepdims=False)
        s_own = jnp.einsum('bqhd,bhkd->bqhk', qc, k_own).astype(jnp.float32) * scale
        q_pos = c * MOBA_QCHUNK + jnp.arange(MOBA_QCHUNK)
        k_pos = blk * MOBA_BLOCK + jnp.arange(MOBA_BLOCK)
        causal = k_pos[None, :] <= q_pos[:, None]
        s_own = jnp.where(causal[None, :, None, :], s_own, -jnp.inf)
        s = jnp.concatenate([s_sel.reshape(B, MOBA_QCHUNK, H, kt * MOBA_BLOCK), s_own], axis=-1)
        p = jax.nn.softmax(s, axis=-1).astype(v.dtype)
        p_sel = p[..., :kt * MOBA_BLOCK].reshape(B, MOBA_QCHUNK, H, kt, MOBA_BLOCK)
        p_own = p[..., kt * MOBA_BLOCK:]
        return (jnp.einsum('bqhjk,bqhjkd->bqhd', p_sel, vg)
                + jnp.einsum('bqhk,bhkd->bqhd', p_own, v_own))

    o = lax.map(step, (qc_all, sel_all, jnp.arange(nc)))
    o = o.transpose(1, 0, 2, 3, 4).reshape(B, s_pad, H, D)[:, :S]
    return o.reshape(B, S, MOBA_WIDTH)


def moe_ffn(x, w_router, b_router, w_gate_up, b_gate_up, w_down, b_down):
    T, D = x.shape
    logits = x.astype(jnp.float32) @ w_router.astype(jnp.float32) + b_router.astype(jnp.float32)
    top_vals, top_idx = lax.top_k(logits, TOP_K)
    gates = jax.nn.softmax(top_vals, axis=-1)
    n_assign = T * TOP_K
    flat_e = top_idx.reshape(n_assign)
    flat_tok = jnp.repeat(jnp.arange(T, dtype=jnp.int32), TOP_K)
    flat_g = gates.reshape(n_assign)
    order = jnp.argsort(flat_e, stable=True)
    sorted_e = flat_e[order]
    counts = jnp.bincount(flat_e, length=N_EXPERTS)
    padded = (counts + MOE_BLOCK - 1) // MOE_BLOCK * MOE_BLOCK
    pad_end = jnp.cumsum(padded)
    pad_start = pad_end - padded
    start = jnp.cumsum(counts) - counts
    dest = pad_start[sorted_e] + jnp.arange(n_assign) - start[sorted_e]
    n_blocks = -(-n_assign // MOE_BLOCK) + N_EXPERTS
    n_rows = n_blocks * MOE_BLOCK
    row_tok = jnp.full((n_rows,), T, jnp.int32).at[dest].set(flat_tok[order])
    row_gate = jnp.zeros((n_rows,), jnp.float32).at[dest].set(flat_g[order])
    block_e = jnp.minimum(
        jnp.searchsorted(pad_end, jnp.arange(n_blocks) * MOE_BLOCK, side="right"), N_EXPERTS - 1)
    x_pad = jnp.concatenate([x, jnp.zeros((1, D), x.dtype)], axis=0)
    xb = x_pad[row_tok].reshape(n_blocks, MOE_BLOCK, D)

    def expert_block(args):
        xe, e = args
        gu = xe @ w_gate_up[e] + b_gate_up[e]
        g = jnp.minimum(gu[:, :D_FF], SWIGLU_LIMIT)
        u = jnp.clip(gu[:, D_FF:], -SWIGLU_LIMIT, SWIGLU_LIMIT)
        act = (u + 1.0) * (g * jax.nn.sigmoid(g * SWIGLU_ALPHA))
        return act @ w_down[e] + b_down[e]

    y = lax.map(expert_block, (xb, block_e)).reshape(n_rows, D)
    y = (y.astype(jnp.float32) * row_gate[:, None]).astype(x.dtype)
    return jnp.zeros((T + 1, D), x.dtype).at[row_tok].add(y)[:T]


def hybrid_layer(x, positions, w_in, g_q_a, w_q_b, g_kv_a, w_kv_b, g_out_mla, g_out_moba, w_o,
                 ln1_g, ln1_b, w_router, b_router, w_gate_up, b_gate_up, w_down, b_down,
                 ln2_g, ln2_b):
    B, S, D = x.shape
    h = x @ w_in
    o1 = Q_LORA
    o2 = o1 + KV_LORA
    o3 = o2 + QK_ROPE
    o4 = o3 + MOBA_WIDTH
    o5 = o4 + MOBA_WIDTH
    mla = mla_attention(h[..., :o1], h[..., o1:o2], h[..., o2:o3], positions,
                        g_q_a, w_q_b, g_kv_a, w_kv_b)
    mq = rope(h[..., o3:o4].reshape(B, S, MOBA_HEADS, MOBA_HEAD_DIM), positions)
    mk = rope(h[..., o4:o5].reshape(B, S, MOBA_HEADS, MOBA_HEAD_DIM), positions)
    mv = h[..., o5:].reshape(B, S, MOBA_HEADS, MOBA_HEAD_DIM)
    moba = moba_attention(mq, mk, mv)
    mixed = jnp.concatenate([rms_norm(mla, g_out_mla), rms_norm(moba, g_out_moba)], axis=-1) @ w_o
    x1 = layer_norm(DEEPNORM_ALPHA * x + mixed, ln1_g, ln1_b)
    ffn = moe_ffn(x1.reshape(B * S, D), w_router, b_router, w_gate_up, b_gate_up,
                  w_down, b_down).reshape(B, S, D)
    return layer_norm(DEEPNORM_ALPHA * x1 + ffn, ln2_g, ln2_b)


def setup_inputs(seed: int = 0) -> dict:
    key = jax.random.key(seed)
    ks = jax.random.split(key, 21)
    f32 = jnp.float32
    L = DEPTH

    def nrm(k, shape, scale):
        return jax.random.normal(k, shape, f32) * scale

    return {
        "x": jax.random.normal(ks[0], (BATCH, SEQ, D_MODEL), f32),
        "positions": jnp.broadcast_to(jnp.arange(SEQ, dtype=jnp.int32), (BATCH, SEQ)),
        "w_in": nrm(ks[1], (L, D_MODEL, IN_COLS), D_MODEL ** -0.5),
        "g_q_a": 1.0 + nrm(ks[2], (L, Q_LORA), 0.01),
        "w_q_b": nrm(ks[3], (L, Q_LORA, MLA_HEADS * (QK_NOPE + QK_ROPE)), Q_LORA ** -0.5),
        "g_kv_a": 1.0 + nrm(ks[4], (L, KV_LORA), 0.01),
        "w_kv_b": nrm(ks[5], (L, KV_LORA, MLA_HEADS * (QK_NOPE + V_HEAD)), KV_LORA ** -0.5),
        "g_out_mla": 1.0 + nrm(ks[6], (L, MLA_WIDTH), 0.01),
        "g_out_moba": 1.0 + nrm(ks[7], (L, MOBA_WIDTH), 0.01),
        "w_o": nrm(ks[8], (L, MIX_WIDTH, D_MODEL), DEEPNORM_BETA * MIX_WIDTH ** -0.5),
        "ln1_g": 1.0 + nrm(ks[9], (L, D_MODEL), 0.01),
        "ln1_b": nrm(ks[10], (L, D_MODEL), 0.01),
        "w_router": nrm(ks[11], (L, D_MODEL, N_EXPERTS), D_MODEL ** -0.5),
        "b_router": nrm(ks[12], (L, N_EXPERTS), 0.01),
        "w_gate_up": nrm(ks[13], (L, N_EXPERTS, D_MODEL, 2 * D_FF), D_MODEL ** -0.5),
        "b_gate_up": nrm(ks[14], (L, N_EXPERTS, 2 * D_FF), 0.01),
        "w_down": nrm(ks[15], (L, N_EXPERTS, D_FF, D_MODEL), DEEPNORM_BETA * D_FF ** -0.5),
        "b_down": nrm(ks[16], (L, N_EXPERTS, D_MODEL), 0.01),
        "ln2_g": 1.0 + nrm(ks[17], (L, D_MODEL), 0.01),
        "ln2_b": nrm(ks[18], (L, D_MODEL), 0.01),
    }


def reference(x, positions, w_in, g_q_a, w_q_b, g_kv_a, w_kv_b, g_out_mla, g_out_moba, w_o,
              ln1_g, ln1_b, w_router, b_router, w_gate_up, b_gate_up, w_down, b_down,
              ln2_g, ln2_b):
    h = x
    for l in range(DEPTH):
        h = hybrid_layer(h, positions, w_in[l], g_q_a[l], w_q_b[l], g_kv_a[l], w_kv_b[l],
                         g_out_mla[l], g_out_moba[l], w_o[l], ln1_g[l], ln1_b[l],
                         w_router[l], b_router[l], w_gate_up[l], b_gate_up[l],
                         w_down[l], b_down[l], ln2_g[l], ln2_b[l])
    return h
```

```python
import functools

import numpy as np
import jax
import jax.numpy as jnp
from jax import lax
from jax.experimental import pallas as pl
from jax.experimental.pallas import tpu as pltpu

D_MODEL = 2048
SEQ = 8192
MLA_HEADS = 8
QK_NOPE = 128
QK_ROPE = 64
V_HEAD = 128
Q_LORA = 512
KV_LORA = 256
MLA_WIDTH = MLA_HEADS * V_HEAD
MOBA_HEADS = 8
MOBA_HEAD_DIM = 128
MOBA_WIDTH = MOBA_HEADS * MOBA_HEAD_DIM
MOBA_BLOCK = 256
MOBA_TOPK = 3
N_MOBA_BLOCKS = SEQ // MOBA_BLOCK
ROPE_THETA = 10000.0
N_EXPERTS = 32
TOP_K = 4
D_FF = D_MODEL
SWIGLU_LIMIT = 7.0
SWIGLU_ALPHA = 1.702
MOE_BLOCK = 256
DEPTH = 1
DEEPNORM_ALPHA = float((2 * DEPTH) ** 0.25)
RMS_EPS = 1e-6
LN_EPS = 1e-5

LANES = 128
HEAD_SLOT = 2 * LANES
H_COLS = 4096
MASK_BIAS = -float(2 ** 17)
V7X_VMEM_LIMIT = 56 * 1024 * 1024

F32 = jnp.float32
BF16 = jnp.bfloat16


def _params(*semantics):
    return pltpu.CompilerParams(dimension_semantics=semantics, vmem_limit_bytes=V7X_VMEM_LIMIT)


def _rms(xf, g):
    return xf * lax.rsqrt(jnp.mean(xf * xf, axis=-1, keepdims=True) + RMS_EPS) * g


def _layer_norm(xf, g, b):
    mu = jnp.mean(xf, axis=-1, keepdims=True)
    xc = xf - mu
    var = jnp.mean(xc * xc, axis=-1, keepdims=True)
    return xc * lax.rsqrt(var + LN_EPS) * g + b


def _nt_dot(a, b, **kw):
    return lax.dot_general(a, b, (((1,), (1,)), ((), ())), preferred_element_type=F32, **kw)


def _topk_rank(vals, row_idx):
    n = vals.shape[0]
    rank = jnp.zeros(vals.shape, jnp.int32)
    for jp in range(n):
        vj = vals[jp:jp + 1, :]
        ahead = (vj > vals) | ((vj == vals) & (jp < row_idx))
        rank = rank + ahead.astype(jnp.int32)
    return rank


def _inproj_kernel(x_ref, w_ref, o_ref, xb_ref):
    @pl.when(pl.program_id(1) == 0)
    def _():
        xb_ref[...] = x_ref[...].astype(BF16)

    o_ref[...] = jnp.dot(xb_ref[...], w_ref[...], preferred_element_type=F32).astype(o_ref.dtype)


def _inproj(x, w_bf16, *, tm=1024, tn=512):
    m, k = x.shape
    n = w_bf16.shape[1]
    return pl.pallas_call(
        _inproj_kernel,
        out_shape=jax.ShapeDtypeStruct((m, n), BF16),
        grid=(m // tm, n // tn),
        in_specs=[pl.BlockSpec((tm, k), lambda i, j: (i, 0)),
                  pl.BlockSpec((k, tn), lambda i, j: (0, j))],
        out_specs=pl.BlockSpec((tm, tn), lambda i, j: (i, j)),
        scratch_shapes=[pltpu.VMEM((tm, k), BF16)],
        compiler_params=_params("parallel", "arbitrary"),
        name="inproj",
    )(x, w_bf16)


def _rope_tables_mla(pos_ref, invf_ref):
    ang = pos_ref[...].astype(F32) * invf_ref[...]
    lane = lax.broadcasted_iota(jnp.int32, ang.shape, 1)
    cos = jnp.cos(ang)
    sin = jnp.sin(ang)
    half = QK_ROPE // 2
    sin_a = jnp.where(lane < half, -sin, 0.0)
    sin_b = jnp.where((lane >= half) & (lane < QK_ROPE), sin, 0.0)
    return cos, sin_a, sin_b


def _rope_mla(t, cos, sin_a, sin_b):
    return (t * cos + pltpu.roll(t, LANES - QK_ROPE // 2, 1) * sin_a
            + pltpu.roll(t, QK_ROPE // 2, 1) * sin_b)


def _rope_tables_moba(pos_ref, invf_ref):
    ang = pos_ref[...].astype(F32) * invf_ref[...]
    lane = lax.broadcasted_iota(jnp.int32, ang.shape, 1)
    cos = jnp.cos(ang)
    sin = jnp.sin(ang)
    return cos, jnp.where(lane < MOBA_HEAD_DIM // 2, -sin, sin)


def _rope_moba(t, cos, sin_signed):
    return t * cos + pltpu.roll(t, MOBA_HEAD_DIM // 2, 1) * sin_signed


def _mla_prep_kernel(h_ref, pos_ref, invf_ref, gq_ref, gkv_ref, wqn_ref, wqr_ref, wkn_ref, wv_ref,
                     q_ref, k_ref, v_ref):
    scale = (QK_NOPE + QK_ROPE) ** -0.5
    cos, sin_a, sin_b = _rope_tables_mla(pos_ref, invf_ref)
    hq = h_ref[:, 0:Q_LORA].astype(F32)
    hkv = h_ref[:, Q_LORA:Q_LORA + KV_LORA].astype(F32)
    hkr = h_ref[:, Q_LORA + KV_LORA:Q_LORA + KV_LORA + LANES].astype(F32)

    qn = _rms(hq, gq_ref[...]).astype(BF16)
    q_nope = jnp.dot(qn, wqn_ref[...], preferred_element_type=F32)
    q_rope = jnp.dot(qn, wqr_ref[...], preferred_element_type=F32)
    kvn = _rms(hkv, gkv_ref[...]).astype(BF16)
    k_nope = jnp.dot(kvn, wkn_ref[...], preferred_element_type=F32)
    v_ref[...] = jnp.dot(kvn, wv_ref[...], preferred_element_type=F32).astype(BF16)
    kpe = _rope_mla(hkr, cos, sin_a, sin_b).astype(BF16)
    for h in range(MLA_HEADS):
        lo = h * HEAD_SLOT
        hs = slice(h * LANES, (h + 1) * LANES)
        q_ref[:, lo:lo + LANES] = (q_nope[:, hs] * scale).astype(BF16)
        q_ref[:, lo + LANES:lo + HEAD_SLOT] = (
            _rope_mla(q_rope[:, hs], cos, sin_a, sin_b) * scale).astype(BF16)
        k_ref[:, lo:lo + LANES] = k_nope[:, hs].astype(BF16)
        k_ref[:, lo + LANES:lo + HEAD_SLOT] = kpe


def _mla_prep(h, pos_col, invf, gq, gkv, wqn, wqr, wkn, wv, *, tm=512):
    s = h.shape[0]
    full = lambda a: pl.BlockSpec(a.shape, lambda i: (0,) * a.ndim)
    return pl.pallas_call(
        _mla_prep_kernel,
        out_shape=(jax.ShapeDtypeStruct((s, MLA_HEADS * HEAD_SLOT), BF16),
                   jax.ShapeDtypeStruct((s, MLA_HEADS * HEAD_SLOT), BF16),
                   jax.ShapeDtypeStruct((s, MLA_WIDTH), BF16)),
        grid=(s // tm,),
        in_specs=[pl.BlockSpec((tm, 1024), lambda i: (i, 0)),
                  pl.BlockSpec((tm, 1), lambda i: (i, 0)),
                  full(invf), full(gq), full(gkv), full(wqn), full(wqr), full(wkn), full(wv)],
        out_specs=(pl.BlockSpec((tm, MLA_HEADS * HEAD_SLOT), lambda i: (i, 0)),
                   pl.BlockSpec((tm, MLA_HEADS * HEAD_SLOT), lambda i: (i, 0)),
                   pl.BlockSpec((tm, MLA_WIDTH), lambda i: (i, 0))),
        compiler_params=_params("parallel"),
        name="mla_prep",
    )(h, pos_col, invf, gq, gkv, wqn, wqr, wkn, wv)


def _moba_kprep_kernel(h_ref, pos_ref, invf_ref, k_ref, kbar_ref, *, rows):
    cos, sin_signed = _rope_tables_moba(pos_ref, invf_ref)
    row = pl.program_id(0) * rows + lax.broadcasted_iota(jnp.int32, (rows, LANES), 0)
    lane = lax.broadcasted_iota(jnp.int32, (rows, LANES), 1)
    onehot = (lane == row // MOBA_BLOCK).astype(BF16)
    for h in range(MOBA_HEADS):
        lo = h * HEAD_SLOT
        kr = _rope_moba(h_ref[:, h * LANES:(h + 1) * LANES].astype(F32), cos, sin_signed)
        k_ref[:, lo:lo + LANES] = kr.astype(BF16)
        k_ref[:, lo + LANES:lo + HEAD_SLOT] = onehot
        for b in range(rows // MOBA_BLOCK):
            kbar_ref[b:b + 1, h * LANES:(h + 1) * LANES] = jnp.mean(
                kr[b * MOBA_BLOCK:(b + 1) * MOBA_BLOCK], axis=0, keepdims=True)


def _moba_kprep(h, pos_col, invf, *, rows=2048):
    s = h.shape[0]
    return pl.pallas_call(
        functools.partial(_moba_kprep_kernel, rows=rows),
        out_shape=(jax.ShapeDtypeStruct((s, MOBA_HEADS * HEAD_SLOT), BF16),
                   jax.ShapeDtypeStruct((s // MOBA_BLOCK, MOBA_WIDTH), F32)),
        grid=(s // rows,),
        in_specs=[pl.BlockSpec((rows, MOBA_WIDTH), lambda i: (i, 2)),
                  pl.BlockSpec((rows, 1), lambda i: (i, 0)),
                  pl.BlockSpec((1, LANES), lambda i: (0, 0))],
        out_specs=(pl.BlockSpec((rows, MOBA_HEADS * HEAD_SLOT), lambda i: (i, 0)),
                   pl.BlockSpec((rows // MOBA_BLOCK, MOBA_WIDTH), lambda i: (i, 0))),
        compiler_params=_params("parallel"),
        name="moba_kprep",
    )(h, pos_col, invf)


def _moba_qprep_kernel(h_ref, pos_ref, invf_ref, kbar_ref, q_ref, *, tm):
    scale = MOBA_HEAD_DIM ** -0.5
    cos, sin_signed = _rope_tables_moba(pos_ref, invf_ref)
    nb = N_MOBA_BLOCKS
    tok = pl.program_id(0) * tm + lax.broadcasted_iota(jnp.int32, (1, tm), 1)
    qblk = tok // MOBA_BLOCK
    blk = lax.broadcasted_iota(jnp.int32, (nb, 1), 0)
    past = blk < qblk
    own = blk == qblk
    for h in range(MOBA_HEADS):
        lo = h * HEAD_SLOT
        qr = _rope_moba(h_ref[:, h * LANES:(h + 1) * LANES].astype(F32), cos, sin_signed)
        gate = _nt_dot(kbar_ref[:, h * LANES:(h + 1) * LANES], qr,
                       precision=lax.Precision.HIGHEST)
        gate = jnp.where(past, gate, -jnp.inf)
        sel = past & (_topk_rank(gate, blk) < MOBA_TOPK)
        bias = jnp.where(sel | own, 0.0, MASK_BIAS)
        bias = jnp.concatenate([bias, jnp.zeros((LANES - nb, tm), F32)], axis=0)
        q_ref[:, lo:lo + LANES] = (qr * scale).astype(BF16)
        q_ref[:, lo + LANES:lo + HEAD_SLOT] = bias.T.astype(BF16)


def _moba_qprep(h, pos_col, invf, kbar, *, tm=512):
    s = h.shape[0]
    return pl.pallas_call(
        functools.partial(_moba_qprep_kernel, tm=tm),
        out_shape=jax.ShapeDtypeStruct((s, MOBA_HEADS * HEAD_SLOT), BF16),
        grid=(s // tm,),
        in_specs=[pl.BlockSpec((tm, MOBA_WIDTH), lambda i: (i, 1)),
                  pl.BlockSpec((tm, 1), lambda i: (i, 0)),
                  pl.BlockSpec((1, LANES), lambda i: (0, 0)),
                  pl.BlockSpec(kbar.shape, lambda i: (0, 0))],
        out_specs=pl.BlockSpec((tm, MOBA_HEADS * HEAD_SLOT), lambda i: (i, 0)),
        compiler_params=_params("parallel"),
        name="moba_qprep",
    )(h, pos_col, invf, kbar)


def _flash_kernel(q_ref, k_ref, v_ref, o_ref, m_ref, l_ref, acc_ref, *, tq):
    qi = pl.program_id(1)
    q = q_ref[...]
    m_ref[...] = jnp.full(m_ref.shape, -jnp.inf, F32)
    l_ref[...] = jnp.zeros(l_ref.shape, F32)
    acc_ref[...] = jnp.zeros(acc_ref.shape, F32)

    def step(j, diagonal):
        k = k_ref[pl.ds(pl.multiple_of(j * tq, tq), tq), :]
        v = v_ref[pl.ds(pl.multiple_of(j * tq, tq), tq), :]
        s = _nt_dot(q, k)
        if diagonal:
            r = lax.broadcasted_iota(jnp.int32, s.shape, 0)
            c = lax.broadcasted_iota(jnp.int32, s.shape, 1)
            s = jnp.where(c <= r, s, -jnp.inf)
        m_old = m_ref[...]
        m_new = jnp.maximum(m_old, jnp.max(s, axis=1, keepdims=True))
        alpha = jnp.exp(m_old - m_new)
        p = jnp.exp(s - m_new)
        l_ref[...] = alpha * l_ref[...] + jnp.sum(p, axis=1, keepdims=True)
        acc_ref[...] = alpha * acc_ref[...] + jnp.dot(p.astype(BF16), v,
                                                      preferred_element_type=F32)
        m_ref[...] = m_new

    def body(j, carry):
        step(j, False)
        return carry

    lax.fori_loop(0, qi, body, 0)
    step(qi, True)
    o_ref[...] = (acc_ref[...] / l_ref[...]).astype(o_ref.dtype)


def _flash(q, k, v, v_col0, n_heads, *, tq=512):
    s = q.shape[0]
    return pl.pallas_call(
        functools.partial(_flash_kernel, tq=tq),
        out_shape=jax.ShapeDtypeStruct((s, n_heads * LANES), BF16),
        grid=(n_heads, s // tq),
        in_specs=[pl.BlockSpec((tq, HEAD_SLOT), lambda h, i: (i, h)),
                  pl.BlockSpec((s, HEAD_SLOT), lambda h, i: (0, h)),
                  pl.BlockSpec((s, LANES), lambda h, i: (0, v_col0 + h))],
        out_specs=pl.BlockSpec((tq, LANES), lambda h, i: (i, h)),
        scratch_shapes=[pltpu.VMEM((tq, 1), F32), pltpu.VMEM((tq, 1), F32),
                        pltpu.VMEM((tq, LANES), F32)],
        compiler_params=_params("parallel", "parallel"),
        name="flash",
    )(q, k, v)


def _outproj_kernel(om_ref, ob_ref, gm_ref, gb_ref, wo_ref, x_ref, lg_ref, lb_ref, wrt_ref, br_ref,
                    x1_ref, x1b_ref, idx_ref, gate_ref):
    a = jnp.concatenate([_rms(om_ref[...].astype(F32), gm_ref[...]).astype(BF16),
                         _rms(ob_ref[...].astype(F32), gb_ref[...]).astype(BF16)], axis=1)
    mixed = jnp.dot(a, wo_ref[...], preferred_element_type=F32)
    x1 = _layer_norm(DEEPNORM_ALPHA * x_ref[...] + mixed, lg_ref[...], lb_ref[...])
    x1_ref[...] = x1
    x1b_ref[...] = x1.astype(BF16)

    logits = _nt_dot(wrt_ref[...], x1, precision=lax.Precision.HIGHEST) + br_ref[...]
    eidx = lax.broadcasted_iota(jnp.int32, (N_EXPERTS, 1), 0)
    rank = _topk_rank(logits, eidx)
    sel = rank < TOP_K
    mx = jnp.max(logits, axis=0, keepdims=True)
    p = jnp.where(sel, jnp.exp(logits - mx), 0.0)
    gates = p / jnp.sum(p, axis=0, keepdims=True)
    tm = logits.shape[1]
    idx_rows = [jnp.sum(jnp.where(rank == k, eidx, 0), axis=0, keepdims=True) for k in range(TOP_K)]
    gate_rows = [jnp.sum(jnp.where(rank == k, gates, 0.0), axis=0, keepdims=True)
                 for k in range(TOP_K)]
    idx_ref[...] = jnp.concatenate(idx_rows + [jnp.zeros((8 - TOP_K, tm), jnp.int32)], axis=0)
    gate_ref[...] = jnp.concatenate(gate_rows + [jnp.zeros((8 - TOP_K, tm), F32)], axis=0)


def _outproj(o_mla, o_moba, g_mla, g_moba, wo, x, ln_g, ln_b, wr_t, b_r, *, tm=256):
    s = x.shape[0]
    full = lambda a: pl.BlockSpec(a.shape, lambda i: (0,) * a.ndim)
    return pl.pallas_call(
        _outproj_kernel,
        out_shape=(jax.ShapeDtypeStruct((s, D_MODEL), F32),
                   jax.ShapeDtypeStruct((s, D_MODEL), BF16),
                   jax.ShapeDtypeStruct((8, s), jnp.int32),
                   jax.ShapeDtypeStruct((8, s), F32)),
        grid=(s // tm,),
        in_specs=[pl.BlockSpec((tm, MLA_WIDTH), lambda i: (i, 0)),
                  pl.BlockSpec((tm, MOBA_WIDTH), lambda i: (i, 0)),
                  full(g_mla), full(g_moba), full(wo),
                  pl.BlockSpec((tm, D_MODEL), lambda i: (i, 0)),
                  full(ln_g), full(ln_b), full(wr_t), full(b_r)],
        out_specs=(pl.BlockSpec((tm, D_MODEL), lambda i: (i, 0)),
                   pl.BlockSpec((tm, D_MODEL), lambda i: (i, 0)),
                   pl.BlockSpec((8, tm), lambda i: (0, i)),
                   pl.BlockSpec((8, tm), lambda i: (0, i))),
        compiler_params=_params("parallel"),
        name="outproj_router",
    )(o_mla, o_moba, g_mla, g_moba, wo, x, ln_g, ln_b, wr_t, b_r)


def _expert_kernel(be_ref, x_ref, wg_ref, wu_ref, bg_ref, bu_ref, wd_ref, bd_ref, rg_ref,
                   o_ref, act_ref, *, nf, tf):
    s = pl.program_id(1)

    @pl.when(s < nf)
    def _():
        x = x_ref[...]
        g = jnp.dot(x, wg_ref[...].astype(BF16), preferred_element_type=F32) + bg_ref[...]
        u = jnp.dot(x, wu_ref[...].astype(BF16), preferred_element_type=F32) + bu_ref[...]
        g = jnp.minimum(g, SWIGLU_LIMIT)
        u = jnp.clip(u, -SWIGLU_LIMIT, SWIGLU_LIMIT)
        act = (u + 1.0) * (g * (1.0 / (1.0 + jnp.exp(-SWIGLU_ALPHA * g))))
        act_ref[s] = act.astype(BF16)

    @pl.when(s >= nf)
    def _():
        acc = jnp.zeros(o_ref.shape, F32)
        for f in range(nf):
            acc = acc + jnp.dot(act_ref[f], wd_ref[f * tf:(f + 1) * tf, :].astype(BF16),
                                preferred_element_type=F32)
        o_ref[...] = (acc + bd_ref[...]) * rg_ref[...]


def _experts(block_e, xs, w_gate_up, b_gate_up, w_down, b_down, row_gate, *, tf=256, tn=256):
    n_rows = xs.shape[0]
    nf = D_FF // tf
    nn = D_MODEL // tn
    fi = lambda s: jnp.minimum(s, nf - 1)
    ni = lambda s: jnp.maximum(s - nf, 0)
    grid_spec = pltpu.PrefetchScalarGridSpec(
        num_scalar_prefetch=1,
        grid=(n_rows // MOE_BLOCK, nf + nn),
        in_specs=[
            pl.BlockSpec((MOE_BLOCK, D_MODEL), lambda b, s, be: (b, 0)),
            pl.BlockSpec((None, D_MODEL, tf), lambda b, s, be: (be[b], 0, fi(s))),
            pl.BlockSpec((None, D_MODEL, tf), lambda b, s, be: (be[b], 0, nf + fi(s))),
            pl.BlockSpec((None, 1, tf), lambda b, s, be: (be[b], 0, fi(s))),
            pl.BlockSpec((None, 1, tf), lambda b, s, be: (be[b], 0, nf + fi(s))),
            pl.BlockSpec((None, D_FF, tn), lambda b, s, be: (be[b], 0, ni(s))),
            pl.BlockSpec((None, 1, tn), lambda b, s, be: (be[b], 0, ni(s))),
            pl.BlockSpec((MOE_BLOCK, 1), lambda b, s, be: (b, 0)),
        ],
        out_specs=pl.BlockSpec((MOE_BLOCK, tn), lambda b, s, be: (b, ni(s))),
        scratch_shapes=[pltpu.VMEM((nf, MOE_BLOCK, tf), BF16)],
    )
    return pl.pallas_call(
        functools.partial(_expert_kernel, nf=nf, tf=tf),
        out_shape=jax.ShapeDtypeStruct((n_rows, D_MODEL), F32),
        grid_spec=grid_spec,
        compiler_params=_params("parallel", "arbitrary"),
        name="experts",
    )(block_e, xs, w_gate_up, w_gate_up, b_gate_up, b_gate_up, w_down, b_down, row_gate)


def _final_kernel(x1_ref, f_ref, g_ref, b_ref, o_ref):
    o_ref[...] = _layer_norm(DEEPNORM_ALPHA * x1_ref[...] + f_ref[...], g_ref[...], b_ref[...])


def _final(x1, ffn, g, b, *, tm=512):
    s = x1.shape[0]
    return pl.pallas_call(
        _final_kernel,
        out_shape=jax.ShapeDtypeStruct((s, D_MODEL), F32),
        grid=(s // tm,),
        in_specs=[pl.BlockSpec((tm, D_MODEL), lambda i: (i, 0)),
                  pl.BlockSpec((tm, D_MODEL), lambda i: (i, 0)),
                  pl.BlockSpec((1, D_MODEL), lambda i: (0, 0)),
                  pl.BlockSpec((1, D_MODEL), lambda i: (0, 0))],
        out_specs=pl.BlockSpec((tm, D_MODEL), lambda i: (i, 0)),
        compiler_params=_params("parallel"),
        name="final_ln",
    )(x1, ffn, g, b)


def _inv_freq(dim, width):
    half = dim // 2
    lane = np.arange(LANES)
    f = ROPE_THETA ** (-(lane % half).astype(np.float32) * 2.0 / dim)
    return jnp.asarray(np.where(lane < width, f, 0.0).astype(np.float32)).reshape(1, LANES)


def _layer(x, pos, w_in, g_q_a, w_q_b, g_kv_a, w_kv_b, g_out_mla, g_out_moba, w_o, ln1_g, ln1_b,
           w_router, b_router, w_gate_up, b_gate_up, w_down, b_down, ln2_g, ln2_b):
    s = x.shape[0]
    mla_cols = Q_LORA + KV_LORA + QK_ROPE
    w_in_p = jnp.concatenate(
        [w_in[:, :mla_cols], jnp.zeros((D_MODEL, 1024 - mla_cols), w_in.dtype), w_in[:, mla_cols:]],
        axis=1).astype(BF16)
    wq = w_q_b.reshape(Q_LORA, MLA_HEADS, QK_NOPE + QK_ROPE)
    wqn = wq[:, :, :QK_NOPE].reshape(Q_LORA, MLA_HEADS * LANES).astype(BF16)
    wqr = jnp.pad(wq[:, :, QK_NOPE:], ((0, 0), (0, 0), (0, LANES - QK_ROPE))).reshape(
        Q_LORA, MLA_HEADS * LANES).astype(BF16)
    wkv = w_kv_b.reshape(KV_LORA, MLA_HEADS, QK_NOPE + V_HEAD)
    wkn = wkv[:, :, :QK_NOPE].reshape(KV_LORA, MLA_HEADS * LANES).astype(BF16)
    wv = wkv[:, :, QK_NOPE:].reshape(KV_LORA, MLA_WIDTH).astype(BF16)
    pos_col = pos.reshape(s, 1)
    row = lambda a: a.reshape(1, -1)

    h = _inproj(x, w_in_p)
    q_mla, k_mla, v_mla = _mla_prep(h, pos_col, _inv_freq(QK_ROPE, QK_ROPE), row(g_q_a),
                                    row(g_kv_a), wqn, wqr, wkn, wv)
    invf_moba = _inv_freq(MOBA_HEAD_DIM, LANES)
    k_moba, kbar = _moba_kprep(h, pos_col, invf_moba)
    q_moba = _moba_qprep(h, pos_col, invf_moba, kbar)
    o_mla = _flash(q_mla, k_mla, v_mla, 0, MLA_HEADS)
    o_moba = _flash(q_moba, k_moba, h, 3 * MOBA_WIDTH // LANES, MOBA_HEADS)

    x1, x1b, idx_t, gate_t = _outproj(o_mla, o_moba, row(g_out_mla), row(g_out_moba),
                                      w_o.astype(BF16), x, row(ln1_g), row(ln1_b),
                                      w_router.T, b_router.reshape(N_EXPERTS, 1))

    n_assign = s * TOP_K
    flat_e = idx_t[:TOP_K].T.reshape(n_assign)
    flat_g = gate_t[:TOP_K].T.reshape(n_assign)
    flat_tok = jnp.repeat(jnp.arange(s, dtype=jnp.int32), TOP_K)
    order = jnp.argsort(flat_e, stable=True)
    sorted_e = flat_e[order]
    counts = jnp.bincount(flat_e, length=N_EXPERTS)
    padded = (counts + MOE_BLOCK - 1) // MOE_BLOCK * MOE_BLOCK
    pad_end = jnp.cumsum(padded)
    pad_start = pad_end - padded
    start = jnp.cumsum(counts) - counts
    dest = pad_start[sorted_e] + jnp.arange(n_assign) - start[sorted_e]
    n_blocks = n_assign // MOE_BLOCK + N_EXPERTS
    n_rows = n_blocks * MOE_BLOCK
    row_tok = jnp.full((n_rows,), s, jnp.int32).at[dest].set(flat_tok[order])
    row_gate = jnp.zeros((n_rows,), F32).at[dest].set(flat_g[order])
    block_e = jnp.minimum(
        jnp.searchsorted(pad_end, jnp.arange(n_blocks) * MOE_BLOCK, side="right"),
        N_EXPERTS - 1).astype(jnp.int32)
    xs = jnp.concatenate([x1b, jnp.zeros((1, D_MODEL), BF16)], axis=0)[row_tok]
    ys = _experts(block_e, xs, w_gate_up, b_gate_up.reshape(N_EXPERTS, 1, 2 * D_FF), w_down,
                  b_down.reshape(N_EXPERTS, 1, D_MODEL), row_gate.reshape(n_rows, 1))
    ffn = jnp.zeros((s + 1, D_MODEL), F32).at[row_tok].add(ys)[:s]
    return _final(x1, ffn, row(ln2_g), row(ln2_b))


def kernel(x, positions, w_in, g_q_a, w_q_b, g_kv_a, w_kv_b, g_out_mla, g_out_moba, w_o, ln1_g, ln1_b,
           w_router, b_router, w_gate_up, b_gate_up, w_down, b_down, ln2_g, ln2_b):
    b, s, d = x.shape
    assert b == 1 and s == SEQ and d == D_MODEL and w_in.shape[0] == DEPTH
    hcur = x[0]
    for l in range(DEPTH):
        hcur = _layer(hcur, positions[0], w_in[l], g_q_a[l], w_q_b[l], g_kv_a[l], w_kv_b[l],
                      g_out_mla[l], g_out_moba[l], w_o[l], ln1_g[l], ln1_b[l], w_router[l],
                      b_router[l], w_gate_up[l], b_gate_up[l], w_down[l], b_down[l], ln2_g[l],
                      ln2_b[l])
    return hcur[None]
```

```python
import functools

import numpy as np
import jax
import jax.numpy as jnp
from jax import lax
from jax.experimental import pallas as pl
from jax.experimental.pallas import tpu as pltpu

D_MODEL = 2048
SEQ = 8192
MLA_HEADS = 8
QK_NOPE = 128
QK_ROPE = 64
V_HEAD = 128
Q_LORA = 512
KV_LORA = 256
MLA_WIDTH = MLA_HEADS * V_HEAD
MOBA_HEADS = 8
MOBA_HEAD_DIM = 128
MOBA_WIDTH = MOBA_HEADS * MOBA_HEAD_DIM
MOBA_BLOCK = 256
MOBA_TOPK = 3
N_MOBA_BLOCKS = SEQ // MOBA_BLOCK
ROPE_THETA = 10000.0
N_EXPERTS = 32
TOP_K = 4
D_FF = D_MODEL
SWIGLU_LIMIT = 7.0
SWIGLU_ALPHA = 1.702
MOE_BLOCK = 256
MOE_ROWS = 1280
COMBINE_ROWS = 256
DEPTH = 1
DEEPNORM_ALPHA = float((2 * DEPTH) ** 0.25)
RMS_EPS = 1e-6
LN_EPS = 1e-5

LANES = 128
HEAD_SLOT = 2 * LANES
H_COLS = 4096
MASK_BIAS = -float(2 ** 17)
V7X_VMEM_LIMIT = 56 * 1024 * 1024

F32 = jnp.float32
BF16 = jnp.bfloat16


def _params(*semantics):
    return pltpu.CompilerParams(dimension_semantics=semantics, vmem_limit_bytes=V7X_VMEM_LIMIT)


def _rms(xf, g):
    return xf * lax.rsqrt(jnp.mean(xf * xf, axis=-1, keepdims=True) + RMS_EPS) * g


def _layer_norm(xf, g, b):
    mu = jnp.mean(xf, axis=-1, keepdims=True)
    xc = xf - mu
    var = jnp.mean(xc * xc, axis=-1, keepdims=True)
    return xc * lax.rsqrt(var + LN_EPS) * g + b


def _nt_dot(a, b, **kw):
    return lax.dot_general(a, b, (((1,), (1,)), ((), ())), preferred_element_type=F32, **kw)


def _topk_rank(vals, row_idx):
    n = vals.shape[0]
    rank = jnp.zeros(vals.shape, jnp.int32)
    for jp in range(n):
        vj = vals[jp:jp + 1, :]
        ahead = (vj > vals) | ((vj == vals) & (jp < row_idx))
        rank = rank + ahead.astype(jnp.int32)
    return rank


def _inproj_kernel(x_ref, w_ref, o_ref, xb_ref):
    @pl.when(pl.program_id(1) == 0)
    def _():
        xb_ref[...] = x_ref[...].astype(BF16)

    o_ref[...] = jnp.dot(xb_ref[...], w_ref[...], preferred_element_type=F32).astype(o_ref.dtype)


def _inproj(x, w_bf16, *, tm=1024, tn=512):
    m, k = x.shape
    n = w_bf16.shape[1]
    return pl.pallas_call(
        _inproj_kernel,
        out_shape=jax.ShapeDtypeStruct((m, n), BF16),
        grid=(m // tm, n // tn),
        in_specs=[pl.BlockSpec((tm, k), lambda i, j: (i, 0)),
                  pl.BlockSpec((k, tn), lambda i, j: (0, j))],
        out_specs=pl.BlockSpec((tm, tn), lambda i, j: (i, j)),
        scratch_shapes=[pltpu.VMEM((tm, k), BF16)],
        compiler_params=_params("parallel", "arbitrary"),
        name="inproj",
    )(x, w_bf16)


def _rope_tables_mla(pos_ref, invf_ref):
    ang = pos_ref[...].astype(F32) * invf_ref[...]
    lane = lax.broadcasted_iota(jnp.int32, ang.shape, 1)
    cos = jnp.cos(ang)
    sin = jnp.sin(ang)
    half = QK_ROPE // 2
    sin_a = jnp.where(lane < half, -sin, 0.0)
    sin_b = jnp.where((lane >= half) & (lane < QK_ROPE), sin, 0.0)
    return cos, sin_a, sin_b


def _rope_mla(t, cos, sin_a, sin_b):
    return (t * cos + pltpu.roll(t, LANES - QK_ROPE // 2, 1) * sin_a
            + pltpu.roll(t, QK_ROPE // 2, 1) * sin_b)


def _rope_tables_moba(pos_ref, invf_ref):
    ang = pos_ref[...].astype(F32) * invf_ref[...]
    lane = lax.broadcasted_iota(jnp.int32, ang.shape, 1)
    cos = jnp.cos(ang)
    sin = jnp.sin(ang)
    return cos, jnp.where(lane < MOBA_HEAD_DIM // 2, -sin, sin)


def _rope_moba(t, cos, sin_signed):
    return t * cos + pltpu.roll(t, MOBA_HEAD_DIM // 2, 1) * sin_signed


def _mla_prep_kernel(h_ref, pos_ref, invf_ref, gq_ref, gkv_ref, wqn_ref, wqr_ref, wkn_ref, wv_ref,
                     q_ref, k_ref, v_ref):
    scale = (QK_NOPE + QK_ROPE) ** -0.5
    cos, sin_a, sin_b = _rope_tables_mla(pos_ref, invf_ref)
    hq = h_ref[:, 0:Q_LORA].astype(F32)
    hkv = h_ref[:, Q_LORA:Q_LORA + KV_LORA].astype(F32)
    hkr = h_ref[:, Q_LORA + KV_LORA:Q_LORA + KV_LORA + LANES].astype(F32)

    qn = _rms(hq, gq_ref[...]).astype(BF16)
    q_nope = jnp.dot(qn, wqn_ref[...], preferred_element_type=F32)
    q_rope = jnp.dot(qn, wqr_ref[...], preferred_element_type=F32)
    kvn = _rms(hkv, gkv_ref[...]).astype(BF16)
    k_nope = jnp.dot(kvn, wkn_ref[...], preferred_element_type=F32)
    v_ref[...] = jnp.dot(kvn, wv_ref[...], preferred_element_type=F32).astype(BF16)
    kpe = _rope_mla(hkr, cos, sin_a, sin_b).astype(BF16)
    for h in range(MLA_HEADS):
        lo = h * HEAD_SLOT
        hs = slice(h * LANES, (h + 1) * LANES)
        q_ref[:, lo:lo + LANES] = (q_nope[:, hs] * scale).astype(BF16)
        q_ref[:, lo + LANES:lo + HEAD_SLOT] = (
            _rope_mla(q_rope[:, hs], cos, sin_a, sin_b) * scale).astype(BF16)
        k_ref[:, lo:lo + LANES] = k_nope[:, hs].astype(BF16)
        k_ref[:, lo + LANES:lo + HEAD_SLOT] = kpe


def _mla_prep(h, pos_col, invf, gq, gkv, wqn, wqr, wkn, wv, *, tm=512):
    s = h.shape[0]
    full = lambda a: pl.BlockSpec(a.shape, lambda i: (0,) * a.ndim)
    return pl.pallas_call(
        _mla_prep_kernel,
        out_shape=(jax.ShapeDtypeStruct((s, MLA_HEADS * HEAD_SLOT), BF16),
                   jax.ShapeDtypeStruct((s, MLA_HEADS * HEAD_SLOT), BF16),
                   jax.ShapeDtypeStruct((s, MLA_WIDTH), BF16)),
        grid=(s // tm,),
        in_specs=[pl.BlockSpec((tm, 1024), lambda i: (i, 0)),
                  pl.BlockSpec((tm, 1), lambda i: (i, 0)),
                  full(invf), full(gq), full(gkv), full(wqn), full(wqr), full(wkn), full(wv)],
        out_specs=(pl.BlockSpec((tm, MLA_HEADS * HEAD_SLOT), lambda i: (i, 0)),
                   pl.BlockSpec((tm, MLA_HEADS * HEAD_SLOT), lambda i: (i, 0)),
                   pl.BlockSpec((tm, MLA_WIDTH), lambda i: (i, 0))),
        compiler_params=_params("parallel"),
        name="mla_prep",
    )(h, pos_col, invf, gq, gkv, wqn, wqr, wkn, wv)


def _moba_kprep_kernel(h_ref, pos_ref, invf_ref, k_ref, kbar_ref, *, rows):
    cos, sin_signed = _rope_tables_moba(pos_ref, invf_ref)
    row = pl.program_id(0) * rows + lax.broadcasted_iota(jnp.int32, (rows, LANES), 0)
    lane = lax.broadcasted_iota(jnp.int32, (rows, LANES), 1)
    onehot = (lane == row // MOBA_BLOCK).astype(BF16)
    for h in range(MOBA_HEADS):
        lo = h * HEAD_SLOT
        kr = _rope_moba(h_ref[:, h * LANES:(h + 1) * LANES].astype(F32), cos, sin_signed)
        k_ref[:, lo:lo + LANES] = kr.astype(BF16)
        k_ref[:, lo + LANES:lo + HEAD_SLOT] = onehot
        for b in range(rows // MOBA_BLOCK):
            kbar_ref[b:b + 1, h * LANES:(h + 1) * LANES] = jnp.mean(
                kr[b * MOBA_BLOCK:(b + 1) * MOBA_BLOCK], axis=0, keepdims=True)


def _moba_kprep(h, pos_col, invf, *, rows=2048):
    s = h.shape[0]
    return pl.pallas_call(
        functools.partial(_moba_kprep_kernel, rows=rows),
        out_shape=(jax.ShapeDtypeStruct((s, MOBA_HEADS * HEAD_SLOT), BF16),
                   jax.ShapeDtypeStruct((s // MOBA_BLOCK, MOBA_WIDTH), F32)),
        grid=(s // rows,),
        in_specs=[pl.BlockSpec((rows, MOBA_WIDTH), lambda i: (i, 2)),
                  pl.BlockSpec((rows, 1), lambda i: (i, 0)),
                  pl.BlockSpec((1, LANES), lambda i: (0, 0))],
        out_specs=(pl.BlockSpec((rows, MOBA_HEADS * HEAD_SLOT), lambda i: (i, 0)),
                   pl.BlockSpec((rows // MOBA_BLOCK, MOBA_WIDTH), lambda i: (i, 0))),
        compiler_params=_params("parallel"),
        name="moba_kprep",
    )(h, pos_col, invf)


def _moba_qprep_kernel(h_ref, pos_ref, invf_ref, kbar_ref, q_ref, *, tm):
    scale = MOBA_HEAD_DIM ** -0.5
    cos, sin_signed = _rope_tables_moba(pos_ref, invf_ref)
    nb = N_MOBA_BLOCKS
    tok = pl.program_id(0) * tm + lax.broadcasted_iota(jnp.int32, (1, tm), 1)
    qblk = tok // MOBA_BLOCK
    blk = lax.broadcasted_iota(jnp.int32, (nb, 1), 0)
    past = blk < qblk
    own = blk == qblk
    for h in range(MOBA_HEADS):
        lo = h * HEAD_SLOT
        qr = _rope_moba(h_ref[:, h * LANES:(h + 1) * LANES].astype(F32), cos, sin_signed)
        gate = _nt_dot(kbar_ref[:, h * LANES:(h + 1) * LANES], qr,
                       precision=lax.Precision.HIGHEST)
        gate = jnp.where(past, gate, -jnp.inf)
        sel = past & (_topk_rank(gate, blk) < MOBA_TOPK)
        bias = jnp.where(sel | own, 0.0, MASK_BIAS)
        bias = jnp.concatenate([bias, jnp.zeros((LANES - nb, tm), F32)], axis=0)
        q_ref[:, lo:lo + LANES] = (qr * scale).astype(BF16)
        q_ref[:, lo + LANES:lo + HEAD_SLOT] = bias.T.astype(BF16)


def _moba_qprep(h, pos_col, invf, kbar, *, tm=512):
    s = h.shape[0]
    return pl.pallas_call(
        functools.partial(_moba_qprep_kernel, tm=tm),
        out_shape=jax.ShapeDtypeStruct((s, MOBA_HEADS * HEAD_SLOT), BF16),
        grid=(s // tm,),
        in_specs=[pl.BlockSpec((tm, MOBA_WIDTH), lambda i: (i, 1)),
                  pl.BlockSpec((tm, 1), lambda i: (i, 0)),
                  pl.BlockSpec((1, LANES), lambda i: (0, 0)),
                  pl.BlockSpec(kbar.shape, lambda i: (0, 0))],
        out_specs=pl.BlockSpec((tm, MOBA_HEADS * HEAD_SLOT), lambda i: (i, 0)),
        compiler_params=_params("parallel"),
        name="moba_qprep",
    )(h, pos_col, invf, kbar)


def _flash_kernel(qa_ref, qb_ref, k_ref, v_ref, o_ref, q2_ref, m2_ref, acc2_ref, *, tq, nq):
    p_idx = pl.program_id(1)
    nc = tq // LANES
    q2_ref[0] = qa_ref[...]
    q2_ref[1] = qb_ref[...]
    m2_ref[...] = jnp.full(m2_ref.shape, -jnp.inf, F32)
    acc2_ref[...] = jnp.zeros(acc2_ref.shape, F32)
    ones = jnp.ones((tq, LANES), BF16)

    def kv_rows(j):
        return pl.ds(pl.multiple_of(j * tq, tq), tq)

    def scores(sel, j):
        return _nt_dot(q2_ref[sel], k_ref[kv_rows(j), :])

    def consume(sel, j, s):
        v = v_ref[kv_rows(j), :]
        chunks = [s[:, c * LANES:(c + 1) * LANES] for c in range(nc)]
        part = functools.reduce(jnp.maximum, chunks)
        m_old = m2_ref[sel]
        m_new = jnp.maximum(m_old, jnp.max(part, axis=1, keepdims=True))
        alpha = jnp.exp(m_old - m_new)
        p = jnp.concatenate([jnp.exp(ch - m_new).astype(BF16) for ch in chunks], axis=1)
        pv = jnp.dot(p, jnp.concatenate([v, ones], axis=1), preferred_element_type=F32)
        acc2_ref[sel] = jnp.concatenate([alpha, alpha], axis=1) * acc2_ref[sel] + pv
        m2_ref[sel] = m_new

    def diagonal(s):
        r = lax.broadcasted_iota(jnp.int32, s.shape, 0)
        c = lax.broadcasted_iota(jnp.int32, s.shape, 1)
        return jnp.where(c <= r, s, -jnp.inf)

    steps = []
    for t in range(nq - 1):
        sel = (t >= p_idx).astype(jnp.int32)
        steps.append((sel, t - sel * p_idx, False))
    steps.append((0, p_idx, True))
    steps.append((1, nq - 1 - p_idx, True))

    s_next = scores(steps[0][0], steps[0][1])
    for t, (sel, j, diag) in enumerate(steps):
        s = s_next
        if t + 1 < len(steps):
            s_next = scores(steps[t + 1][0], steps[t + 1][1])
        consume(sel, j, diagonal(s) if diag else s)
    for sel in range(2):
        o_ref[sel] = (acc2_ref[sel, :, :LANES] / acc2_ref[sel, :, LANES:]).astype(o_ref.dtype)


def _flash(q, k, v, v_col0, n_heads, *, tq=512):
    s = q.shape[0]
    nq = s // tq
    return pl.pallas_call(
        functools.partial(_flash_kernel, tq=tq, nq=nq),
        out_shape=jax.ShapeDtypeStruct((2, s // 2, n_heads * LANES), BF16),
        grid=(n_heads, nq // 2),
        in_specs=[pl.BlockSpec((tq, HEAD_SLOT), lambda h, p: (p, h)),
                  pl.BlockSpec((tq, HEAD_SLOT), lambda h, p: (nq - 1 - p, h)),
                  pl.BlockSpec((s, HEAD_SLOT), lambda h, p: (0, h)),
                  pl.BlockSpec((s, LANES), lambda h, p: (0, v_col0 + h))],
        out_specs=pl.BlockSpec((2, tq, LANES), lambda h, p: (0, p, h)),
        scratch_shapes=[pltpu.VMEM((2, tq, HEAD_SLOT), BF16), pltpu.VMEM((2, tq, LANES), F32),
                        pltpu.VMEM((2, tq, 2 * LANES), F32)],
        compiler_params=_params("parallel", "parallel"),
        name="flash",
    )(q, q, k, v)


def _flash_row_block(i, tm, s, tq=512):
    per_tile = tq // tm
    tile = i // per_tile
    sub = i % per_tile
    nq = s // tq
    hi = (tile >= nq // 2).astype(jnp.int32)
    return hi, jnp.where(hi == 1, nq - 1 - tile, tile) * per_tile + sub


def _outproj_kernel(om_ref, ob_ref, gm_ref, gb_ref, wo_ref, x_ref, lg_ref, lb_ref, wrt_ref, br_ref,
                    x1_ref, idx_ref, pos_ref, gate_ref, cnt_ref):
    @pl.when(pl.program_id(0) == 0)
    def _():
        cnt_ref[...] = jnp.zeros(cnt_ref.shape, jnp.int32)

    a = jnp.concatenate([_rms(om_ref[...].astype(F32), gm_ref[...]).astype(BF16),
                         _rms(ob_ref[...].astype(F32), gb_ref[...]).astype(BF16)], axis=1)
    mixed = jnp.dot(a, wo_ref[...], preferred_element_type=F32)
    x1 = _layer_norm(DEEPNORM_ALPHA * x_ref[...] + mixed, lg_ref[...], lb_ref[...])
    x1_ref[...] = x1

    logits = _nt_dot(wrt_ref[...], x1, precision=lax.Precision.HIGHEST) + br_ref[...]
    tm = logits.shape[1]
    eidx = lax.broadcasted_iota(jnp.int32, (N_EXPERTS, 1), 0)
    rank = _topk_rank(logits, eidx)
    sel = rank < TOP_K
    mx = jnp.max(logits, axis=0, keepdims=True)
    p = jnp.where(sel, jnp.exp(logits - mx), 0.0)
    gates = p / jnp.sum(p, axis=0, keepdims=True)

    before = (lax.broadcasted_iota(jnp.int32, (tm, tm), 0)
              < lax.broadcasted_iota(jnp.int32, (tm, tm), 1)).astype(BF16)
    prefix = jnp.dot(sel.astype(BF16), before, preferred_element_type=F32).astype(jnp.int32)
    pos = cnt_ref[:, 0:1] + prefix
    cnt_ref[...] = cnt_ref[...] + jnp.sum(sel.astype(jnp.int32), axis=1, keepdims=True)

    pick = lambda vals, k, zero: jnp.sum(jnp.where(rank == k, vals, zero), axis=0, keepdims=True)
    pad_i = [jnp.zeros((8 - TOP_K, tm), jnp.int32)]
    idx_ref[...] = jnp.concatenate([pick(eidx, k, 0) for k in range(TOP_K)] + pad_i, axis=0)
    pos_ref[...] = jnp.concatenate([pick(pos, k, 0) for k in range(TOP_K)] + pad_i, axis=0)
    gate_rows = jnp.concatenate([pick(gates, k, 0.0) for k in range(TOP_K)]
                                + [jnp.zeros((LANES - TOP_K, tm), F32)], axis=0)
    gate_ref[...] = gate_rows.T


def _outproj(o_mla, o_moba, g_mla, g_moba, wo, x, ln_g, ln_b, wr_t, b_r, *, tm=256):
    s = x.shape[0]
    full = lambda a: pl.BlockSpec(a.shape, lambda i: (0,) * a.ndim)
    return pl.pallas_call(
        _outproj_kernel,
        out_shape=(jax.ShapeDtypeStruct((s, D_MODEL), F32),
                   jax.ShapeDtypeStruct((8, s), jnp.int32),
                   jax.ShapeDtypeStruct((8, s), jnp.int32),
                   jax.ShapeDtypeStruct((s, LANES), F32),
                   jax.ShapeDtypeStruct((N_EXPERTS, LANES), jnp.int32)),
        grid=(s // tm,),
        in_specs=[pl.BlockSpec((None, tm, MLA_WIDTH), lambda i: (*_flash_row_block(i, tm, s), 0)),
                  pl.BlockSpec((None, tm, MOBA_WIDTH), lambda i: (*_flash_row_block(i, tm, s), 0)),
                  full(g_mla), full(g_moba), full(wo),
                  pl.BlockSpec((tm, D_MODEL), lambda i: (i, 0)),
                  full(ln_g), full(ln_b), full(wr_t), full(b_r)],
        out_specs=(pl.BlockSpec((tm, D_MODEL), lambda i: (i, 0)),
                   pl.BlockSpec((8, tm), lambda i: (0, i)),
                   pl.BlockSpec((8, tm), lambda i: (0, i)),
                   pl.BlockSpec((tm, LANES), lambda i: (i, 0)),
                   pl.BlockSpec((N_EXPERTS, LANES), lambda i: (0, 0))),
        compiler_params=_params("arbitrary"),
        name="outproj_router",
    )(o_mla, o_moba, g_mla, g_moba, wo, x, ln_g, ln_b, wr_t, b_r)


def _row_gather(idx_smem, src_hbm, dst_vmem, sem, n_rows):
    def issue(r, carry):
        row = idx_smem[lax.shift_right_logical(r, 7), lax.bitwise_and(r, LANES - 1)]
        pltpu.make_async_copy(src_hbm.at[pl.ds(row, 1)], dst_vmem.at[pl.ds(r, 1)], sem).start()
        return carry

    lax.fori_loop(0, n_rows, issue, 0, unroll=8)
    pltpu.make_async_copy(src_hbm.at[pl.ds(0, n_rows)], dst_vmem, sem).wait()


def _expert_kernel(ie_ref, nit_ref, tok_hbm, x1_hbm, wg_ref, wu_ref, bg_ref, bu_ref, wd_ref, bd_ref,
                   o_ref, tok_smem, xf_ref, xb_ref, act_ref, sem_tok, sem_rows, *, nf, tf, rows):
    w = pl.program_id(0)
    s = pl.program_id(1)
    valid = w < nit_ref[0]

    @pl.when(valid & (s == 0))
    def _():
        cp = pltpu.make_async_copy(tok_hbm.at[w], tok_smem, sem_tok)
        cp.start()
        cp.wait()
        _row_gather(tok_smem, x1_hbm, xf_ref, sem_rows, rows)

        def cast(c, carry):
            sl = pl.ds(pl.multiple_of(c * MOE_BLOCK, MOE_BLOCK), MOE_BLOCK)
            xb_ref[sl, :] = xf_ref[sl, :].astype(BF16)
            return carry

        lax.fori_loop(0, rows // MOE_BLOCK, cast, 0)

    @pl.when(valid & (s < nf))
    def _():
        x = xb_ref[...]
        g = jnp.dot(x, wg_ref[...].astype(BF16), preferred_element_type=F32) + bg_ref[...]
        u = jnp.dot(x, wu_ref[...].astype(BF16), preferred_element_type=F32) + bu_ref[...]
        g = jnp.minimum(g, SWIGLU_LIMIT)
        u = jnp.clip(u, -SWIGLU_LIMIT, SWIGLU_LIMIT)
        act = (u + 1.0) * (g * (1.0 / (1.0 + jnp.exp(-SWIGLU_ALPHA * g))))
        act_ref[s] = act.astype(BF16)

    @pl.when(valid & (s >= nf))
    def _():
        acc = jnp.zeros(o_ref.shape, F32)
        for f in range(nf):
            acc = acc + jnp.dot(act_ref[f], wd_ref[f * tf:(f + 1) * tf, :].astype(BF16),
                                preferred_element_type=F32)
        o_ref[...] = acc + bd_ref[...]


def _experts(item_e, n_items, tok_tab, x1, w_gate_up, b_gate_up, w_down, b_down, *, tf=256, tn=256):
    n_work, _, _ = tok_tab.shape
    rows = MOE_ROWS
    nf = D_FF // tf
    nn = D_MODEL // tn
    last = nf + nn - 1

    def clamp(w, s, nit):
        ok = w < nit[0]
        return jnp.where(ok, w, nit[0] - 1), jnp.where(ok, s, last)

    def fmap(off):
        def index(w, s, ie, nit):
            we, se = clamp(w, s, nit)
            return ie[we], 0, off + jnp.minimum(se, nf - 1)
        return index

    def nmap(w, s, ie, nit):
        we, se = clamp(w, s, nit)
        return ie[we], 0, jnp.maximum(se - nf, 0)

    def omap(w, s, ie, nit):
        we, se = clamp(w, s, nit)
        return we, jnp.maximum(se - nf, 0)

    grid_spec = pltpu.PrefetchScalarGridSpec(
        num_scalar_prefetch=2,
        grid=(n_work, nf + nn),
        in_specs=[
            pl.BlockSpec(memory_space=pl.ANY),
            pl.BlockSpec(memory_space=pl.ANY),
            pl.BlockSpec((None, D_MODEL, tf), fmap(0)),
            pl.BlockSpec((None, D_MODEL, tf), fmap(nf)),
            pl.BlockSpec((None, 1, tf), fmap(0)),
            pl.BlockSpec((None, 1, tf), fmap(nf)),
            pl.BlockSpec((None, D_FF, tn), nmap),
            pl.BlockSpec((None, 1, tn), nmap),
        ],
        out_specs=pl.BlockSpec((rows, tn), omap),
        scratch_shapes=[pltpu.SMEM((rows // LANES, LANES), jnp.int32),
                        pltpu.VMEM((rows, D_MODEL), F32),
                        pltpu.VMEM((rows, D_MODEL), BF16),
                        pltpu.VMEM((nf, rows, tf), BF16),
                        pltpu.SemaphoreType.DMA, pltpu.SemaphoreType.DMA],
    )
    return pl.pallas_call(
        functools.partial(_expert_kernel, nf=nf, tf=tf, rows=rows),
        out_shape=jax.ShapeDtypeStruct((n_work * rows, D_MODEL), F32),
        grid_spec=grid_spec,
        compiler_params=_params("arbitrary", "arbitrary"),
        name="experts",
    )(item_e, n_items, tok_tab, x1, w_gate_up, w_gate_up, b_gate_up, b_gate_up, w_down, b_down)


def _final_kernel(yrow_hbm, ys_hbm, x1_ref, gate_ref, g_ref, b_ref, o_ref,
                  yrow_smem, ybuf_ref, sem_idx, sem_rows, *, tm):
    cp = pltpu.make_async_copy(yrow_hbm.at[pl.program_id(0)], yrow_smem, sem_idx)
    cp.start()
    cp.wait()
    _row_gather(yrow_smem, ys_hbm, ybuf_ref, sem_rows, TOP_K * tm)
    ffn = gate_ref[:, 0:1] * ybuf_ref[0:tm, :]
    for k in range(1, TOP_K):
        ffn = ffn + gate_ref[:, k:k + 1] * ybuf_ref[k * tm:(k + 1) * tm, :]
    o_ref[...] = _layer_norm(DEEPNORM_ALPHA * x1_ref[...] + ffn, g_ref[...], b_ref[...])


def _final(yrow, ys, x1, gates, g, b, *, tm):
    s = x1.shape[0]
    return pl.pallas_call(
        functools.partial(_final_kernel, tm=tm),
        out_shape=jax.ShapeDtypeStruct((s, D_MODEL), F32),
        grid=(s // tm,),
        in_specs=[pl.BlockSpec(memory_space=pl.ANY),
                  pl.BlockSpec(memory_space=pl.ANY),
                  pl.BlockSpec((tm, D_MODEL), lambda i: (i, 0)),
                  pl.BlockSpec((tm, LANES), lambda i: (i, 0)),
                  pl.BlockSpec((1, D_MODEL), lambda i: (0, 0)),
                  pl.BlockSpec((1, D_MODEL), lambda i: (0, 0))],
        out_specs=pl.BlockSpec((tm, D_MODEL), lambda i: (i, 0)),
        scratch_shapes=[pltpu.SMEM((TOP_K * tm // LANES, LANES), jnp.int32),
                        pltpu.VMEM((TOP_K * tm, D_MODEL), F32),
                        pltpu.SemaphoreType.DMA, pltpu.SemaphoreType.DMA],
        compiler_params=_params("arbitrary"),
        name="combine_final_ln",
    )(yrow, ys, x1, gates, g, b)


def _inv_freq(dim, width):
    half = dim // 2
    lane = np.arange(LANES)
    f = ROPE_THETA ** (-(lane % half).astype(np.float32) * 2.0 / dim)
    return jnp.asarray(np.where(lane < width, f, 0.0).astype(np.float32)).reshape(1, LANES)


def _layer(x, pos, w_in, g_q_a, w_q_b, g_kv_a, w_kv_b, g_out_mla, g_out_moba, w_o, ln1_g, ln1_b,
           w_router, b_router, w_gate_up, b_gate_up, w_down, b_down, ln2_g, ln2_b):
    s = x.shape[0]
    mla_cols = Q_LORA + KV_LORA + QK_ROPE
    w_in_p = jnp.concatenate(
        [w_in[:, :mla_cols], jnp.zeros((D_MODEL, 1024 - mla_cols), w_in.dtype), w_in[:, mla_cols:]],
        axis=1).astype(BF16)
    wq = w_q_b.reshape(Q_LORA, MLA_HEADS, QK_NOPE + QK_ROPE)
    wqn = wq[:, :, :QK_NOPE].reshape(Q_LORA, MLA_HEADS * LANES).astype(BF16)
    wqr = jnp.pad(wq[:, :, QK_NOPE:], ((0, 0), (0, 0), (0, LANES - QK_ROPE))).reshape(
        Q_LORA, MLA_HEADS * LANES).astype(BF16)
    wkv = w_kv_b.reshape(KV_LORA, MLA_HEADS, QK_NOPE + V_HEAD)
    wkn = wkv[:, :, :QK_NOPE].reshape(KV_LORA, MLA_HEADS * LANES).astype(BF16)
    wv = wkv[:, :, QK_NOPE:].reshape(KV_LORA, MLA_WIDTH).astype(BF16)
    pos_col = pos.reshape(s, 1)
    row = lambda a: a.reshape(1, -1)

    h = _inproj(x, w_in_p)
    q_mla, k_mla, v_mla = _mla_prep(h, pos_col, _inv_freq(QK_ROPE, QK_ROPE), row(g_q_a),
                                    row(g_kv_a), wqn, wqr, wkn, wv)
    invf_moba = _inv_freq(MOBA_HEAD_DIM, LANES)
    k_moba, kbar = _moba_kprep(h, pos_col, invf_moba)
    q_moba = _moba_qprep(h, pos_col, invf_moba, kbar)
    o_mla = _flash(q_mla, k_mla, v_mla, 0, MLA_HEADS)
    o_moba = _flash(q_moba, k_moba, h, 3 * MOBA_WIDTH // LANES, MOBA_HEADS)

    x1, idx_t, pos_t, gates, cnt = _outproj(o_mla, o_moba, row(g_out_mla), row(g_out_moba),
                                            w_o.astype(BF16), x, row(ln1_g), row(ln1_b),
                                            w_router.T, b_router.reshape(N_EXPERTS, 1))

    n_work = -(-s * TOP_K // MOE_ROWS) + N_EXPERTS
    counts = cnt[:, 0]
    items_per_e = (counts + MOE_ROWS - 1) // MOE_ROWS
    item_end = jnp.cumsum(items_per_e)
    item_start = item_end - items_per_e
    n_items = item_end[-1:].astype(jnp.int32)
    item_e = jnp.minimum(jnp.sum(jnp.arange(n_work)[:, None] >= item_end[None, :], axis=1),
                         N_EXPERTS - 1).astype(jnp.int32)
    e_k = idx_t[:TOP_K]
    p_k = pos_t[:TOP_K]
    start_k = jnp.sum(jnp.where(e_k[..., None] == jnp.arange(N_EXPERTS), item_start, 0), axis=-1)
    yrow = ((start_k + p_k // MOE_ROWS) * MOE_ROWS + p_k % MOE_ROWS).astype(jnp.int32)
    tok = jnp.broadcast_to(jnp.arange(s, dtype=jnp.int32), (TOP_K, s))
    tok_tab = jnp.zeros((n_work * MOE_ROWS,), jnp.int32).at[yrow.reshape(-1)].set(tok.reshape(-1))
    tok_tab = tok_tab.reshape(n_work, MOE_ROWS // LANES, LANES)

    ys = _experts(item_e, n_items, tok_tab, x1, w_gate_up,
                  b_gate_up.reshape(N_EXPERTS, 1, 2 * D_FF), w_down,
                  b_down.reshape(N_EXPERTS, 1, D_MODEL))
    tm = COMBINE_ROWS
    yrow_tiles = yrow.reshape(TOP_K, s // tm, tm).transpose(1, 0, 2).reshape(
        s // tm, TOP_K * tm // LANES, LANES)
    return _final(yrow_tiles, ys, x1, gates, row(ln2_g), row(ln2_b), tm=tm)


def kernel(x, positions, w_in, g_q_a, w_q_b, g_kv_a, w_kv_b, g_out_mla, g_out_moba, w_o, ln1_g, ln1_b,
           w_router, b_router, w_gate_up, b_gate_up, w_down, b_down, ln2_g, ln2_b):
    b, s, d = x.shape
    assert b == 1 and s == SEQ and d == D_MODEL and w_in.shape[0] == DEPTH
    hcur = x[0]
    for l in range(DEPTH):
        hcur = _layer(hcur, positions[0], w_in[l], g_q_a[l], w_q_b[l], g_kv_a[l], w_kv_b[l],
                      g_out_mla[l], g_out_moba[l], w_o[l], ln1_g[l], ln1_b[l], w_router[l],
                      b_router[l], w_gate_up[l], b_gate_up[l], w_down[l], b_down[l], ln2_g[l],
                      ln2_b[l])
    return hcur[None]
```

```python
import functools

import numpy as np
import jax
import jax.numpy as jnp
from jax import lax
from jax.experimental import pallas as pl
from jax.experimental.pallas import tpu as pltpu

D_MODEL = 2048
SEQ = 8192
MLA_HEADS = 8
QK_NOPE = 128
QK_ROPE = 64
V_HEAD = 128
Q_LORA = 512
KV_LORA = 256
MLA_WIDTH = MLA_HEADS * V_HEAD
MOBA_HEADS = 8
MOBA_HEAD_DIM = 128
MOBA_WIDTH = MOBA_HEADS * MOBA_HEAD_DIM
MOBA_BLOCK = 256
MOBA_TOPK = 3
N_MOBA_BLOCKS = SEQ // MOBA_BLOCK
ROPE_THETA = 10000.0
N_EXPERTS = 32
TOP_K = 4
D_FF = D_MODEL
SWIGLU_LIMIT = 7.0
SWIGLU_ALPHA = 1.702
MOE_BLOCK = 256
MOE_ROWS = 1280
MOE_STEPS = 8
DEPTH = 1
DEEPNORM_ALPHA = float((2 * DEPTH) ** 0.25)
RMS_EPS = 1e-6
LN_EPS = 1e-5

LANES = 128
HEAD_SLOT = 2 * LANES
H_COLS = 4096
MASK_BIAS = -float(2 ** 17)
V7X_VMEM_LIMIT = 56 * 1024 * 1024

F32 = jnp.float32
BF16 = jnp.bfloat16


def _params(*semantics):
    return pltpu.CompilerParams(dimension_semantics=semantics, vmem_limit_bytes=V7X_VMEM_LIMIT)


def _rms(xf, g):
    return xf * lax.rsqrt(jnp.mean(xf * xf, axis=-1, keepdims=True) + RMS_EPS) * g


def _layer_norm(xf, g, b):
    mu = jnp.mean(xf, axis=-1, keepdims=True)
    xc = xf - mu
    var = jnp.mean(xc * xc, axis=-1, keepdims=True)
    return xc * lax.rsqrt(var + LN_EPS) * g + b


def _nt_dot(a, b, **kw):
    return lax.dot_general(a, b, (((1,), (1,)), ((), ())), preferred_element_type=F32, **kw)


def _topk_rank(vals, row_idx):
    n = vals.shape[0]
    rank = jnp.zeros(vals.shape, jnp.int32)
    for jp in range(n):
        vj = vals[jp:jp + 1, :]
        ahead = (vj > vals) | ((vj == vals) & (jp < row_idx))
        rank = rank + ahead.astype(jnp.int32)
    return rank


def _inproj_kernel(x_ref, w_ref, o_ref, xb_ref):
    @pl.when(pl.program_id(1) == 0)
    def _():
        xb_ref[...] = x_ref[...].astype(BF16)

    o_ref[...] = jnp.dot(xb_ref[...], w_ref[...], preferred_element_type=F32).astype(o_ref.dtype)


def _inproj(x, w_bf16, *, tm=1024, tn=512):
    m, k = x.shape
    n = w_bf16.shape[1]
    return pl.pallas_call(
        _inproj_kernel,
        out_shape=jax.ShapeDtypeStruct((m, n), BF16),
        grid=(m // tm, n // tn),
        in_specs=[pl.BlockSpec((tm, k), lambda i, j: (i, 0)),
                  pl.BlockSpec((k, tn), lambda i, j: (0, j))],
        out_specs=pl.BlockSpec((tm, tn), lambda i, j: (i, j)),
        scratch_shapes=[pltpu.VMEM((tm, k), BF16)],
        compiler_params=_params("parallel", "arbitrary"),
        name="inproj",
    )(x, w_bf16)


def _rope_tables_mla(pos_ref, invf_ref):
    ang = pos_ref[...].astype(F32) * invf_ref[...]
    lane = lax.broadcasted_iota(jnp.int32, ang.shape, 1)
    cos = jnp.cos(ang)
    sin = jnp.sin(ang)
    half = QK_ROPE // 2
    sin_a = jnp.where(lane < half, -sin, 0.0)
    sin_b = jnp.where((lane >= half) & (lane < QK_ROPE), sin, 0.0)
    return cos, sin_a, sin_b


def _rope_mla(t, cos, sin_a, sin_b):
    return (t * cos + pltpu.roll(t, LANES - QK_ROPE // 2, 1) * sin_a
            + pltpu.roll(t, QK_ROPE // 2, 1) * sin_b)


def _rope_tables_moba(pos_ref, invf_ref):
    ang = pos_ref[...].astype(F32) * invf_ref[...]
    lane = lax.broadcasted_iota(jnp.int32, ang.shape, 1)
    cos = jnp.cos(ang)
    sin = jnp.sin(ang)
    return cos, jnp.where(lane < MOBA_HEAD_DIM // 2, -sin, sin)


def _rope_moba(t, cos, sin_signed):
    return t * cos + pltpu.roll(t, MOBA_HEAD_DIM // 2, 1) * sin_signed


def _mla_prep_kernel(h_ref, pos_ref, invf_ref, gq_ref, gkv_ref, wqn_ref, wqr_ref, wkn_ref, wv_ref,
                     q_ref, k_ref, v_ref):
    scale = (QK_NOPE + QK_ROPE) ** -0.5
    cos, sin_a, sin_b = _rope_tables_mla(pos_ref, invf_ref)
    hq = h_ref[:, 0:Q_LORA].astype(F32)
    hkv = h_ref[:, Q_LORA:Q_LORA + KV_LORA].astype(F32)
    hkr = h_ref[:, Q_LORA + KV_LORA:Q_LORA + KV_LORA + LANES].astype(F32)

    qn = _rms(hq, gq_ref[...]).astype(BF16)
    q_nope = jnp.dot(qn, wqn_ref[...], preferred_element_type=F32)
    q_rope = jnp.dot(qn, wqr_ref[...], preferred_element_type=F32)
    kvn = _rms(hkv, gkv_ref[...]).astype(BF16)
    k_nope = jnp.dot(kvn, wkn_ref[...], preferred_element_type=F32)
    v_ref[...] = jnp.dot(kvn, wv_ref[...], preferred_element_type=F32).astype(BF16)
    kpe = _rope_mla(hkr, cos, sin_a, sin_b).astype(BF16)
    for h in range(MLA_HEADS):
        lo = h * HEAD_SLOT
        hs = slice(h * LANES, (h + 1) * LANES)
        q_ref[:, lo:lo + LANES] = (q_nope[:, hs] * scale).astype(BF16)
        q_ref[:, lo + LANES:lo + HEAD_SLOT] = (
            _rope_mla(q_rope[:, hs], cos, sin_a, sin_b) * scale).astype(BF16)
        k_ref[:, lo:lo + LANES] = k_nope[:, hs].astype(BF16)
        k_ref[:, lo + LANES:lo + HEAD_SLOT] = kpe


def _mla_prep(h, pos_col, invf, gq, gkv, wqn, wqr, wkn, wv, *, tm=512):
    s = h.shape[0]
    full = lambda a: pl.BlockSpec(a.shape, lambda i: (0,) * a.ndim)
    return pl.pallas_call(
        _mla_prep_kernel,
        out_shape=(jax.ShapeDtypeStruct((s, MLA_HEADS * HEAD_SLOT), BF16),
                   jax.ShapeDtypeStruct((s, MLA_HEADS * HEAD_SLOT), BF16),
                   jax.ShapeDtypeStruct((s, MLA_WIDTH), BF16)),
        grid=(s // tm,),
        in_specs=[pl.BlockSpec((tm, 1024), lambda i: (i, 0)),
                  pl.BlockSpec((tm, 1), lambda i: (i, 0)),
                  full(invf), full(gq), full(gkv), full(wqn), full(wqr), full(wkn), full(wv)],
        out_specs=(pl.BlockSpec((tm, MLA_HEADS * HEAD_SLOT), lambda i: (i, 0)),
                   pl.BlockSpec((tm, MLA_HEADS * HEAD_SLOT), lambda i: (i, 0)),
                   pl.BlockSpec((tm, MLA_WIDTH), lambda i: (i, 0))),
        compiler_params=_params("parallel"),
        name="mla_prep",
    )(h, pos_col, invf, gq, gkv, wqn, wqr, wkn, wv)


def _moba_kprep_kernel(h_ref, pos_ref, invf_ref, k_ref, kbar_ref, *, rows):
    cos, sin_signed = _rope_tables_moba(pos_ref, invf_ref)
    row = pl.program_id(0) * rows + lax.broadcasted_iota(jnp.int32, (rows, LANES), 0)
    lane = lax.broadcasted_iota(jnp.int32, (rows, LANES), 1)
    onehot = (lane == row // MOBA_BLOCK).astype(BF16)
    for h in range(MOBA_HEADS):
        lo = h * HEAD_SLOT
        kr = _rope_moba(h_ref[:, h * LANES:(h + 1) * LANES].astype(F32), cos, sin_signed)
        k_ref[:, lo:lo + LANES] = kr.astype(BF16)
        k_ref[:, lo + LANES:lo + HEAD_SLOT] = onehot
        for b in range(rows // MOBA_BLOCK):
            kbar_ref[b:b + 1, h * LANES:(h + 1) * LANES] = jnp.mean(
                kr[b * MOBA_BLOCK:(b + 1) * MOBA_BLOCK], axis=0, keepdims=True)


def _moba_kprep(h, pos_col, invf, *, rows=2048):
    s = h.shape[0]
    return pl.pallas_call(
        functools.partial(_moba_kprep_kernel, rows=rows),
        out_shape=(jax.ShapeDtypeStruct((s, MOBA_HEADS * HEAD_SLOT), BF16),
                   jax.ShapeDtypeStruct((s // MOBA_BLOCK, MOBA_WIDTH), F32)),
        grid=(s // rows,),
        in_specs=[pl.BlockSpec((rows, MOBA_WIDTH), lambda i: (i, 2)),
                  pl.BlockSpec((rows, 1), lambda i: (i, 0)),
                  pl.BlockSpec((1, LANES), lambda i: (0, 0))],
        out_specs=(pl.BlockSpec((rows, MOBA_HEADS * HEAD_SLOT), lambda i: (i, 0)),
                   pl.BlockSpec((rows // MOBA_BLOCK, MOBA_WIDTH), lambda i: (i, 0))),
        compiler_params=_params("parallel"),
        name="moba_kprep",
    )(h, pos_col, invf)


def _moba_qprep_kernel(h_ref, pos_ref, invf_ref, kbar_ref, q_ref, *, tm):
    scale = MOBA_HEAD_DIM ** -0.5
    cos, sin_signed = _rope_tables_moba(pos_ref, invf_ref)
    nb = N_MOBA_BLOCKS
    tok = pl.program_id(0) * tm + lax.broadcasted_iota(jnp.int32, (1, tm), 1)
    qblk = tok // MOBA_BLOCK
    blk = lax.broadcasted_iota(jnp.int32, (nb, 1), 0)
    past = blk < qblk
    own = blk == qblk
    for h in range(MOBA_HEADS):
        lo = h * HEAD_SLOT
        qr = _rope_moba(h_ref[:, h * LANES:(h + 1) * LANES].astype(F32), cos, sin_signed)
        gate = _nt_dot(kbar_ref[:, h * LANES:(h + 1) * LANES], qr,
                       precision=lax.Precision.HIGHEST)
        gate = jnp.where(past, gate, -jnp.inf)
        sel = past & (_topk_rank(gate, blk) < MOBA_TOPK)
        bias = jnp.where(sel | own, 0.0, MASK_BIAS)
        bias = jnp.concatenate([bias, jnp.zeros((LANES - nb, tm), F32)], axis=0)
        q_ref[:, lo:lo + LANES] = (qr * scale).astype(BF16)
        q_ref[:, lo + LANES:lo + HEAD_SLOT] = bias.T.astype(BF16)


def _moba_qprep(h, pos_col, invf, kbar, *, tm=512):
    s = h.shape[0]
    return pl.pallas_call(
        functools.partial(_moba_qprep_kernel, tm=tm),
        out_shape=jax.ShapeDtypeStruct((s, MOBA_HEADS * HEAD_SLOT), BF16),
        grid=(s // tm,),
        in_specs=[pl.BlockSpec((tm, MOBA_WIDTH), lambda i: (i, 1)),
                  pl.BlockSpec((tm, 1), lambda i: (i, 0)),
                  pl.BlockSpec((1, LANES), lambda i: (0, 0)),
                  pl.BlockSpec(kbar.shape, lambda i: (0, 0))],
        out_specs=pl.BlockSpec((tm, MOBA_HEADS * HEAD_SLOT), lambda i: (i, 0)),
        compiler_params=_params("parallel"),
        name="moba_qprep",
    )(h, pos_col, invf, kbar)


def _flash_kernel(qa_ref, qb_ref, k_ref, v_ref, o_ref, q2_ref, m2_ref, acc2_ref, *, tq, nq):
    p_idx = pl.program_id(1)
    nc = tq // LANES
    q2_ref[0] = qa_ref[...]
    q2_ref[1] = qb_ref[...]
    m2_ref[...] = jnp.full(m2_ref.shape, -jnp.inf, F32)
    acc2_ref[...] = jnp.zeros(acc2_ref.shape, F32)
    ones = jnp.ones((tq, LANES), BF16)

    def kv_rows(j):
        return pl.ds(pl.multiple_of(j * tq, tq), tq)

    def scores(sel, j):
        return _nt_dot(q2_ref[sel], k_ref[kv_rows(j), :])

    def consume(sel, j, s):
        v = v_ref[kv_rows(j), :]
        chunks = [s[:, c * LANES:(c + 1) * LANES] for c in range(nc)]
        part = functools.reduce(jnp.maximum, chunks)
        m_old = m2_ref[sel]
        m_new = jnp.maximum(m_old, jnp.max(part, axis=1, keepdims=True))
        alpha = jnp.exp(m_old - m_new)
        p = jnp.concatenate([jnp.exp(ch - m_new).astype(BF16) for ch in chunks], axis=1)
        pv = jnp.dot(p, jnp.concatenate([v, ones], axis=1), preferred_element_type=F32)
        acc2_ref[sel] = jnp.concatenate([alpha, alpha], axis=1) * acc2_ref[sel] + pv
        m2_ref[sel] = m_new

    def diagonal(s):
        r = lax.broadcasted_iota(jnp.int32, s.shape, 0)
        c = lax.broadcasted_iota(jnp.int32, s.shape, 1)
        return jnp.where(c <= r, s, -jnp.inf)

    steps = []
    for t in range(nq - 1):
        sel = (t >= p_idx).astype(jnp.int32)
        steps.append((sel, t - sel * p_idx, False))
    steps.append((0, p_idx, True))
    steps.append((1, nq - 1 - p_idx, True))

    s_next = scores(steps[0][0], steps[0][1])
    for t, (sel, j, diag) in enumerate(steps):
        s = s_next
        if t + 1 < len(steps):
            s_next = scores(steps[t + 1][0], steps[t + 1][1])
        consume(sel, j, diagonal(s) if diag else s)
    for sel in range(2):
        o_ref[sel] = (acc2_ref[sel, :, :LANES] / acc2_ref[sel, :, LANES:]).astype(o_ref.dtype)


def _flash(q, k, v, v_col0, n_heads, *, tq=512):
    s = q.shape[0]
    nq = s // tq
    return pl.pallas_call(
        functools.partial(_flash_kernel, tq=tq, nq=nq),
        out_shape=jax.ShapeDtypeStruct((2, s // 2, n_heads * LANES), BF16),
        grid=(n_heads, nq // 2),
        in_specs=[pl.BlockSpec((tq, HEAD_SLOT), lambda h, p: (p, h)),
                  pl.BlockSpec((tq, HEAD_SLOT), lambda h, p: (nq - 1 - p, h)),
                  pl.BlockSpec((s, HEAD_SLOT), lambda h, p: (0, h)),
                  pl.BlockSpec((s, LANES), lambda h, p: (0, v_col0 + h))],
        out_specs=pl.BlockSpec((2, tq, LANES), lambda h, p: (0, p, h)),
        scratch_shapes=[pltpu.VMEM((2, tq, HEAD_SLOT), BF16), pltpu.VMEM((2, tq, LANES), F32),
                        pltpu.VMEM((2, tq, 2 * LANES), F32)],
        compiler_params=_params("parallel", "parallel"),
        name="flash",
    )(q, q, k, v)


def _flash_row_block(i, tm, s, tq=512):
    per_tile = tq // tm
    tile = i // per_tile
    sub = i % per_tile
    nq = s // tq
    hi = (tile >= nq // 2).astype(jnp.int32)
    return hi, jnp.where(hi == 1, nq - 1 - tile, tile) * per_tile + sub


def _outproj_kernel(om_ref, ob_ref, gm_ref, gb_ref, wo_ref, x_ref, lg_ref, lb_ref, wrt_ref, br_ref,
                    x1_ref, idx_ref, pos_ref, gate_ref, cnt_ref):
    @pl.when(pl.program_id(0) == 0)
    def _():
        cnt_ref[...] = jnp.zeros(cnt_ref.shape, jnp.int32)

    a = jnp.concatenate([_rms(om_ref[...].astype(F32), gm_ref[...]).astype(BF16),
                         _rms(ob_ref[...].astype(F32), gb_ref[...]).astype(BF16)], axis=1)
    mixed = jnp.dot(a, wo_ref[...], preferred_element_type=F32)
    x1 = _layer_norm(DEEPNORM_ALPHA * x_ref[...] + mixed, lg_ref[...], lb_ref[...])
    x1_ref[...] = x1

    logits = _nt_dot(wrt_ref[...], x1, precision=lax.Precision.HIGHEST) + br_ref[...]
    tm = logits.shape[1]
    eidx = lax.broadcasted_iota(jnp.int32, (N_EXPERTS, 1), 0)
    rank = _topk_rank(logits, eidx)
    sel = rank < TOP_K
    mx = jnp.max(logits, axis=0, keepdims=True)
    p = jnp.where(sel, jnp.exp(logits - mx), 0.0)
    gates = p / jnp.sum(p, axis=0, keepdims=True)

    before = (lax.broadcasted_iota(jnp.int32, (tm, tm), 0)
              < lax.broadcasted_iota(jnp.int32, (tm, tm), 1)).astype(BF16)
    prefix = jnp.dot(sel.astype(BF16), before, preferred_element_type=F32).astype(jnp.int32)
    pos = cnt_ref[:, 0:1] + prefix
    cnt_ref[...] = cnt_ref[...] + jnp.sum(sel.astype(jnp.int32), axis=1, keepdims=True)

    pick = lambda vals, k, zero: jnp.sum(jnp.where(rank == k, vals, zero), axis=0, keepdims=True)
    pad_i = [jnp.zeros((8 - TOP_K, tm), jnp.int32)]
    idx_ref[...] = jnp.concatenate([pick(eidx, k, 0) for k in range(TOP_K)] + pad_i, axis=0)
    pos_ref[...] = jnp.concatenate([pick(pos, k, 0) for k in range(TOP_K)] + pad_i, axis=0)
    gate_rows = jnp.concatenate([pick(gates, k, 0.0) for k in range(TOP_K)]
                                + [jnp.zeros((LANES - TOP_K, tm), F32)], axis=0)
    gate_ref[...] = gate_rows.T


def _outproj(o_mla, o_moba, g_mla, g_moba, wo, x, ln_g, ln_b, wr_t, b_r, *, tm=256):
    s = x.shape[0]
    full = lambda a: pl.BlockSpec(a.shape, lambda i: (0,) * a.ndim)
    return pl.pallas_call(
        _outproj_kernel,
        out_shape=(jax.ShapeDtypeStruct((s, D_MODEL), F32),
                   jax.ShapeDtypeStruct((8, s), jnp.int32),
                   jax.ShapeDtypeStruct((8, s), jnp.int32),
                   jax.ShapeDtypeStruct((s, LANES), F32),
                   jax.ShapeDtypeStruct((N_EXPERTS, LANES), jnp.int32)),
        grid=(s // tm,),
        in_specs=[pl.BlockSpec((None, tm, MLA_WIDTH), lambda i: (*_flash_row_block(i, tm, s), 0)),
                  pl.BlockSpec((None, tm, MOBA_WIDTH), lambda i: (*_flash_row_block(i, tm, s), 0)),
                  full(g_mla), full(g_moba), full(wo),
                  pl.BlockSpec((tm, D_MODEL), lambda i: (i, 0)),
                  full(ln_g), full(ln_b), full(wr_t), full(b_r)],
        out_specs=(pl.BlockSpec((tm, D_MODEL), lambda i: (i, 0)),
                   pl.BlockSpec((8, tm), lambda i: (0, i)),
                   pl.BlockSpec((8, tm), lambda i: (0, i)),
                   pl.BlockSpec((tm, LANES), lambda i: (i, 0)),
                   pl.BlockSpec((N_EXPERTS, LANES), lambda i: (0, 0))),
        compiler_params=_params("arbitrary"),
        name="outproj_router",
    )(o_mla, o_moba, g_mla, g_moba, wo, x, ln_g, ln_b, wr_t, b_r)


def _expert_kernel(ie_ref, nit_ref, tab_hbm, x1_hbm, wg_ref, wu_ref, bg_ref, bu_ref, wd_ref, bd_ref,
                   y_hbm, tab_smem, xf_ref, xb_ref, ya_ref, yb_ref, sem_tab, sem_g, sem_s,
                   *, nf, rows, n_tok, n_work):
    w = pl.program_id(0)
    s = pl.program_id(1)
    nit = nit_ref[0]
    valid = w < nit
    even = lax.bitwise_and(w, 1) == 0
    chunk = rows // nf
    slot_cur = lax.rem(w, 3)
    slot_next = lax.rem(w + 1, 3)
    slot_prev = lax.rem(w + 2, 3)

    def load_table(item, slot):
        cp = pltpu.make_async_copy(tab_hbm.at[item], tab_smem.at[slot], sem_tab)
        cp.start()
        cp.wait()

    def start_gather(slot, c, i):
        tok = lax.bitwise_and(tab_smem[slot, c, i], n_tok - 1)
        pltpu.make_async_copy(x1_hbm.at[pl.ds(tok, 1)], xf_ref.at[c, pl.ds(i, 1)], sem_g).start()

    def start_scatter(slot, c, i, y_ref):
        pltpu.make_async_copy(y_ref.at[c, pl.ds(i, 1)], y_hbm.at[pl.ds(tab_smem[slot, c, i], 1)],
                              sem_s).start()

    def wait_gather():
        for c in range(nf):
            pltpu.make_async_copy(x1_hbm.at[pl.ds(0, chunk)], xf_ref.at[c], sem_g).wait()

    def wait_scatter():
        for c in range(nf):
            pltpu.make_async_copy(ya_ref.at[c], y_hbm.at[pl.ds(0, chunk)], sem_s).wait()

    def for_rows(fn):
        def body(i, carry):
            for c in range(nf):
                fn(c, i)
            return carry
        lax.fori_loop(0, chunk, body, 0)

    @pl.when(valid & (s == 0))
    def _():
        @pl.when(w == 0)
        def _():
            load_table(n_work, 2)
            load_table(0, 0)
            for_rows(lambda c, i: start_gather(0, c, i))
            yb_ref[...] = jnp.zeros(yb_ref.shape, F32)

        load_table(jnp.minimum(w + 1, nit - 1), slot_next)
        wait_gather()
        for c in range(nf):
            xb_ref[c * chunk:(c + 1) * chunk, :] = xf_ref[c].astype(BF16)
        bias = jnp.broadcast_to(bd_ref[...], ya_ref.shape)

        @pl.when(even)
        def _():
            ya_ref[...] = bias

        @pl.when(jnp.logical_not(even))
        def _():
            yb_ref[...] = bias

    def step(y_cur, y_prev):
        for i in range(chunk):
            start_gather(slot_next, s, i)
        for i in range(chunk):
            start_scatter(slot_prev, s, i, y_prev)
        x = xb_ref[...]
        g = jnp.dot(x, wg_ref[...].astype(BF16), preferred_element_type=F32) + bg_ref[...]
        u = jnp.dot(x, wu_ref[...].astype(BF16), preferred_element_type=F32) + bu_ref[...]
        g = jnp.minimum(g, SWIGLU_LIMIT)
        u = jnp.clip(u, -SWIGLU_LIMIT, SWIGLU_LIMIT)
        act = ((u + 1.0) * (g * (1.0 / (1.0 + jnp.exp(-SWIGLU_ALPHA * g))))).astype(BF16)
        down = jnp.dot(act, wd_ref[...].astype(BF16), preferred_element_type=F32)
        y_cur[...] = y_cur[...] + down.reshape(y_cur.shape)

    pl.when(valid & even)(lambda: step(ya_ref, yb_ref))
    pl.when(valid & jnp.logical_not(even))(lambda: step(yb_ref, ya_ref))

    @pl.when(valid & (s == nf - 1))
    def _():
        wait_scatter()

        @pl.when(w == nit - 1)
        def _():
            wait_gather()
            pl.when(even)(lambda: for_rows(lambda c, i: start_scatter(slot_cur, c, i, ya_ref)))
            pl.when(jnp.logical_not(even))(
                lambda: for_rows(lambda c, i: start_scatter(slot_cur, c, i, yb_ref)))
            wait_scatter()


def _experts(item_e, n_items, tab, x1, w_gate_up, b_gate_up, w_down, b_down, *, tf=256):
    n_work, nf, tab_lanes = tab.shape[0] - 1, tab.shape[1], tab.shape[2]
    n_tok = x1.shape[0]
    rows = MOE_ROWS
    chunk = rows // nf
    assert nf * tf == D_FF and n_tok & (n_tok - 1) == 0 and rows <= n_tok and chunk % 16 == 0

    def clamp(w, s, nit):
        ok = w < nit[0]
        return jnp.where(ok, w, nit[0] - 1), jnp.where(ok, s, nf - 1)

    def colmap(off):
        def index(w, s, ie, nit):
            we, se = clamp(w, s, nit)
            return ie[we], 0, off + se
        return index

    def rowmap(w, s, ie, nit):
        we, se = clamp(w, s, nit)
        return ie[we], se, 0

    def emap(w, s, ie, nit):
        we, _ = clamp(w, s, nit)
        return ie[we], 0, 0

    grid_spec = pltpu.PrefetchScalarGridSpec(
        num_scalar_prefetch=2,
        grid=(n_work, nf),
        in_specs=[
            pl.BlockSpec(memory_space=pl.ANY),
            pl.BlockSpec(memory_space=pl.ANY),
            pl.BlockSpec((None, D_MODEL, tf), colmap(0)),
            pl.BlockSpec((None, D_MODEL, tf), colmap(nf)),
            pl.BlockSpec((None, 1, tf), colmap(0)),
            pl.BlockSpec((None, 1, tf), colmap(nf)),
            pl.BlockSpec((None, tf, D_MODEL), rowmap),
            pl.BlockSpec((None, 1, D_MODEL), emap),
        ],
        out_specs=pl.BlockSpec(memory_space=pl.ANY),
        scratch_shapes=[pltpu.SMEM((3, nf, tab_lanes), jnp.int32),
                        pltpu.VMEM((nf, chunk, D_MODEL), F32),
                        pltpu.VMEM((rows, D_MODEL), BF16),
                        pltpu.VMEM((nf, chunk, D_MODEL), F32),
                        pltpu.VMEM((nf, chunk, D_MODEL), F32),
                        pltpu.SemaphoreType.DMA, pltpu.SemaphoreType.DMA, pltpu.SemaphoreType.DMA],
    )
    return pl.pallas_call(
        functools.partial(_expert_kernel, nf=nf, rows=rows, n_tok=n_tok, n_work=n_work),
        out_shape=jax.ShapeDtypeStruct((TOP_K * n_tok + rows, D_MODEL), F32),
        grid_spec=grid_spec,
        compiler_params=_params("arbitrary", "arbitrary"),
        name="experts",
    )(item_e, n_items, tab, x1, w_gate_up, w_gate_up, b_gate_up, b_gate_up, w_down, b_down)


def _final_kernel(*refs):
    y_refs = refs[:TOP_K]
    x1_ref, gate_ref, g_ref, b_ref, o_ref = refs[TOP_K:]
    ffn = gate_ref[:, 0:1] * y_refs[0][...]
    for k in range(1, TOP_K):
        ffn = ffn + gate_ref[:, k:k + 1] * y_refs[k][...]
    o_ref[...] = _layer_norm(DEEPNORM_ALPHA * x1_ref[...] + ffn, g_ref[...], b_ref[...])


def _final(y, x1, gates, g, b, *, tm=256):
    s = x1.shape[0]
    plane = lambda k: pl.BlockSpec((tm, D_MODEL), lambda i: (k * (s // tm) + i, 0))
    return pl.pallas_call(
        _final_kernel,
        out_shape=jax.ShapeDtypeStruct((s, D_MODEL), F32),
        grid=(s // tm,),
        in_specs=[plane(k) for k in range(TOP_K)] + [
            pl.BlockSpec((tm, D_MODEL), lambda i: (i, 0)),
            pl.BlockSpec((tm, LANES), lambda i: (i, 0)),
            pl.BlockSpec((1, D_MODEL), lambda i: (0, 0)),
            pl.BlockSpec((1, D_MODEL), lambda i: (0, 0))],
        out_specs=pl.BlockSpec((tm, D_MODEL), lambda i: (i, 0)),
        compiler_params=_params("parallel"),
        name="combine_final_ln",
    )(*([y] * TOP_K), x1, gates, g, b)


def _inv_freq(dim, width):
    half = dim // 2
    lane = np.arange(LANES)
    f = ROPE_THETA ** (-(lane % half).astype(np.float32) * 2.0 / dim)
    return jnp.asarray(np.where(lane < width, f, 0.0).astype(np.float32)).reshape(1, LANES)


def _layer(x, pos, w_in, g_q_a, w_q_b, g_kv_a, w_kv_b, g_out_mla, g_out_moba, w_o, ln1_g, ln1_b,
           w_router, b_router, w_gate_up, b_gate_up, w_down, b_down, ln2_g, ln2_b):
    s = x.shape[0]
    mla_cols = Q_LORA + KV_LORA + QK_ROPE
    w_in_p = jnp.concatenate(
        [w_in[:, :mla_cols], jnp.zeros((D_MODEL, 1024 - mla_cols), w_in.dtype), w_in[:, mla_cols:]],
        axis=1).astype(BF16)
    wq = w_q_b.reshape(Q_LORA, MLA_HEADS, QK_NOPE + QK_ROPE)
    wqn = wq[:, :, :QK_NOPE].reshape(Q_LORA, MLA_HEADS * LANES).astype(BF16)
    wqr = jnp.pad(wq[:, :, QK_NOPE:], ((0, 0), (0, 0), (0, LANES - QK_ROPE))).reshape(
        Q_LORA, MLA_HEADS * LANES).astype(BF16)
    wkv = w_kv_b.reshape(KV_LORA, MLA_HEADS, QK_NOPE + V_HEAD)
    wkn = wkv[:, :, :QK_NOPE].reshape(KV_LORA, MLA_HEADS * LANES).astype(BF16)
    wv = wkv[:, :, QK_NOPE:].reshape(KV_LORA, MLA_WIDTH).astype(BF16)
    pos_col = pos.reshape(s, 1)
    row = lambda a: a.reshape(1, -1)

    h = _inproj(x, w_in_p)
    q_mla, k_mla, v_mla = _mla_prep(h, pos_col, _inv_freq(QK_ROPE, QK_ROPE), row(g_q_a),
                                    row(g_kv_a), wqn, wqr, wkn, wv)
    invf_moba = _inv_freq(MOBA_HEAD_DIM, LANES)
    k_moba, kbar = _moba_kprep(h, pos_col, invf_moba)
    q_moba = _moba_qprep(h, pos_col, invf_moba, kbar)
    o_mla = _flash(q_mla, k_mla, v_mla, 0, MLA_HEADS)
    o_moba = _flash(q_moba, k_moba, h, 3 * MOBA_WIDTH // LANES, MOBA_HEADS)

    x1, idx_t, pos_t, gates, cnt = _outproj(o_mla, o_moba, row(g_out_mla), row(g_out_moba),
                                            w_o.astype(BF16), x, row(ln1_g), row(ln1_b),
                                            w_router.T, b_router.reshape(N_EXPERTS, 1))

    n_work = -(-s * TOP_K // MOE_ROWS) + N_EXPERTS
    counts = cnt[:, 0]
    items_per_e = (counts + MOE_ROWS - 1) // MOE_ROWS
    item_end = jnp.cumsum(items_per_e)
    item_start = item_end - items_per_e
    n_items = item_end[-1:].astype(jnp.int32)
    item_e = jnp.minimum(jnp.sum(jnp.arange(n_work)[:, None] >= item_end[None, :], axis=1),
                         N_EXPERTS - 1).astype(jnp.int32)
    e_k = idx_t[:TOP_K]
    p_k = pos_t[:TOP_K]
    start_k = jnp.sum(jnp.where(e_k[..., None] == jnp.arange(N_EXPERTS), item_start, 0), axis=-1)
    item_row = ((start_k + p_k // MOE_ROWS) * MOE_ROWS + p_k % MOE_ROWS).astype(jnp.int32)
    out_row = (jnp.arange(TOP_K, dtype=jnp.int32)[:, None] * s
               + jnp.arange(s, dtype=jnp.int32)[None, :])
    dump = TOP_K * s + jnp.arange((n_work + 1) * MOE_ROWS, dtype=jnp.int32) % MOE_ROWS
    tab = dump.at[item_row.reshape(-1)].set(out_row.reshape(-1))
    chunk = MOE_ROWS // MOE_STEPS
    tab = jnp.pad(tab.reshape(n_work + 1, MOE_STEPS, chunk),
                  ((0, 0), (0, 0), (0, -chunk % LANES)))

    y = _experts(item_e, n_items, tab, x1, w_gate_up, b_gate_up.reshape(N_EXPERTS, 1, 2 * D_FF),
                 w_down, b_down.reshape(N_EXPERTS, 1, D_MODEL))
    return _final(y, x1, gates, row(ln2_g), row(ln2_b))


def kernel(x, positions, w_in, g_q_a, w_q_b, g_kv_a, w_kv_b, g_out_mla, g_out_moba, w_o, ln1_g, ln1_b,
           w_router, b_router, w_gate_up, b_gate_up, w_down, b_down, ln2_g, ln2_b):
    b, s, d = x.shape
    assert b == 1 and s == SEQ and d == D_MODEL and w_in.shape[0] == DEPTH
    hcur = x[0]
    for l in range(DEPTH):
        hcur = _layer(hcur, positions[0], w_in[l], g_q_a[l], w_q_b[l], g_kv_a[l], w_kv_b[l],
                      g_out_mla[l], g_out_moba[l], w_o[l], ln1_g[l], ln1_b[l], w_router[l],
                      b_router[l], w_gate_up[l], b_gate_up[l], w_down[l], b_down[l], ln2_g[l],
                      ln2_b[l])
    return hcur[None]
```

```python
import functools

import numpy as np
import jax
import jax.numpy as jnp
from jax import lax
from jax.experimental import pallas as pl
from jax.experimental.pallas import tpu as pltpu

D_MODEL = 2048
SEQ = 8192
MLA_HEADS = 8
QK_NOPE = 128
QK_ROPE = 64
V_HEAD = 128
Q_LORA = 512
KV_LORA = 256
MLA_WIDTH = MLA_HEADS * V_HEAD
MOBA_HEADS = 8
MOBA_HEAD_DIM = 128
MOBA_WIDTH = MOBA_HEADS * MOBA_HEAD_DIM
MOBA_BLOCK = 256
MOBA_TOPK = 3
N_MOBA_BLOCKS = SEQ // MOBA_BLOCK
ROPE_THETA = 10000.0
N_EXPERTS = 32
TOP_K = 4
D_FF = D_MODEL
SWIGLU_LIMIT = 7.0
SWIGLU_ALPHA = 1.702
MOE_BLOCK = 256
MOE_ROWS = 1152
MOE_STEPS = 8
DEPTH = 1
DEEPNORM_ALPHA = float((2 * DEPTH) ** 0.25)
RMS_EPS = 1e-6
LN_EPS = 1e-5

LANES = 128
HEAD_SLOT = 2 * LANES
H_COLS = 4096
MASK_BIAS = -float(2 ** 17)
V7X_VMEM_LIMIT = 56 * 1024 * 1024

F32 = jnp.float32
BF16 = jnp.bfloat16


def _params(*semantics):
    return pltpu.CompilerParams(dimension_semantics=semantics, vmem_limit_bytes=V7X_VMEM_LIMIT)


def _rms(xf, g):
    return xf * lax.rsqrt(jnp.mean(xf * xf, axis=-1, keepdims=True) + RMS_EPS) * g


def _layer_norm(xf, g, b):
    mu = jnp.mean(xf, axis=-1, keepdims=True)
    xc = xf - mu
    var = jnp.mean(xc * xc, axis=-1, keepdims=True)
    return xc * lax.rsqrt(var + LN_EPS) * g + b


def _nt_dot(a, b, **kw):
    return lax.dot_general(a, b, (((1,), (1,)), ((), ())), preferred_element_type=F32, **kw)


def _topk_rank(vals, row_idx):
    n = vals.shape[0]
    rank = jnp.zeros(vals.shape, jnp.int32)
    for jp in range(n):
        vj = vals[jp:jp + 1, :]
        ahead = (vj > vals) | ((vj == vals) & (jp < row_idx))
        rank = rank + ahead.astype(jnp.int32)
    return rank


def _inproj_kernel(x_ref, w_ref, o_ref, xb_ref):
    @pl.when(pl.program_id(1) == 0)
    def _():
        xb_ref[...] = x_ref[...].astype(BF16)

    o_ref[...] = jnp.dot(xb_ref[...], w_ref[...], preferred_element_type=F32).astype(o_ref.dtype)


def _inproj(x, w_bf16, *, tm=1024, tn=512):
    m, k = x.shape
    n = w_bf16.shape[1]
    return pl.pallas_call(
        _inproj_kernel,
        out_shape=jax.ShapeDtypeStruct((m, n), BF16),
        grid=(m // tm, n // tn),
        in_specs=[pl.BlockSpec((tm, k), lambda i, j: (i, 0)),
                  pl.BlockSpec((k, tn), lambda i, j: (0, j))],
        out_specs=pl.BlockSpec((tm, tn), lambda i, j: (i, j)),
        scratch_shapes=[pltpu.VMEM((tm, k), BF16)],
        compiler_params=_params("parallel", "arbitrary"),
        name="inproj",
    )(x, w_bf16)


def _rope_tables_mla(pos_ref, invf_ref):
    ang = pos_ref[...].astype(F32) * invf_ref[...]
    lane = lax.broadcasted_iota(jnp.int32, ang.shape, 1)
    cos = jnp.cos(ang)
    sin = jnp.sin(ang)
    half = QK_ROPE // 2
    sin_a = jnp.where(lane < half, -sin, 0.0)
    sin_b = jnp.where((lane >= half) & (lane < QK_ROPE), sin, 0.0)
    return cos, sin_a, sin_b


def _rope_mla(t, cos, sin_a, sin_b):
    return (t * cos + pltpu.roll(t, LANES - QK_ROPE // 2, 1) * sin_a
            + pltpu.roll(t, QK_ROPE // 2, 1) * sin_b)


def _rope_tables_moba(pos_ref, invf_ref):
    ang = pos_ref[...].astype(F32) * invf_ref[...]
    lane = lax.broadcasted_iota(jnp.int32, ang.shape, 1)
    cos = jnp.cos(ang)
    sin = jnp.sin(ang)
    return cos, jnp.where(lane < MOBA_HEAD_DIM // 2, -sin, sin)


def _rope_moba(t, cos, sin_signed):
    return t * cos + pltpu.roll(t, MOBA_HEAD_DIM // 2, 1) * sin_signed


def _mla_prep_kernel(h_ref, pos_ref, invf_ref, gq_ref, gkv_ref, wqn_ref, wqr_ref, wkn_ref, wv_ref,
                     q_ref, k_ref, v_ref):
    scale = (QK_NOPE + QK_ROPE) ** -0.5
    cos, sin_a, sin_b = _rope_tables_mla(pos_ref, invf_ref)
    hq = h_ref[:, 0:Q_LORA].astype(F32)
    hkv = h_ref[:, Q_LORA:Q_LORA + KV_LORA].astype(F32)
    hkr = h_ref[:, Q_LORA + KV_LORA:Q_LORA + KV_LORA + LANES].astype(F32)

    qn = _rms(hq, gq_ref[...]).astype(BF16)
    q_nope = jnp.dot(qn, wqn_ref[...], preferred_element_type=F32)
    q_rope = jnp.dot(qn, wqr_ref[...], preferred_element_type=F32)
    kvn = _rms(hkv, gkv_ref[...]).astype(BF16)
    k_nope = jnp.dot(kvn, wkn_ref[...], preferred_element_type=F32)
    v_ref[...] = jnp.dot(kvn, wv_ref[...], preferred_element_type=F32).astype(BF16)
    kpe = _rope_mla(hkr, cos, sin_a, sin_b).astype(BF16)
    for h in range(MLA_HEADS):
        lo = h * HEAD_SLOT
        hs = slice(h * LANES, (h + 1) * LANES)
        q_ref[:, lo:lo + LANES] = (q_nope[:, hs] * scale).astype(BF16)
        q_ref[:, lo + LANES:lo + HEAD_SLOT] = (
            _rope_mla(q_rope[:, hs], cos, sin_a, sin_b) * scale).astype(BF16)
        k_ref[:, lo:lo + LANES] = k_nope[:, hs].astype(BF16)
        k_ref[:, lo + LANES:lo + HEAD_SLOT] = kpe


def _mla_prep(h, pos_col, invf, gq, gkv, wqn, wqr, wkn, wv, *, tm=512):
    s = h.shape[0]
    full = lambda a: pl.BlockSpec(a.shape, lambda i: (0,) * a.ndim)
    return pl.pallas_call(
        _mla_prep_kernel,
        out_shape=(jax.ShapeDtypeStruct((s, MLA_HEADS * HEAD_SLOT), BF16),
                   jax.ShapeDtypeStruct((s, MLA_HEADS * HEAD_SLOT), BF16),
                   jax.ShapeDtypeStruct((s, MLA_WIDTH), BF16)),
        grid=(s // tm,),
        in_specs=[pl.BlockSpec((tm, 1024), lambda i: (i, 0)),
                  pl.BlockSpec((tm, 1), lambda i: (i, 0)),
                  full(invf), full(gq), full(gkv), full(wqn), full(wqr), full(wkn), full(wv)],
        out_specs=(pl.BlockSpec((tm, MLA_HEADS * HEAD_SLOT), lambda i: (i, 0)),
                   pl.BlockSpec((tm, MLA_HEADS * HEAD_SLOT), lambda i: (i, 0)),
                   pl.BlockSpec((tm, MLA_WIDTH), lambda i: (i, 0))),
        compiler_params=_params("parallel"),
        name="mla_prep",
    )(h, pos_col, invf, gq, gkv, wqn, wqr, wkn, wv)


def _moba_kprep_kernel(h_ref, pos_ref, invf_ref, k_ref, kbar_ref, *, rows):
    cos, sin_signed = _rope_tables_moba(pos_ref, invf_ref)
    row = pl.program_id(0) * rows + lax.broadcasted_iota(jnp.int32, (rows, LANES), 0)
    lane = lax.broadcasted_iota(jnp.int32, (rows, LANES), 1)
    onehot = (lane == row // MOBA_BLOCK).astype(BF16)
    for h in range(MOBA_HEADS):
        lo = h * HEAD_SLOT
        kr = _rope_moba(h_ref[:, h * LANES:(h + 1) * LANES].astype(F32), cos, sin_signed)
        k_ref[:, lo:lo + LANES] = kr.astype(BF16)
        k_ref[:, lo + LANES:lo + HEAD_SLOT] = onehot
        for b in range(rows // MOBA_BLOCK):
            kbar_ref[b:b + 1, h * LANES:(h + 1) * LANES] = jnp.mean(
                kr[b * MOBA_BLOCK:(b + 1) * MOBA_BLOCK], axis=0, keepdims=True)


def _moba_kprep(h, pos_col, invf, *, rows=2048):
    s = h.shape[0]
    return pl.pallas_call(
        functools.partial(_moba_kprep_kernel, rows=rows),
        out_shape=(jax.ShapeDtypeStruct((s, MOBA_HEADS * HEAD_SLOT), BF16),
                   jax.ShapeDtypeStruct((s // MOBA_BLOCK, MOBA_WIDTH), F32)),
        grid=(s // rows,),
        in_specs=[pl.BlockSpec((rows, MOBA_WIDTH), lambda i: (i, 2)),
                  pl.BlockSpec((rows, 1), lambda i: (i, 0)),
                  pl.BlockSpec((1, LANES), lambda i: (0, 0))],
        out_specs=(pl.BlockSpec((rows, MOBA_HEADS * HEAD_SLOT), lambda i: (i, 0)),
                   pl.BlockSpec((rows // MOBA_BLOCK, MOBA_WIDTH), lambda i: (i, 0))),
        compiler_params=_params("parallel"),
        name="moba_kprep",
    )(h, pos_col, invf)


def _moba_qprep_kernel(h_ref, pos_ref, invf_ref, kbar_ref, q_ref, *, tm):
    scale = MOBA_HEAD_DIM ** -0.5
    cos, sin_signed = _rope_tables_moba(pos_ref, invf_ref)
    nb = N_MOBA_BLOCKS
    tok = pl.program_id(0) * tm + lax.broadcasted_iota(jnp.int32, (1, tm), 1)
    qblk = tok // MOBA_BLOCK
    blk = lax.broadcasted_iota(jnp.int32, (nb, 1), 0)
    past = blk < qblk
    own = blk == qblk
    for h in range(MOBA_HEADS):
        lo = h * HEAD_SLOT
        qr = _rope_moba(h_ref[:, h * LANES:(h + 1) * LANES].astype(F32), cos, sin_signed)
        gate = _nt_dot(kbar_ref[:, h * LANES:(h + 1) * LANES], qr,
                       precision=lax.Precision.HIGHEST)
        gate = jnp.where(past, gate, -jnp.inf)
        sel = past & (_topk_rank(gate, blk) < MOBA_TOPK)
        bias = jnp.where(sel | own, 0.0, MASK_BIAS)
        bias = jnp.concatenate([bias, jnp.zeros((LANES - nb, tm), F32)], axis=0)
        q_ref[:, lo:lo + LANES] = (qr * scale).astype(BF16)
        q_ref[:, lo + LANES:lo + HEAD_SLOT] = bias.T.astype(BF16)


def _moba_qprep(h, pos_col, invf, kbar, *, tm=512):
    s = h.shape[0]
    return pl.pallas_call(
        functools.partial(_moba_qprep_kernel, tm=tm),
        out_shape=jax.ShapeDtypeStruct((s, MOBA_HEADS * HEAD_SLOT), BF16),
        grid=(s // tm,),
        in_specs=[pl.BlockSpec((tm, MOBA_WIDTH), lambda i: (i, 1)),
                  pl.BlockSpec((tm, 1), lambda i: (i, 0)),
                  pl.BlockSpec((1, LANES), lambda i: (0, 0)),
                  pl.BlockSpec(kbar.shape, lambda i: (0, 0))],
        out_specs=pl.BlockSpec((tm, MOBA_HEADS * HEAD_SLOT), lambda i: (i, 0)),
        compiler_params=_params("parallel"),
        name="moba_qprep",
    )(h, pos_col, invf, kbar)


def _flash_kernel(qa_ref, qb_ref, k_ref, v_ref, o_ref, q2_ref, m2_ref, acc2_ref, *, tq, nq):
    p_idx = pl.program_id(1)
    nc = tq // LANES
    q2_ref[0] = qa_ref[...]
    q2_ref[1] = qb_ref[...]
    m2_ref[...] = jnp.full(m2_ref.shape, -jnp.inf, F32)
    acc2_ref[...] = jnp.zeros(acc2_ref.shape, F32)
    ones = jnp.ones((tq, LANES), BF16)

    def kv_rows(j):
        return pl.ds(pl.multiple_of(j * tq, tq), tq)

    def scores(sel, j):
        return _nt_dot(q2_ref[sel], k_ref[kv_rows(j), :])

    def consume(sel, j, s):
        v = v_ref[kv_rows(j), :]
        chunks = [s[:, c * LANES:(c + 1) * LANES] for c in range(nc)]
        part = functools.reduce(jnp.maximum, chunks)
        m_old = m2_ref[sel]
        m_new = jnp.maximum(m_old, jnp.max(part, axis=1, keepdims=True))
        alpha = jnp.exp(m_old - m_new)
        p = jnp.concatenate([jnp.exp(ch - m_new).astype(BF16) for ch in chunks], axis=1)
        pv = jnp.dot(p, jnp.concatenate([v, ones], axis=1), preferred_element_type=F32)
        acc2_ref[sel] = jnp.concatenate([alpha, alpha], axis=1) * acc2_ref[sel] + pv
        m2_ref[sel] = m_new

    def diagonal(s):
        r = lax.broadcasted_iota(jnp.int32, s.shape, 0)
        c = lax.broadcasted_iota(jnp.int32, s.shape, 1)
        return jnp.where(c <= r, s, -jnp.inf)

    steps = []
    for t in range(nq - 1):
        sel = (t >= p_idx).astype(jnp.int32)
        steps.append((sel, t - sel * p_idx, False))
    steps.append((0, p_idx, True))
    steps.append((1, nq - 1 - p_idx, True))

    s_next = scores(steps[0][0], steps[0][1])
    for t, (sel, j, diag) in enumerate(steps):
        s = s_next
        if t + 1 < len(steps):
            s_next = scores(steps[t + 1][0], steps[t + 1][1])
        consume(sel, j, diagonal(s) if diag else s)
    for sel in range(2):
        o_ref[sel] = (acc2_ref[sel, :, :LANES] / acc2_ref[sel, :, LANES:]).astype(o_ref.dtype)


def _flash(q, k, v, v_col0, n_heads, *, tq=512):
    s = q.shape[0]
    nq = s // tq
    return pl.pallas_call(
        functools.partial(_flash_kernel, tq=tq, nq=nq),
        out_shape=jax.ShapeDtypeStruct((2, s // 2, n_heads * LANES), BF16),
        grid=(n_heads, nq // 2),
        in_specs=[pl.BlockSpec((tq, HEAD_SLOT), lambda h, p: (p, h)),
                  pl.BlockSpec((tq, HEAD_SLOT), lambda h, p: (nq - 1 - p, h)),
                  pl.BlockSpec((s, HEAD_SLOT), lambda h, p: (0, h)),
                  pl.BlockSpec((s, LANES), lambda h, p: (0, v_col0 + h))],
        out_specs=pl.BlockSpec((2, tq, LANES), lambda h, p: (0, p, h)),
        scratch_shapes=[pltpu.VMEM((2, tq, HEAD_SLOT), BF16), pltpu.VMEM((2, tq, LANES), F32),
                        pltpu.VMEM((2, tq, 2 * LANES), F32)],
        compiler_params=_params("parallel", "parallel"),
        name="flash",
    )(q, q, k, v)


def _flash_row_block(i, tm, s, tq=512):
    per_tile = tq // tm
    tile = i // per_tile
    sub = i % per_tile
    nq = s // tq
    hi = (tile >= nq // 2).astype(jnp.int32)
    return hi, jnp.where(hi == 1, nq - 1 - tile, tile) * per_tile + sub


def _outproj_kernel(om_ref, ob_ref, gm_ref, gb_ref, wo_ref, x_ref, lg_ref, lb_ref, wrt_ref, br_ref,
                    x1_ref, idx_ref, pos_ref, gate_ref, cnt_ref):
    @pl.when(pl.program_id(0) == 0)
    def _():
        cnt_ref[...] = jnp.zeros(cnt_ref.shape, jnp.int32)

    a = jnp.concatenate([_rms(om_ref[...].astype(F32), gm_ref[...]).astype(BF16),
                         _rms(ob_ref[...].astype(F32), gb_ref[...]).astype(BF16)], axis=1)
    mixed = jnp.dot(a, wo_ref[...], preferred_element_type=F32)
    x1 = _layer_norm(DEEPNORM_ALPHA * x_ref[...] + mixed, lg_ref[...], lb_ref[...])
    x1_ref[...] = x1

    logits = _nt_dot(wrt_ref[...], x1, precision=lax.Precision.HIGHEST) + br_ref[...]
    tm = logits.shape[1]
    eidx = lax.broadcasted_iota(jnp.int32, (N_EXPERTS, 1), 0)
    rank = _topk_rank(logits, eidx)
    sel = rank < TOP_K
    mx = jnp.max(logits, axis=0, keepdims=True)
    p = jnp.where(sel, jnp.exp(logits - mx), 0.0)
    gates = p / jnp.sum(p, axis=0, keepdims=True)

    before = (lax.broadcasted_iota(jnp.int32, (tm, tm), 0)
              < lax.broadcasted_iota(jnp.int32, (tm, tm), 1)).astype(BF16)
    prefix = jnp.dot(sel.astype(BF16), before, preferred_element_type=F32).astype(jnp.int32)
    pos = cnt_ref[:, 0:1] + prefix
    cnt_ref[...] = cnt_ref[...] + jnp.sum(sel.astype(jnp.int32), axis=1, keepdims=True)

    pick = lambda vals, k, zero: jnp.sum(jnp.where(rank == k, vals, zero), axis=0, keepdims=True)
    pad_i = [jnp.zeros((8 - TOP_K, tm), jnp.int32)]
    idx_ref[...] = jnp.concatenate([pick(eidx, k, 0) for k in range(TOP_K)] + pad_i, axis=0)
    pos_ref[...] = jnp.concatenate([pick(pos, k, 0) for k in range(TOP_K)] + pad_i, axis=0)
    gate_rows = jnp.concatenate([pick(gates, k, 0.0) for k in range(TOP_K)]
                                + [jnp.zeros((LANES - TOP_K, tm), F32)], axis=0)
    gate_ref[...] = gate_rows.T


def _outproj(o_mla, o_moba, g_mla, g_moba, wo, x, ln_g, ln_b, wr_t, b_r, *, tm=512):
    s = x.shape[0]
    full = lambda a: pl.BlockSpec(a.shape, lambda i: (0,) * a.ndim)
    return pl.pallas_call(
        _outproj_kernel,
        out_shape=(jax.ShapeDtypeStruct((s, D_MODEL), F32),
                   jax.ShapeDtypeStruct((8, s), jnp.int32),
                   jax.ShapeDtypeStruct((8, s), jnp.int32),
                   jax.ShapeDtypeStruct((s, LANES), F32),
                   jax.ShapeDtypeStruct((N_EXPERTS, LANES), jnp.int32)),
        grid=(s // tm,),
        in_specs=[pl.BlockSpec((None, tm, MLA_WIDTH), lambda i: (*_flash_row_block(i, tm, s), 0)),
                  pl.BlockSpec((None, tm, MOBA_WIDTH), lambda i: (*_flash_row_block(i, tm, s), 0)),
                  full(g_mla), full(g_moba), full(wo),
                  pl.BlockSpec((tm, D_MODEL), lambda i: (i, 0)),
                  full(ln_g), full(ln_b), full(wr_t), full(b_r)],
        out_specs=(pl.BlockSpec((tm, D_MODEL), lambda i: (i, 0)),
                   pl.BlockSpec((8, tm), lambda i: (0, i)),
                   pl.BlockSpec((8, tm), lambda i: (0, i)),
                   pl.BlockSpec((tm, LANES), lambda i: (i, 0)),
                   pl.BlockSpec((N_EXPERTS, LANES), lambda i: (0, 0))),
        compiler_params=_params("arbitrary"),
        name="outproj_router",
    )(o_mla, o_moba, g_mla, g_moba, wo, x, ln_g, ln_b, wr_t, b_r)


def _expert_kernel(ie_ref, nit_ref, tab_hbm, x1_hbm, wg_ref, wu_ref, bg_ref, bu_ref, wd_ref, bd_ref,
                   y_hbm, tab_smem, xf_ref, xb_ref, ya_ref, yb_ref, sem_tab, sem_g, sem_s,
                   *, nf, rows, n_tok, n_work):
    w = pl.program_id(0)
    s = pl.program_id(1)
    nit = nit_ref[0]
    valid = w < nit
    even = lax.bitwise_and(w, 1) == 0
    chunk = rows // nf
    slot_cur = lax.rem(w, 3)
    slot_next = lax.rem(w + 1, 3)
    slot_prev = lax.rem(w + 2, 3)

    def load_table(item, slot):
        cp = pltpu.make_async_copy(tab_hbm.at[item], tab_smem.at[slot], sem_tab)
        cp.start()
        cp.wait()

    def start_gather(slot, c, i):
        tok = lax.bitwise_and(tab_smem[slot, c, i], n_tok - 1)
        pltpu.make_async_copy(x1_hbm.at[pl.ds(tok, 1)], xf_ref.at[c, pl.ds(i, 1)], sem_g).start()

    def start_scatter(slot, c, i, y_ref):
        pltpu.make_async_copy(y_ref.at[c, pl.ds(i, 1)], y_hbm.at[pl.ds(tab_smem[slot, c, i], 1)],
                              sem_s).start()

    def wait_gather():
        for c in range(nf):
            pltpu.make_async_copy(x1_hbm.at[pl.ds(0, chunk)], xf_ref.at[c], sem_g).wait()

    def wait_scatter():
        for c in range(nf):
            pltpu.make_async_copy(ya_ref.at[c], y_hbm.at[pl.ds(0, chunk)], sem_s).wait()

    def for_rows(fn):
        def body(i, carry):
            for c in range(nf):
                fn(c, i)
            return carry
        lax.fori_loop(0, chunk, body, 0)

    @pl.when(valid & (s == 0))
    def _():
        @pl.when(w == 0)
        def _():
            load_table(n_work, 2)
            load_table(0, 0)
            for_rows(lambda c, i: start_gather(0, c, i))
            yb_ref[...] = jnp.zeros(yb_ref.shape, F32)

        load_table(jnp.minimum(w + 1, nit - 1), slot_next)
        wait_gather()
        for c in range(nf):
            xb_ref[c * chunk:(c + 1) * chunk, :] = xf_ref[c].astype(BF16)
        bias = jnp.broadcast_to(bd_ref[...], ya_ref.shape)

        @pl.when(even)
        def _():
            ya_ref[...] = bias

        @pl.when(jnp.logical_not(even))
        def _():
            yb_ref[...] = bias

    def step(y_cur, y_prev):
        for i in range(chunk):
            start_gather(slot_next, s, i)
        for i in range(chunk):
            start_scatter(slot_prev, s, i, y_prev)
        x = xb_ref[...]
        g = jnp.dot(x, wg_ref[...].astype(BF16), preferred_element_type=F32) + bg_ref[...]
        u = jnp.dot(x, wu_ref[...].astype(BF16), preferred_element_type=F32) + bu_ref[...]
        g = jnp.minimum(g, SWIGLU_LIMIT)
        u = jnp.clip(u, -SWIGLU_LIMIT, SWIGLU_LIMIT)
        act = ((u + 1.0) * (g * (1.0 / (1.0 + jnp.exp(-SWIGLU_ALPHA * g))))).astype(BF16)
        down = jnp.dot(act, wd_ref[...].astype(BF16), preferred_element_type=F32)
        y_cur[...] = y_cur[...] + down.reshape(y_cur.shape)

    pl.when(valid & even)(lambda: step(ya_ref, yb_ref))
    pl.when(valid & jnp.logical_not(even))(lambda: step(yb_ref, ya_ref))

    @pl.when(valid & (s == nf - 1))
    def _():
        wait_scatter()

        @pl.when(w == nit - 1)
        def _():
            wait_gather()
            pl.when(even)(lambda: for_rows(lambda c, i: start_scatter(slot_cur, c, i, ya_ref)))
            pl.when(jnp.logical_not(even))(
                lambda: for_rows(lambda c, i: start_scatter(slot_cur, c, i, yb_ref)))
            wait_scatter()


def _experts(item_e, n_items, tab, x1, w_gate_up, b_gate_up, w_down, b_down, *, tf=256):
    n_work, nf, tab_lanes = tab.shape[0] - 1, tab.shape[1], tab.shape[2]
    n_tok = x1.shape[0]
    rows = MOE_ROWS
    chunk = rows // nf
    assert nf * tf == D_FF and n_tok & (n_tok - 1) == 0 and rows <= n_tok and chunk % 16 == 0

    def clamp(w, s, nit):
        ok = w < nit[0]
        return jnp.where(ok, w, nit[0] - 1), jnp.where(ok, s, nf - 1)

    def colmap(off):
        def index(w, s, ie, nit):
            we, se = clamp(w, s, nit)
            return ie[we], 0, off + se
        return index

    def rowmap(w, s, ie, nit):
        we, se = clamp(w, s, nit)
        return ie[we], se, 0

    def emap(w, s, ie, nit):
        we, _ = clamp(w, s, nit)
        return ie[we], 0, 0

    grid_spec = pltpu.PrefetchScalarGridSpec(
        num_scalar_prefetch=2,
        grid=(n_work, nf),
        in_specs=[
            pl.BlockSpec(memory_space=pl.ANY),
            pl.BlockSpec(memory_space=pl.ANY),
            pl.BlockSpec((None, D_MODEL, tf), colmap(0)),
            pl.BlockSpec((None, D_MODEL, tf), colmap(nf)),
            pl.BlockSpec((None, 1, tf), colmap(0)),
            pl.BlockSpec((None, 1, tf), colmap(nf)),
            pl.BlockSpec((None, tf, D_MODEL), rowmap),
            pl.BlockSpec((None, 1, D_MODEL), emap),
        ],
        out_specs=pl.BlockSpec(memory_space=pl.ANY),
        scratch_shapes=[pltpu.SMEM((3, nf, tab_lanes), jnp.int32),
                        pltpu.VMEM((nf, chunk, D_MODEL), F32),
                        pltpu.VMEM((rows, D_MODEL), BF16),
                        pltpu.VMEM((nf, chunk, D_MODEL), F32),
                        pltpu.VMEM((nf, chunk, D_MODEL), F32),
                        pltpu.SemaphoreType.DMA, pltpu.SemaphoreType.DMA, pltpu.SemaphoreType.DMA],
    )
    return pl.pallas_call(
        functools.partial(_expert_kernel, nf=nf, rows=rows, n_tok=n_tok, n_work=n_work),
        out_shape=jax.ShapeDtypeStruct((TOP_K * n_tok + rows, D_MODEL), F32),
        grid_spec=grid_spec,
        compiler_params=_params("arbitrary", "arbitrary"),
        name="experts",
    )(item_e, n_items, tab, x1, w_gate_up, w_gate_up, b_gate_up, b_gate_up, w_down, b_down)


def _final_kernel(*refs):
    y_refs = refs[:TOP_K]
    x1_ref, gate_ref, g_ref, b_ref, o_ref = refs[TOP_K:]
    ffn = gate_ref[:, 0:1] * y_refs[0][...]
    for k in range(1, TOP_K):
        ffn = ffn + gate_ref[:, k:k + 1] * y_refs[k][...]
    o_ref[...] = _layer_norm(DEEPNORM_ALPHA * x1_ref[...] + ffn, g_ref[...], b_ref[...])


def _final(y, x1, gates, g, b, *, tm=256):
    s = x1.shape[0]
    plane = lambda k: pl.BlockSpec((tm, D_MODEL), lambda i: (k * (s // tm) + i, 0))
    return pl.pallas_call(
        _final_kernel,
        out_shape=jax.ShapeDtypeStruct((s, D_MODEL), F32),
        grid=(s // tm,),
        in_specs=[plane(k) for k in range(TOP_K)] + [
            pl.BlockSpec((tm, D_MODEL), lambda i: (i, 0)),
            pl.BlockSpec((tm, LANES), lambda i: (i, 0)),
            pl.BlockSpec((1, D_MODEL), lambda i: (0, 0)),
            pl.BlockSpec((1, D_MODEL), lambda i: (0, 0))],
        out_specs=pl.BlockSpec((tm, D_MODEL), lambda i: (i, 0)),
        compiler_params=_params("parallel"),
        name="combine_final_ln",
    )(*([y] * TOP_K), x1, gates, g, b)


def _inv_freq(dim, width):
    half = dim // 2
    lane = np.arange(LANES)
    f = ROPE_THETA ** (-(lane % half).astype(np.float32) * 2.0 / dim)
    return jnp.asarray(np.where(lane < width, f, 0.0).astype(np.float32)).reshape(1, LANES)


def _layer(x, pos, w_in, g_q_a, w_q_b, g_kv_a, w_kv_b, g_out_mla, g_out_moba, w_o, ln1_g, ln1_b,
           w_router, b_router, w_gate_up, b_gate_up, w_down, b_down, ln2_g, ln2_b):
    s = x.shape[0]
    mla_cols = Q_LORA + KV_LORA + QK_ROPE
    w_in_p = jnp.concatenate(
        [w_in[:, :mla_cols], jnp.zeros((D_MODEL, 1024 - mla_cols), w_in.dtype), w_in[:, mla_cols:]],
        axis=1).astype(BF16)
    wq = w_q_b.reshape(Q_LORA, MLA_HEADS, QK_NOPE + QK_ROPE)
    wqn = wq[:, :, :QK_NOPE].reshape(Q_LORA, MLA_HEADS * LANES).astype(BF16)
    wqr = jnp.pad(wq[:, :, QK_NOPE:], ((0, 0), (0, 0), (0, LANES - QK_ROPE))).reshape(
        Q_LORA, MLA_HEADS * LANES).astype(BF16)
    wkv = w_kv_b.reshape(KV_LORA, MLA_HEADS, QK_NOPE + V_HEAD)
    wkn = wkv[:, :, :QK_NOPE].reshape(KV_LORA, MLA_HEADS * LANES).astype(BF16)
    wv = wkv[:, :, QK_NOPE:].reshape(KV_LORA, MLA_WIDTH).astype(BF16)
    pos_col = pos.reshape(s, 1)
    row = lambda a: a.reshape(1, -1)

    h = _inproj(x, w_in_p)
    q_mla, k_mla, v_mla = _mla_prep(h, pos_col, _inv_freq(QK_ROPE, QK_ROPE), row(g_q_a),
                                    row(g_kv_a), wqn, wqr, wkn, wv)
    invf_moba = _inv_freq(MOBA_HEAD_DIM, LANES)
    k_moba, kbar = _moba_kprep(h, pos_col, invf_moba)
    q_moba = _moba_qprep(h, pos_col, invf_moba, kbar)
    o_mla = _flash(q_mla, k_mla, v_mla, 0, MLA_HEADS)
    o_moba = _flash(q_moba, k_moba, h, 3 * MOBA_WIDTH // LANES, MOBA_HEADS)

    x1, idx_t, pos_t, gates, cnt = _outproj(o_mla, o_moba, row(g_out_mla), row(g_out_moba),
                                            w_o.astype(BF16), x, row(ln1_g), row(ln1_b),
                                            w_router.T, b_router.reshape(N_EXPERTS, 1))

    n_work = -(-s * TOP_K // MOE_ROWS) + N_EXPERTS
    counts = cnt[:, 0]
    items_per_e = (counts + MOE_ROWS - 1) // MOE_ROWS
    item_end = jnp.cumsum(items_per_e)
    item_start = item_end - items_per_e
    n_items = item_end[-1:].astype(jnp.int32)
    item_e = jnp.minimum(jnp.sum(jnp.arange(n_work)[:, None] >= item_end[None, :], axis=1),
                         N_EXPERTS - 1).astype(jnp.int32)
    e_k = idx_t[:TOP_K]
    p_k = pos_t[:TOP_K]
    start_k = jnp.sum(jnp.where(e_k[..., None] == jnp.arange(N_EXPERTS), item_start, 0), axis=-1)
    item_row = ((start_k + p_k // MOE_ROWS) * MOE_ROWS + p_k % MOE_ROWS).astype(jnp.int32)
    out_row = (jnp.arange(TOP_K, dtype=jnp.int32)[:, None] * s
               + jnp.arange(s, dtype=jnp.int32)[None, :])
    dump = TOP_K * s + jnp.arange((n_work + 1) * MOE_ROWS, dtype=jnp.int32) % MOE_ROWS
    tab = dump.at[item_row.reshape(-1)].set(out_row.reshape(-1))
    chunk = MOE_ROWS // MOE_STEPS
    tab = jnp.pad(tab.reshape(n_work + 1, MOE_STEPS, chunk),
                  ((0, 0), (0, 0), (0, -chunk % LANES)))

    y = _experts(item_e, n_items, tab, x1, w_gate_up, b_gate_up.reshape(N_EXPERTS, 1, 2 * D_FF),
                 w_down, b_down.reshape(N_EXPERTS, 1, D_MODEL))
    return _final(y, x1, gates, row(ln2_g), row(ln2_b))


def kernel(x, positions, w_in, g_q_a, w_q_b, g_kv_a, w_kv_b, g_out_mla, g_out_moba, w_o, ln1_g, ln1_b,
           w_router, b_router, w_gate_up, b_gate_up, w_down, b_down, ln2_g, ln2_b):
    b, s, d = x.shape
    assert b == 1 and s == SEQ and d == D_MODEL and w_in.shape[0] == DEPTH
    hcur = x[0]
    for l in range(DEPTH):
        hcur = _layer(hcur, positions[0], w_in[l], g_q_a[l], w_q_b[l], g_kv_a[l], w_kv_b[l],
                      g_out_mla[l], g_out_moba[l], w_o[l], ln1_g[l], ln1_b[l], w_router[l],
                      b_router[l], w_gate_up[l], b_gate_up[l], w_down[l], b_down[l], ln2_g[l],
                      ln2_b[l])
    return hcur[None]
```

```python
import functools

import numpy as np
import jax
import jax.numpy as jnp
from jax import lax
from jax.experimental import pallas as pl
from jax.experimental.pallas import tpu as pltpu

D_MODEL = 2048
SEQ = 8192
MLA_HEADS = 8
QK_NOPE = 128
QK_ROPE = 64
V_HEAD = 128
Q_LORA = 512
KV_LORA = 256
MLA_WIDTH = MLA_HEADS * V_HEAD
MOBA_HEADS = 8
MOBA_HEAD_DIM = 128
MOBA_WIDTH = MOBA_HEADS * MOBA_HEAD_DIM
MOBA_BLOCK = 256
MOBA_TOPK = 3
N_MOBA_BLOCKS = SEQ // MOBA_BLOCK
ROPE_THETA = 10000.0
N_EXPERTS = 32
TOP_K = 4
D_FF = D_MODEL
SWIGLU_LIMIT = 7.0
SWIGLU_ALPHA = 1.702
MOE_BLOCK = 256
MOE_ROWS = 1536
DEPTH = 1
DEEPNORM_ALPHA = float((2 * DEPTH) ** 0.25)
RMS_EPS = 1e-6
LN_EPS = 1e-5

LANES = 128
SUBLANES = 8
MOE_TAB_LEN = 2048
HEAD_SLOT = 2 * LANES
H_COLS = 4096
MASK_BIAS = -float(2 ** 17)
V7X_VMEM_LIMIT = 56 * 1024 * 1024

F32 = jnp.float32
BF16 = jnp.bfloat16


def _params(*semantics):
    return pltpu.CompilerParams(dimension_semantics=semantics, vmem_limit_bytes=V7X_VMEM_LIMIT)


def _rms(xf, g):
    return xf * lax.rsqrt(jnp.mean(xf * xf, axis=-1, keepdims=True) + RMS_EPS) * g


def _layer_norm(xf, g, b):
    mu = jnp.mean(xf, axis=-1, keepdims=True)
    xc = xf - mu
    var = jnp.mean(xc * xc, axis=-1, keepdims=True)
    return xc * lax.rsqrt(var + LN_EPS) * g + b


def _nt_dot(a, b, **kw):
    return lax.dot_general(a, b, (((1,), (1,)), ((), ())), preferred_element_type=F32, **kw)


def _topk_rank(vals, row_idx):
    n = vals.shape[0]
    rank = jnp.zeros(vals.shape, jnp.int32)
    for jp in range(n):
        vj = vals[jp:jp + 1, :]
        ahead = (vj > vals) | ((vj == vals) & (jp < row_idx))
        rank = rank + ahead.astype(jnp.int32)
    return rank


def _inproj_kernel(x_ref, w_ref, o_ref, xb_ref):
    @pl.when(pl.program_id(1) == 0)
    def _():
        xb_ref[...] = x_ref[...].astype(BF16)

    o_ref[...] = jnp.dot(xb_ref[...], w_ref[...], preferred_element_type=F32).astype(o_ref.dtype)


def _inproj(x, w_bf16, *, tm=1024, tn=512):
    m, k = x.shape
    n = w_bf16.shape[1]
    return pl.pallas_call(
        _inproj_kernel,
        out_shape=jax.ShapeDtypeStruct((m, n), BF16),
        grid=(m // tm, n // tn),
        in_specs=[pl.BlockSpec((tm, k), lambda i, j: (i, 0)),
                  pl.BlockSpec((k, tn), lambda i, j: (0, j))],
        out_specs=pl.BlockSpec((tm, tn), lambda i, j: (i, j)),
        scratch_shapes=[pltpu.VMEM((tm, k), BF16)],
        compiler_params=_params("parallel", "arbitrary"),
        name="inproj",
    )(x, w_bf16)


def _rope_tables_mla(pos_ref, invf_ref):
    ang = pos_ref[...].astype(F32) * invf_ref[...]
    lane = lax.broadcasted_iota(jnp.int32, ang.shape, 1)
    cos = jnp.cos(ang)
    sin = jnp.sin(ang)
    half = QK_ROPE // 2
    sin_a = jnp.where(lane < half, -sin, 0.0)
    sin_b = jnp.where((lane >= half) & (lane < QK_ROPE), sin, 0.0)
    return cos, sin_a, sin_b


def _rope_mla(t, cos, sin_a, sin_b):
    return (t * cos + pltpu.roll(t, LANES - QK_ROPE // 2, 1) * sin_a
            + pltpu.roll(t, QK_ROPE // 2, 1) * sin_b)


def _rope_tables_moba(pos_ref, invf_ref):
    ang = pos_ref[...].astype(F32) * invf_ref[...]
    lane = lax.broadcasted_iota(jnp.int32, ang.shape, 1)
    cos = jnp.cos(ang)
    sin = jnp.sin(ang)
    return cos, jnp.where(lane < MOBA_HEAD_DIM // 2, -sin, sin)


def _rope_moba(t, cos, sin_signed):
    return t * cos + pltpu.roll(t, MOBA_HEAD_DIM // 2, 1) * sin_signed


def _mla_prep_kernel(h_ref, pos_ref, invf_ref, gq_ref, gkv_ref, wqn_ref, wqr_ref, wkn_ref, wv_ref,
                     q_ref, k_ref, v_ref):
    scale = (QK_NOPE + QK_ROPE) ** -0.5
    cos, sin_a, sin_b = _rope_tables_mla(pos_ref, invf_ref)
    hq = h_ref[:, 0:Q_LORA].astype(F32)
    hkv = h_ref[:, Q_LORA:Q_LORA + KV_LORA].astype(F32)
    hkr = h_ref[:, Q_LORA + KV_LORA:Q_LORA + KV_LORA + LANES].astype(F32)

    qn = _rms(hq, gq_ref[...]).astype(BF16)
    q_nope = jnp.dot(qn, wqn_ref[...], preferred_element_type=F32)
    q_rope = jnp.dot(qn, wqr_ref[...], preferred_element_type=F32)
    kvn = _rms(hkv, gkv_ref[...]).astype(BF16)
    k_nope = jnp.dot(kvn, wkn_ref[...], preferred_element_type=F32)
    v_ref[...] = jnp.dot(kvn, wv_ref[...], preferred_element_type=F32).astype(BF16)
    kpe = _rope_mla(hkr, cos, sin_a, sin_b).astype(BF16)
    for h in range(MLA_HEADS):
        lo = h * HEAD_SLOT
        hs = slice(h * LANES, (h + 1) * LANES)
        q_ref[:, lo:lo + LANES] = (q_nope[:, hs] * scale).astype(BF16)
        q_ref[:, lo + LANES:lo + HEAD_SLOT] = (
            _rope_mla(q_rope[:, hs], cos, sin_a, sin_b) * scale).astype(BF16)
        k_ref[:, lo:lo + LANES] = k_nope[:, hs].astype(BF16)
        k_ref[:, lo + LANES:lo + HEAD_SLOT] = kpe


def _mla_prep(h, pos_col, invf, gq, gkv, wqn, wqr, wkn, wv, *, tm=512):
    s = h.shape[0]
    full = lambda a: pl.BlockSpec(a.shape, lambda i: (0,) * a.ndim)
    return pl.pallas_call(
        _mla_prep_kernel,
        out_shape=(jax.ShapeDtypeStruct((s, MLA_HEADS * HEAD_SLOT), BF16),
                   jax.ShapeDtypeStruct((s, MLA_HEADS * HEAD_SLOT), BF16),
                   jax.ShapeDtypeStruct((s, MLA_WIDTH), BF16)),
        grid=(s // tm,),
        in_specs=[pl.BlockSpec((tm, 1024), lambda i: (i, 0)),
                  pl.BlockSpec((tm, 1), lambda i: (i, 0)),
                  full(invf), full(gq), full(gkv), full(wqn), full(wqr), full(wkn), full(wv)],
        out_specs=(pl.BlockSpec((tm, MLA_HEADS * HEAD_SLOT), lambda i: (i, 0)),
                   pl.BlockSpec((tm, MLA_HEADS * HEAD_SLOT), lambda i: (i, 0)),
                   pl.BlockSpec((tm, MLA_WIDTH), lambda i: (i, 0))),
        compiler_params=_params("parallel"),
        name="mla_prep",
    )(h, pos_col, invf, gq, gkv, wqn, wqr, wkn, wv)


def _moba_kprep_kernel(h_ref, pos_ref, invf_ref, k_ref, kbar_ref, *, rows):
    cos, sin_signed = _rope_tables_moba(pos_ref, invf_ref)
    row = pl.program_id(0) * rows + lax.broadcasted_iota(jnp.int32, (rows, LANES), 0)
    lane = lax.broadcasted_iota(jnp.int32, (rows, LANES), 1)
    onehot = (lane == row // MOBA_BLOCK).astype(BF16)
    for h in range(MOBA_HEADS):
        lo = h * HEAD_SLOT
        kr = _rope_moba(h_ref[:, h * LANES:(h + 1) * LANES].astype(F32), cos, sin_signed)
        k_ref[:, lo:lo + LANES] = kr.astype(BF16)
        k_ref[:, lo + LANES:lo + HEAD_SLOT] = onehot
        for b in range(rows // MOBA_BLOCK):
            kbar_ref[b:b + 1, h * LANES:(h + 1) * LANES] = jnp.mean(
                kr[b * MOBA_BLOCK:(b + 1) * MOBA_BLOCK], axis=0, keepdims=True)


def _moba_kprep(h, pos_col, invf, *, rows=2048):
    s = h.shape[0]
    return pl.pallas_call(
        functools.partial(_moba_kprep_kernel, rows=rows),
        out_shape=(jax.ShapeDtypeStruct((s, MOBA_HEADS * HEAD_SLOT), BF16),
                   jax.ShapeDtypeStruct((s // MOBA_BLOCK, MOBA_WIDTH), F32)),
        grid=(s // rows,),
        in_specs=[pl.BlockSpec((rows, MOBA_WIDTH), lambda i: (i, 2)),
                  pl.BlockSpec((rows, 1), lambda i: (i, 0)),
                  pl.BlockSpec((1, LANES), lambda i: (0, 0))],
        out_specs=(pl.BlockSpec((rows, MOBA_HEADS * HEAD_SLOT), lambda i: (i, 0)),
                   pl.BlockSpec((rows // MOBA_BLOCK, MOBA_WIDTH), lambda i: (i, 0))),
        compiler_params=_params("parallel"),
        name="moba_kprep",
    )(h, pos_col, invf)


def _moba_qprep_kernel(h_ref, pos_ref, invf_ref, kbar_ref, q_ref, *, tm):
    scale = MOBA_HEAD_DIM ** -0.5
    cos, sin_signed = _rope_tables_moba(pos_ref, invf_ref)
    nb = N_MOBA_BLOCKS
    tok = pl.program_id(0) * tm + lax.broadcasted_iota(jnp.int32, (1, tm), 1)
    qblk = tok // MOBA_BLOCK
    blk = lax.broadcasted_iota(jnp.int32, (nb, 1), 0)
    past = blk < qblk
    own = blk == qblk
    for h in range(MOBA_HEADS):
        lo = h * HEAD_SLOT
        qr = _rope_moba(h_ref[:, h * LANES:(h + 1) * LANES].astype(F32), cos, sin_signed)
        gate = _nt_dot(kbar_ref[:, h * LANES:(h + 1) * LANES], qr,
                       precision=lax.Precision.HIGHEST)
        gate = jnp.where(past, gate, -jnp.inf)
        sel = past & (_topk_rank(gate, blk) < MOBA_TOPK)
        bias = jnp.where(sel | own, 0.0, MASK_BIAS)
        bias = jnp.concatenate([bias, jnp.zeros((LANES - nb, tm), F32)], axis=0)
        q_ref[:, lo:lo + LANES] = (qr * scale).astype(BF16)
        q_ref[:, lo + LANES:lo + HEAD_SLOT] = bias.T.astype(BF16)


def _moba_qprep(h, pos_col, invf, kbar, *, tm=512):
    s = h.shape[0]
    return pl.pallas_call(
        functools.partial(_moba_qprep_kernel, tm=tm),
        out_shape=jax.ShapeDtypeStruct((s, MOBA_HEADS * HEAD_SLOT), BF16),
        grid=(s // tm,),
        in_specs=[pl.BlockSpec((tm, MOBA_WIDTH), lambda i: (i, 1)),
                  pl.BlockSpec((tm, 1), lambda i: (i, 0)),
                  pl.BlockSpec((1, LANES), lambda i: (0, 0)),
                  pl.BlockSpec(kbar.shape, lambda i: (0, 0))],
        out_specs=pl.BlockSpec((tm, MOBA_HEADS * HEAD_SLOT), lambda i: (i, 0)),
        compiler_params=_params("parallel"),
        name="moba_qprep",
    )(h, pos_col, invf, kbar)


def _flash_kernel(qa_ref, qb_ref, k_ref, v_ref, o_ref, q2_ref, m2_ref, acc2_ref, *, tq, nq):
    p_idx = pl.program_id(1)
    nc = tq // LANES
    q2_ref[0] = qa_ref[...]
    q2_ref[1] = qb_ref[...]
    m2_ref[...] = jnp.full(m2_ref.shape, -jnp.inf, F32)
    acc2_ref[...] = jnp.zeros(acc2_ref.shape, F32)
    ones = jnp.ones((tq, LANES), BF16)

    def kv_rows(j):
        return pl.ds(pl.multiple_of(j * tq, tq), tq)

    def scores(sel, j):
        return _nt_dot(q2_ref[sel], k_ref[kv_rows(j), :])

    def consume(sel, j, s):
        v = v_ref[kv_rows(j), :]
        chunks = [s[:, c * LANES:(c + 1) * LANES] for c in range(nc)]
        part = functools.reduce(jnp.maximum, chunks)
        m_old = m2_ref[sel]
        m_new = jnp.maximum(m_old, jnp.max(part, axis=1, keepdims=True))
        alpha = jnp.exp(m_old - m_new)
        p = jnp.concatenate([jnp.exp(ch - m_new).astype(BF16) for ch in chunks], axis=1)
        pv = jnp.dot(p, jnp.concatenate([v, ones], axis=1), preferred_element_type=F32)
        acc2_ref[sel] = jnp.concatenate([alpha, alpha], axis=1) * acc2_ref[sel] + pv
        m2_ref[sel] = m_new

    def diagonal(s):
        r = lax.broadcasted_iota(jnp.int32, s.shape, 0)
        c = lax.broadcasted_iota(jnp.int32, s.shape, 1)
        return jnp.where(c <= r, s, -jnp.inf)

    steps = []
    for t in range(nq - 1):
        sel = (t >= p_idx).astype(jnp.int32)
        steps.append((sel, t - sel * p_idx, False))
    steps.append((0, p_idx, True))
    steps.append((1, nq - 1 - p_idx, True))

    s_next = scores(steps[0][0], steps[0][1])
    for t, (sel, j, diag) in enumerate(steps):
        s = s_next
        if t + 1 < len(steps):
            s_next = scores(steps[t + 1][0], steps[t + 1][1])
        consume(sel, j, diagonal(s) if diag else s)
    for sel in range(2):
        o_ref[sel] = (acc2_ref[sel, :, :LANES] / acc2_ref[sel, :, LANES:]).astype(o_ref.dtype)


def _flash(q, k, v, v_col0, n_heads, *, tq=512):
    s = q.shape[0]
    nq = s // tq
    return pl.pallas_call(
        functools.partial(_flash_kernel, tq=tq, nq=nq),
        out_shape=jax.ShapeDtypeStruct((2, s // 2, n_heads * LANES), BF16),
        grid=(n_heads, nq // 2),
        in_specs=[pl.BlockSpec((tq, HEAD_SLOT), lambda h, p: (p, h)),
                  pl.BlockSpec((tq, HEAD_SLOT), lambda h, p: (nq - 1 - p, h)),
                  pl.BlockSpec((s, HEAD_SLOT), lambda h, p: (0, h)),
                  pl.BlockSpec((s, LANES), lambda h, p: (0, v_col0 + h))],
        out_specs=pl.BlockSpec((2, tq, LANES), lambda h, p: (0, p, h)),
        scratch_shapes=[pltpu.VMEM((2, tq, HEAD_SLOT), BF16), pltpu.VMEM((2, tq, LANES), F32),
                        pltpu.VMEM((2, tq, 2 * LANES), F32)],
        compiler_params=_params("parallel", "parallel"),
        name="flash",
    )(q, q, k, v)


def _flash_row_block(i, tm, s, tq=512):
    per_tile = tq // tm
    tile = i // per_tile
    sub = i % per_tile
    nq = s // tq
    hi = (tile >= nq // 2).astype(jnp.int32)
    return hi, jnp.where(hi == 1, nq - 1 - tile, tile) * per_tile + sub


def _outproj_kernel(om_ref, ob_ref, gm_ref, gb_ref, wo_ref, x_ref, lg_ref, lb_ref, wrt_ref, br_ref,
                    x1_ref, idx_ref, pos_ref, gate_ref, cnt_ref):
    @pl.when(pl.program_id(0) == 0)
    def _():
        cnt_ref[...] = jnp.zeros(cnt_ref.shape, jnp.int32)

    a = jnp.concatenate([_rms(om_ref[...].astype(F32), gm_ref[...]).astype(BF16),
                         _rms(ob_ref[...].astype(F32), gb_ref[...]).astype(BF16)], axis=1)
    mixed = jnp.dot(a, wo_ref[...], preferred_element_type=F32)
    x1 = _layer_norm(DEEPNORM_ALPHA * x_ref[...] + mixed, lg_ref[...], lb_ref[...])
    x1_ref[...] = x1

    logits = _nt_dot(wrt_ref[...], x1, precision=lax.Precision.HIGHEST) + br_ref[...]
    tm = logits.shape[1]
    eidx = lax.broadcasted_iota(jnp.int32, (N_EXPERTS, 1), 0)
    rank = _topk_rank(logits, eidx)
    sel = rank < TOP_K
    mx = jnp.max(logits, axis=0, keepdims=True)
    p = jnp.where(sel, jnp.exp(logits - mx), 0.0)
    gates = p / jnp.sum(p, axis=0, keepdims=True)

    before = (lax.broadcasted_iota(jnp.int32, (tm, tm), 0)
              < lax.broadcasted_iota(jnp.int32, (tm, tm), 1)).astype(BF16)
    prefix = jnp.dot(sel.astype(BF16), before, preferred_element_type=F32).astype(jnp.int32)
    pos = cnt_ref[:, 0:1] + prefix
    cnt_ref[...] = cnt_ref[...] + jnp.sum(sel.astype(jnp.int32), axis=1, keepdims=True)

    pick = lambda vals, k, zero: jnp.sum(jnp.where(rank == k, vals, zero), axis=0, keepdims=True)
    pad_i = [jnp.zeros((8 - TOP_K, tm), jnp.int32)]
    idx_ref[...] = jnp.concatenate([pick(eidx, k, 0) for k in range(TOP_K)] + pad_i, axis=0)
    pos_ref[...] = jnp.concatenate([pick(pos, k, 0) for k in range(TOP_K)] + pad_i, axis=0)
    gate_rows = jnp.concatenate([pick(gates, k, 0.0) for k in range(TOP_K)]
                                + [jnp.zeros((LANES - TOP_K, tm), F32)], axis=0)
    gate_ref[...] = gate_rows.T


def _outproj(o_mla, o_moba, g_mla, g_moba, wo, x, ln_g, ln_b, wr_t, b_r, *, tm=512):
    s = x.shape[0]
    full = lambda a: pl.BlockSpec(a.shape, lambda i: (0,) * a.ndim)
    return pl.pallas_call(
        _outproj_kernel,
        out_shape=(jax.ShapeDtypeStruct((s, D_MODEL), F32),
                   jax.ShapeDtypeStruct((8, s), jnp.int32),
                   jax.ShapeDtypeStruct((8, s), jnp.int32),
                   jax.ShapeDtypeStruct((s, LANES), F32),
                   jax.ShapeDtypeStruct((N_EXPERTS, LANES), jnp.int32)),
        grid=(s // tm,),
        in_specs=[pl.BlockSpec((None, tm, MLA_WIDTH), lambda i: (*_flash_row_block(i, tm, s), 0)),
                  pl.BlockSpec((None, tm, MOBA_WIDTH), lambda i: (*_flash_row_block(i, tm, s), 0)),
                  full(g_mla), full(g_moba), full(wo),
                  pl.BlockSpec((tm, D_MODEL), lambda i: (i, 0)),
                  full(ln_g), full(ln_b), full(wr_t), full(b_r)],
        out_specs=(pl.BlockSpec((tm, D_MODEL), lambda i: (i, 0)),
                   pl.BlockSpec((8, tm), lambda i: (0, i)),
                   pl.BlockSpec((8, tm), lambda i: (0, i)),
                   pl.BlockSpec((tm, LANES), lambda i: (i, 0)),
                   pl.BlockSpec((N_EXPERTS, LANES), lambda i: (0, 0))),
        compiler_params=_params("arbitrary"),
        name="outproj_router",
    )(o_mla, o_moba, g_mla, g_moba, wo, x, ln_g, ln_b, wr_t, b_r)


def _ffn_kernel(ie_ref, nb_ref, tab_hbm, x1_hbm, wg_ref, wu_ref, bg_ref, bu_ref, wd_ref, bd_ref,
                y_hbm, tab_smem, xb_ref, y_ref, sem_tab, sem_rows, *, nf, n_tok, max_blocks):
    w = pl.program_id(0)
    s = pl.program_id(1)
    nb = nb_ref[w]
    sub = SUBLANES
    tiles_per_block = MOE_BLOCK // sub
    n_tiles = nb * tiles_per_block

    def for_rows(fn):
        def body(g, carry):
            base = g * sub
            for j in range(sub):
                fn(g, j, tab_smem[base + j])
            return carry
        lax.fori_loop(0, n_tiles, body, 0)

    def wait_tiles(make_copy):
        def body(g, carry):
            make_copy().wait()
            return carry
        lax.fori_loop(0, n_tiles, body, 0)

    def hbm_row(ref, r):
        return ref.at[lax.shift_right_logical(r, 3), pl.ds(lax.bitwise_and(r, sub - 1), 1)]

    @pl.when((nb > 0) & (s == 0))
    def _():
        cp = pltpu.make_async_copy(tab_hbm.at[w], tab_smem, sem_tab)
        cp.start()
        cp.wait()
        for_rows(lambda g, j, e: pltpu.make_async_copy(
            hbm_row(x1_hbm, lax.bitwise_and(e, n_tok - 1)), y_ref.at[g, pl.ds(j, 1)],
            sem_rows).start())
        wait_tiles(lambda: pltpu.make_async_copy(x1_hbm.at[0], y_ref.at[0], sem_rows))
        bias = jnp.broadcast_to(bd_ref[...], (tiles_per_block, sub, D_MODEL))

        def cast(b, carry):
            tiles = pl.ds(b * tiles_per_block, tiles_per_block)
            rows = pl.ds(pl.multiple_of(b * MOE_BLOCK, MOE_BLOCK), MOE_BLOCK)
            xb_ref[rows, :] = y_ref[tiles].reshape(MOE_BLOCK, D_MODEL).astype(BF16)
            y_ref[tiles] = bias
            return carry

        lax.fori_loop(0, nb, cast, 0)

    def step(m):
        x = xb_ref[0:m, :]
        g = jnp.dot(x, wg_ref[...].astype(BF16), preferred_element_type=F32) + bg_ref[...]
        u = jnp.dot(x, wu_ref[...].astype(BF16), preferred_element_type=F32) + bu_ref[...]
        g = jnp.minimum(g, SWIGLU_LIMIT)
        u = jnp.clip(u, -SWIGLU_LIMIT, SWIGLU_LIMIT)
        act = ((u + 1.0) * (g * (1.0 / (1.0 + jnp.exp(-SWIGLU_ALPHA * g))))).astype(BF16)
        down = jnp.dot(act, wd_ref[...].astype(BF16), preferred_element_type=F32)
        y_ref[0:m // sub] = y_ref[0:m // sub] + down.reshape(m // sub, sub, D_MODEL)

    for k in range(1, max_blocks + 1):
        pl.when(nb == k)(functools.partial(step, k * MOE_BLOCK))

    @pl.when((nb > 0) & (s == nf - 1))
    def _():
        for_rows(lambda g, j, e: pltpu.make_async_copy(
            y_ref.at[g, pl.ds(j, 1)], hbm_row(y_hbm, e), sem_rows).start())
        wait_tiles(lambda: pltpu.make_async_copy(y_ref.at[0], y_hbm.at[0], sem_rows))


def _ffn(item_e, item_nb, tab, x1, w_gate_up, b_gate_up, w_down, b_down, *, tf=256):
    n_work, tab_len = tab.shape
    n_tok = x1.shape[0]
    nf = D_FF // tf
    n_out = TOP_K * n_tok + MOE_ROWS
    assert n_tok & (n_tok - 1) == 0 and MOE_ROWS <= n_tok and MOE_ROWS % MOE_BLOCK == 0
    assert n_tok % SUBLANES == 0 and n_out % SUBLANES == 0 and tab_len >= MOE_ROWS

    def tile(w, s, nbr):
        return jnp.where(nbr[w] > 0, s, nf - 1)

    def colmap(off):
        return lambda w, s, ie, nbr: (ie[w], 0, off + tile(w, s, nbr))

    def rowmap(w, s, ie, nbr):
        return ie[w], tile(w, s, nbr), 0

    grid_spec = pltpu.PrefetchScalarGridSpec(
        num_scalar_prefetch=2,
        grid=(n_work, nf),
        in_specs=[
            pl.BlockSpec(memory_space=pl.ANY),
            pl.BlockSpec(memory_space=pl.ANY),
            pl.BlockSpec((None, D_MODEL, tf), colmap(0)),
            pl.BlockSpec((None, D_MODEL, tf), colmap(nf)),
            pl.BlockSpec((None, 1, tf), colmap(0)),
            pl.BlockSpec((None, 1, tf), colmap(nf)),
            pl.BlockSpec((None, tf, D_MODEL), rowmap),
            pl.BlockSpec((None, 1, D_MODEL), lambda w, s, ie, nbr: (ie[w], 0, 0)),
        ],
        out_specs=pl.BlockSpec(memory_space=pl.ANY),
        scratch_shapes=[pltpu.SMEM((tab_len,), jnp.int32),
                        pltpu.VMEM((MOE_ROWS, D_MODEL), BF16),
                        pltpu.VMEM((MOE_ROWS // SUBLANES, SUBLANES, D_MODEL), F32),
                        pltpu.SemaphoreType.DMA, pltpu.SemaphoreType.DMA],
    )
    y = pl.pallas_call(
        functools.partial(_ffn_kernel, nf=nf, n_tok=n_tok, max_blocks=MOE_ROWS // MOE_BLOCK),
        out_shape=jax.ShapeDtypeStruct((n_out // SUBLANES, SUBLANES, D_MODEL), F32),
        grid_spec=grid_spec,
        compiler_params=_params("arbitrary", "arbitrary"),
        name="experts",
    )(item_e, item_nb, tab, x1.reshape(n_tok // SUBLANES, SUBLANES, D_MODEL), w_gate_up, w_gate_up,
      b_gate_up, b_gate_up, w_down, b_down)
    return y.reshape(n_out, D_MODEL)


def _final_kernel(*refs):
    y_refs = refs[:TOP_K]
    x1_ref, gate_ref, g_ref, b_ref, o_ref = refs[TOP_K:]
    ffn = gate_ref[:, 0:1] * y_refs[0][...]
    for k in range(1, TOP_K):
        ffn = ffn + gate_ref[:, k:k + 1] * y_refs[k][...]
    o_ref[...] = _layer_norm(DEEPNORM_ALPHA * x1_ref[...] + ffn, g_ref[...], b_ref[...])


def _final(y, x1, gates, g, b, *, tm=256):
    s = x1.shape[0]
    plane = lambda k: pl.BlockSpec((tm, D_MODEL), lambda i: (k * (s // tm) + i, 0))
    return pl.pallas_call(
        _final_kernel,
        out_shape=jax.ShapeDtypeStruct((s, D_MODEL), F32),
        grid=(s // tm,),
        in_specs=[plane(k) for k in range(TOP_K)] + [
            pl.BlockSpec((tm, D_MODEL), lambda i: (i, 0)),
            pl.BlockSpec((tm, LANES), lambda i: (i, 0)),
            pl.BlockSpec((1, D_MODEL), lambda i: (0, 0)),
            pl.BlockSpec((1, D_MODEL), lambda i: (0, 0))],
        out_specs=pl.BlockSpec((tm, D_MODEL), lambda i: (i, 0)),
        compiler_params=_params("parallel"),
        name="combine_final_ln",
    )(*([y] * TOP_K), x1, gates, g, b)


def _inv_freq(dim, width):
    half = dim // 2
    lane = np.arange(LANES)
    f = ROPE_THETA ** (-(lane % half).astype(np.float32) * 2.0 / dim)
    return jnp.asarray(np.where(lane < width, f, 0.0).astype(np.float32)).reshape(1, LANES)


def _layer(x, pos, w_in, g_q_a, w_q_b, g_kv_a, w_kv_b, g_out_mla, g_out_moba, w_o, ln1_g, ln1_b,
           w_router, b_router, w_gate_up, b_gate_up, w_down, b_down, ln2_g, ln2_b):
    s = x.shape[0]
    mla_cols = Q_LORA + KV_LORA + QK_ROPE
    w_in_p = jnp.concatenate(
        [w_in[:, :mla_cols], jnp.zeros((D_MODEL, 1024 - mla_cols), w_in.dtype), w_in[:, mla_cols:]],
        axis=1).astype(BF16)
    wq = w_q_b.reshape(Q_LORA, MLA_HEADS, QK_NOPE + QK_ROPE)
    wqn = wq[:, :, :QK_NOPE].reshape(Q_LORA, MLA_HEADS * LANES).astype(BF16)
    wqr = jnp.pad(wq[:, :, QK_NOPE:], ((0, 0), (0, 0), (0, LANES - QK_ROPE))).reshape(
        Q_LORA, MLA_HEADS * LANES).astype(BF16)
    wkv = w_kv_b.reshape(KV_LORA, MLA_HEADS, QK_NOPE + V_HEAD)
    wkn = wkv[:, :, :QK_NOPE].reshape(KV_LORA, MLA_HEADS * LANES).astype(BF16)
    wv = wkv[:, :, QK_NOPE:].reshape(KV_LORA, MLA_WIDTH).astype(BF16)
    pos_col = pos.reshape(s, 1)
    row = lambda a: a.reshape(1, -1)

    h = _inproj(x, w_in_p)
    q_mla, k_mla, v_mla = _mla_prep(h, pos_col, _inv_freq(QK_ROPE, QK_ROPE), row(g_q_a),
                                    row(g_kv_a), wqn, wqr, wkn, wv)
    invf_moba = _inv_freq(MOBA_HEAD_DIM, LANES)
    k_moba, kbar = _moba_kprep(h, pos_col, invf_moba)
    q_moba = _moba_qprep(h, pos_col, invf_moba, kbar)
    o_mla = _flash(q_mla, k_mla, v_mla, 0, MLA_HEADS)
    o_moba = _flash(q_moba, k_moba, h, 3 * MOBA_WIDTH // LANES, MOBA_HEADS)

    x1, idx_t, pos_t, gates, cnt = _outproj(o_mla, o_moba, row(g_out_mla), row(g_out_moba),
                                            w_o.astype(BF16), x, row(ln1_g), row(ln1_b),
                                            w_router.T, b_router.reshape(N_EXPERTS, 1))

    n_work = -(-s * TOP_K // MOE_ROWS) + N_EXPERTS
    counts = cnt[:, 0]
    items_per_e = (counts + MOE_ROWS - 1) // MOE_ROWS
    item_end = jnp.cumsum(items_per_e)
    item_start = item_end - items_per_e
    w_idx = jnp.arange(n_work)
    item_e = jnp.minimum(jnp.sum(w_idx[:, None] >= item_end[None, :], axis=1), N_EXPERTS - 1)
    item_rows = jnp.clip(counts[item_e] - (w_idx - item_start[item_e]) * MOE_ROWS, 0, MOE_ROWS)
    item_nb = jnp.where(w_idx < item_end[-1], (item_rows + MOE_BLOCK - 1) // MOE_BLOCK, 0)
    e_k = idx_t[:TOP_K]
    p_k = pos_t[:TOP_K]
    start_k = jnp.sum(jnp.where(e_k[..., None] == jnp.arange(N_EXPERTS), item_start, 0), axis=-1)
    item_row = ((start_k + p_k // MOE_ROWS) * MOE_ROWS + p_k % MOE_ROWS).astype(jnp.int32)
    out_row = (jnp.arange(TOP_K, dtype=jnp.int32)[:, None] * s
               + jnp.arange(s, dtype=jnp.int32)[None, :])
    dump = TOP_K * s + jnp.arange(n_work * MOE_ROWS, dtype=jnp.int32) % MOE_ROWS
    tab = dump.at[item_row.reshape(-1)].set(out_row.reshape(-1))
    tab = jnp.pad(tab.reshape(n_work, MOE_ROWS), ((0, 0), (0, MOE_TAB_LEN - MOE_ROWS)))

    y = _ffn(item_e.astype(jnp.int32), item_nb.astype(jnp.int32), tab, x1, w_gate_up,
             b_gate_up.reshape(N_EXPERTS, 1, 2 * D_FF), w_down, b_down.reshape(N_EXPERTS, 1, D_MODEL))
    return _final(y, x1, gates, row(ln2_g), row(ln2_b))


def kernel(x, positions, w_in, g_q_a, w_q_b, g_kv_a, w_kv_b, g_out_mla, g_out_moba, w_o, ln1_g, ln1_b,
           w_router, b_router, w_gate_up, b_gate_up, w_down, b_down, ln2_g, ln2_b):
    b, s, d = x.shape
    assert b == 1 and s == SEQ and d == D_MODEL and w_in.shape[0] == DEPTH
    hcur = x[0]
    for l in range(DEPTH):
        hcur = _layer(hcur, positions[0], w_in[l], g_q_a[l], w_q_b[l], g_kv_a[l], w_kv_b[l],
                      g_out_mla[l], g_out_moba[l], w_o[l], ln1_g[l], ln1_b[l], w_router[l],
                      b_router[l], w_gate_up[l], b_gate_up[l], w_down[l], b_down[l], ln2_g[l],
                      ln2_b[l])
    return hcur[None]
```

```python
import functools

import numpy as np
import jax
import jax.numpy as jnp
from jax import lax
from jax.experimental import pallas as pl
from jax.experimental.pallas import tpu as pltpu

D_MODEL = 2048
SEQ = 8192
MLA_HEADS = 8
QK_NOPE = 128
QK_ROPE = 64
V_HEAD = 128
Q_LORA = 512
KV_LORA = 256
MLA_WIDTH = MLA_HEADS * V_HEAD
MOBA_HEADS = 8
MOBA_HEAD_DIM = 128
MOBA_WIDTH = MOBA_HEADS * MOBA_HEAD_DIM
MOBA_BLOCK = 256
MOBA_TOPK = 3
N_MOBA_BLOCKS = SEQ // MOBA_BLOCK
ROPE_THETA = 10000.0
N_EXPERTS = 32
TOP_K = 4
D_FF = D_MODEL
SWIGLU_LIMIT = 7.0
SWIGLU_ALPHA = 1.702
MOE_BLOCK = 256
MOE_ROWS = 1536
DEPTH = 1
DEEPNORM_ALPHA = float((2 * DEPTH) ** 0.25)
RMS_EPS = 1e-6
LN_EPS = 1e-5

LANES = 128
SUBLANES = 8
MOE_TAB_LEN = 2048
FLASH_HEADS_PER_STEP = 1
HEAD_SLOT = 2 * LANES
H_COLS = 4096
MASK_BIAS = -float(2 ** 17)
V7X_VMEM_LIMIT = 56 * 1024 * 1024

F32 = jnp.float32
BF16 = jnp.bfloat16


def _params(*semantics):
    return pltpu.CompilerParams(dimension_semantics=semantics, vmem_limit_bytes=V7X_VMEM_LIMIT)


def _rms(xf, g):
    return xf * lax.rsqrt(jnp.mean(xf * xf, axis=-1, keepdims=True) + RMS_EPS) * g


def _layer_norm(xf, g, b):
    mu = jnp.mean(xf, axis=-1, keepdims=True)
    xc = xf - mu
    var = jnp.mean(xc * xc, axis=-1, keepdims=True)
    return xc * lax.rsqrt(var + LN_EPS) * g + b


def _nt_dot(a, b, **kw):
    return lax.dot_general(a, b, (((1,), (1,)), ((), ())), preferred_element_type=F32, **kw)


def _topk_rank(vals, row_idx):
    n = vals.shape[0]
    rank = jnp.zeros(vals.shape, jnp.int32)
    for jp in range(n):
        vj = vals[jp:jp + 1, :]
        ahead = (vj > vals) | ((vj == vals) & (jp < row_idx))
        rank = rank + ahead.astype(jnp.int32)
    return rank


def _inproj_kernel(x_ref, w_ref, o_ref, xb_ref):
    @pl.when(pl.program_id(1) == 0)
    def _():
        xb_ref[...] = x_ref[...].astype(BF16)

    o_ref[...] = jnp.dot(xb_ref[...], w_ref[...], preferred_element_type=F32).astype(o_ref.dtype)


def _inproj(x, w_bf16, *, tm=1024, tn=512):
    m, k = x.shape
    n = w_bf16.shape[1]
    return pl.pallas_call(
        _inproj_kernel,
        out_shape=jax.ShapeDtypeStruct((m, n), BF16),
        grid=(m // tm, n // tn),
        in_specs=[pl.BlockSpec((tm, k), lambda i, j: (i, 0)),
                  pl.BlockSpec((k, tn), lambda i, j: (0, j))],
        out_specs=pl.BlockSpec((tm, tn), lambda i, j: (i, j)),
        scratch_shapes=[pltpu.VMEM((tm, k), BF16)],
        compiler_params=_params("parallel", "arbitrary"),
        name="inproj",
    )(x, w_bf16)


def _rope_tables_mla(pos_ref, invf_ref):
    ang = pos_ref[...].astype(F32) * invf_ref[...]
    lane = lax.broadcasted_iota(jnp.int32, ang.shape, 1)
    cos = jnp.cos(ang)
    sin = jnp.sin(ang)
    half = QK_ROPE // 2
    sin_a = jnp.where(lane < half, -sin, 0.0)
    sin_b = jnp.where((lane >= half) & (lane < QK_ROPE), sin, 0.0)
    return cos, sin_a, sin_b


def _rope_mla(t, cos, sin_a, sin_b):
    return (t * cos + pltpu.roll(t, LANES - QK_ROPE // 2, 1) * sin_a
            + pltpu.roll(t, QK_ROPE // 2, 1) * sin_b)


def _rope_tables_moba(pos_ref, invf_ref):
    ang = pos_ref[...].astype(F32) * invf_ref[...]
    lane = lax.broadcasted_iota(jnp.int32, ang.shape, 1)
    cos = jnp.cos(ang)
    sin = jnp.sin(ang)
    return cos, jnp.where(lane < MOBA_HEAD_DIM // 2, -sin, sin)


def _rope_moba(t, cos, sin_signed):
    return t * cos + pltpu.roll(t, MOBA_HEAD_DIM // 2, 1) * sin_signed


def _mla_prep_kernel(h_ref, pos_ref, invf_ref, gq_ref, gkv_ref, wqn_ref, wqr_ref, wkn_ref, wv_ref,
                     q_ref, k_ref, v_ref):
    scale = (QK_NOPE + QK_ROPE) ** -0.5
    cos, sin_a, sin_b = _rope_tables_mla(pos_ref, invf_ref)
    hq = h_ref[:, 0:Q_LORA].astype(F32)
    hkv = h_ref[:, Q_LORA:Q_LORA + KV_LORA].astype(F32)
    hkr = h_ref[:, Q_LORA + KV_LORA:Q_LORA + KV_LORA + LANES].astype(F32)

    qn = _rms(hq, gq_ref[...]).astype(BF16)
    q_nope = jnp.dot(qn, wqn_ref[...], preferred_element_type=F32)
    q_rope = jnp.dot(qn, wqr_ref[...], preferred_element_type=F32)
    kvn = _rms(hkv, gkv_ref[...]).astype(BF16)
    k_nope = jnp.dot(kvn, wkn_ref[...], preferred_element_type=F32)
    v_ref[...] = jnp.dot(kvn, wv_ref[...], preferred_element_type=F32).astype(BF16)
    kpe = _rope_mla(hkr, cos, sin_a, sin_b).astype(BF16)
    for h in range(MLA_HEADS):
        lo = h * HEAD_SLOT
        hs = slice(h * LANES, (h + 1) * LANES)
        q_ref[:, lo:lo + LANES] = (q_nope[:, hs] * scale).astype(BF16)
        q_ref[:, lo + LANES:lo + HEAD_SLOT] = (
            _rope_mla(q_rope[:, hs], cos, sin_a, sin_b) * scale).astype(BF16)
        k_ref[:, lo:lo + LANES] = k_nope[:, hs].astype(BF16)
        k_ref[:, lo + LANES:lo + HEAD_SLOT] = kpe


def _mla_prep(h, pos_col, invf, gq, gkv, wqn, wqr, wkn, wv, *, tm=512):
    s = h.shape[0]
    full = lambda a: pl.BlockSpec(a.shape, lambda i: (0,) * a.ndim)
    return pl.pallas_call(
        _mla_prep_kernel,
        out_shape=(jax.ShapeDtypeStruct((s, MLA_HEADS * HEAD_SLOT), BF16),
                   jax.ShapeDtypeStruct((s, MLA_HEADS * HEAD_SLOT), BF16),
                   jax.ShapeDtypeStruct((s, MLA_WIDTH), BF16)),
        grid=(s // tm,),
        in_specs=[pl.BlockSpec((tm, 1024), lambda i: (i, 0)),
                  pl.BlockSpec((tm, 1), lambda i: (i, 0)),
                  full(invf), full(gq), full(gkv), full(wqn), full(wqr), full(wkn), full(wv)],
        out_specs=(pl.BlockSpec((tm, MLA_HEADS * HEAD_SLOT), lambda i: (i, 0)),
                   pl.BlockSpec((tm, MLA_HEADS * HEAD_SLOT), lambda i: (i, 0)),
                   pl.BlockSpec((tm, MLA_WIDTH), lambda i: (i, 0))),
        compiler_params=_params("parallel"),
        name="mla_prep",
    )(h, pos_col, invf, gq, gkv, wqn, wqr, wkn, wv)


def _moba_kprep_kernel(h_ref, pos_ref, invf_ref, k_ref, kbar_ref, *, rows):
    cos, sin_signed = _rope_tables_moba(pos_ref, invf_ref)
    row = pl.program_id(0) * rows + lax.broadcasted_iota(jnp.int32, (rows, LANES), 0)
    lane = lax.broadcasted_iota(jnp.int32, (rows, LANES), 1)
    onehot = (lane == row // MOBA_BLOCK).astype(BF16)
    for h in range(MOBA_HEADS):
        lo = h * HEAD_SLOT
        kr = _rope_moba(h_ref[:, h * LANES:(h + 1) * LANES].astype(F32), cos, sin_signed)
        k_ref[:, lo:lo + LANES] = kr.astype(BF16)
        k_ref[:, lo + LANES:lo + HEAD_SLOT] = onehot
        for b in range(rows // MOBA_BLOCK):
            kbar_ref[b:b + 1, h * LANES:(h + 1) * LANES] = jnp.mean(
                kr[b * MOBA_BLOCK:(b + 1) * MOBA_BLOCK], axis=0, keepdims=True)


def _moba_kprep(h, pos_col, invf, *, rows=2048):
    s = h.shape[0]
    return pl.pallas_call(
        functools.partial(_moba_kprep_kernel, rows=rows),
        out_shape=(jax.ShapeDtypeStruct((s, MOBA_HEADS * HEAD_SLOT), BF16),
                   jax.ShapeDtypeStruct((s // MOBA_BLOCK, MOBA_WIDTH), F32)),
        grid=(s // rows,),
        in_specs=[pl.BlockSpec((rows, MOBA_WIDTH), lambda i: (i, 2)),
                  pl.BlockSpec((rows, 1), lambda i: (i, 0)),
                  pl.BlockSpec((1, LANES), lambda i: (0, 0))],
        out_specs=(pl.BlockSpec((rows, MOBA_HEADS * HEAD_SLOT), lambda i: (i, 0)),
                   pl.BlockSpec((rows // MOBA_BLOCK, MOBA_WIDTH), lambda i: (i, 0))),
        compiler_params=_params("parallel"),
        name="moba_kprep",
    )(h, pos_col, invf)


def _moba_qprep_kernel(h_ref, pos_ref, invf_ref, kbar_ref, q_ref, *, tm):
    scale = MOBA_HEAD_DIM ** -0.5
    cos, sin_signed = _rope_tables_moba(pos_ref, invf_ref)
    nb = N_MOBA_BLOCKS
    tok = pl.program_id(0) * tm + lax.broadcasted_iota(jnp.int32, (1, tm), 1)
    qblk = tok // MOBA_BLOCK
    blk = lax.broadcasted_iota(jnp.int32, (nb, 1), 0)
    past = blk < qblk
    own = blk == qblk
    for h in range(MOBA_HEADS):
        lo = h * HEAD_SLOT
        qr = _rope_moba(h_ref[:, h * LANES:(h + 1) * LANES].astype(F32), cos, sin_signed)
        gate = _nt_dot(kbar_ref[:, h * LANES:(h + 1) * LANES], qr,
                       precision=lax.Precision.HIGHEST)
        gate = jnp.where(past, gate, -jnp.inf)
        sel = past & (_topk_rank(gate, blk) < MOBA_TOPK)
        bias = jnp.where(sel | own, 0.0, MASK_BIAS)
        bias = jnp.concatenate([bias, jnp.zeros((LANES - nb, tm), F32)], axis=0)
        q_ref[:, lo:lo + LANES] = (qr * scale).astype(BF16)
        q_ref[:, lo + LANES:lo + HEAD_SLOT] = bias.T.astype(BF16)


def _moba_qprep(h, pos_col, invf, kbar, *, tm=512):
    s = h.shape[0]
    return pl.pallas_call(
        functools.partial(_moba_qprep_kernel, tm=tm),
        out_shape=jax.ShapeDtypeStruct((s, MOBA_HEADS * HEAD_SLOT), BF16),
        grid=(s // tm,),
        in_specs=[pl.BlockSpec((tm, MOBA_WIDTH), lambda i: (i, 1)),
                  pl.BlockSpec((tm, 1), lambda i: (i, 0)),
                  pl.BlockSpec((1, LANES), lambda i: (0, 0)),
                  pl.BlockSpec(kbar.shape, lambda i: (0, 0))],
        out_specs=pl.BlockSpec((tm, MOBA_HEADS * HEAD_SLOT), lambda i: (i, 0)),
        compiler_params=_params("parallel"),
        name="moba_qprep",
    )(h, pos_col, invf, kbar)


def _flash_kernel(qa_ref, qb_ref, k_ref, v_ref, o_ref, *scratch, tq, nq, hp):
    p_idx = pl.program_id(1)
    nc = tq // LANES
    ones = jnp.ones((tq, LANES), BF16)
    heads = [scratch[3 * h:3 * h + 3] for h in range(hp)]
    for h, (q2_ref, m2_ref, acc2_ref) in enumerate(heads):
        q2_ref[0] = qa_ref[:, h * HEAD_SLOT:(h + 1) * HEAD_SLOT]
        q2_ref[1] = qb_ref[:, h * HEAD_SLOT:(h + 1) * HEAD_SLOT]
        m2_ref[...] = jnp.full(m2_ref.shape, -jnp.inf, F32)
        acc2_ref[...] = jnp.zeros(acc2_ref.shape, F32)

    def kv_rows(j):
        return pl.ds(pl.multiple_of(j * tq, tq), tq)

    def scores(h, sel, j):
        return _nt_dot(heads[h][0][sel], k_ref[kv_rows(j), h * HEAD_SLOT:(h + 1) * HEAD_SLOT])

    def consume(h, sel, j, s):
        _, m2_ref, acc2_ref = heads[h]
        v = v_ref[kv_rows(j), h * LANES:(h + 1) * LANES]
        chunks = [s[:, c * LANES:(c + 1) * LANES] for c in range(nc)]
        part = functools.reduce(jnp.maximum, chunks)
        m_old = m2_ref[sel]
        m_new = jnp.maximum(m_old, jnp.max(part, axis=1, keepdims=True))
        alpha = jnp.exp(m_old - m_new)
        p = jnp.concatenate([jnp.exp(ch - m_new).astype(BF16) for ch in chunks], axis=1)
        pv = jnp.dot(p, jnp.concatenate([v, ones], axis=1), preferred_element_type=F32)
        acc2_ref[sel] = jnp.concatenate([alpha, alpha], axis=1) * acc2_ref[sel] + pv
        m2_ref[sel] = m_new

    def diagonal(s):
        r = lax.broadcasted_iota(jnp.int32, s.shape, 0)
        c = lax.broadcasted_iota(jnp.int32, s.shape, 1)
        return jnp.where(c <= r, s, -jnp.inf)

    steps = []
    for t in range(nq - 1):
        sel = (t >= p_idx).astype(jnp.int32)
        steps += [(h, sel, t - sel * p_idx, False) for h in range(hp)]
    steps += [(h, 0, p_idx, True) for h in range(hp)]
    steps += [(h, 1, nq - 1 - p_idx, True) for h in range(hp)]

    ahead = hp
    pending = [scores(*st[:3]) for st in steps[:ahead]]
    for t, (h, sel, j, diag) in enumerate(steps):
        s = pending.pop(0)
        if t + ahead < len(steps):
            pending.append(scores(*steps[t + ahead][:3]))
        consume(h, sel, j, diagonal(s) if diag else s)
    for h, (_, _, acc2_ref) in enumerate(heads):
        for sel in range(2):
            o_ref[sel, :, h * LANES:(h + 1) * LANES] = (
                acc2_ref[sel, :, :LANES] / acc2_ref[sel, :, LANES:]).astype(o_ref.dtype)


def _flash(q, k, v, v_col0, n_heads, *, tq=512, hp=FLASH_HEADS_PER_STEP):
    s = q.shape[0]
    nq = s // tq
    assert n_heads % hp == 0 and v_col0 % hp == 0
    per_head = [pltpu.VMEM((2, tq, HEAD_SLOT), BF16), pltpu.VMEM((2, tq, LANES), F32),
                pltpu.VMEM((2, tq, 2 * LANES), F32)]
    return pl.pallas_call(
        functools.partial(_flash_kernel, tq=tq, nq=nq, hp=hp),
        out_shape=jax.ShapeDtypeStruct((2, s // 2, n_heads * LANES), BF16),
        grid=(n_heads // hp, nq // 2),
        in_specs=[pl.BlockSpec((tq, hp * HEAD_SLOT), lambda h, p: (p, h)),
                  pl.BlockSpec((tq, hp * HEAD_SLOT), lambda h, p: (nq - 1 - p, h)),
                  pl.BlockSpec((s, hp * HEAD_SLOT), lambda h, p: (0, h)),
                  pl.BlockSpec((s, hp * LANES), lambda h, p: (0, v_col0 // hp + h))],
        out_specs=pl.BlockSpec((2, tq, hp * LANES), lambda h, p: (0, p, h)),
        scratch_shapes=per_head * hp,
        compiler_params=_params("parallel", "parallel"),
        name="flash",
    )(q, q, k, v)


def _flash_row_block(i, tm, s, tq=512):
    per_tile = tq // tm
    tile = i // per_tile
    sub = i % per_tile
    nq = s // tq
    hi = (tile >= nq // 2).astype(jnp.int32)
    return hi, jnp.where(hi == 1, nq - 1 - tile, tile) * per_tile + sub


def _outproj_kernel(om_ref, ob_ref, gm_ref, gb_ref, wo_ref, x_ref, lg_ref, lb_ref, wrt_ref, br_ref,
                    x1_ref, idx_ref, pos_ref, gate_ref, cnt_ref):
    @pl.when(pl.program_id(0) == 0)
    def _():
        cnt_ref[...] = jnp.zeros(cnt_ref.shape, jnp.int32)

    a = jnp.concatenate([_rms(om_ref[...].astype(F32), gm_ref[...]).astype(BF16),
                         _rms(ob_ref[...].astype(F32), gb_ref[...]).astype(BF16)], axis=1)
    mixed = jnp.dot(a, wo_ref[...], preferred_element_type=F32)
    x1 = _layer_norm(DEEPNORM_ALPHA * x_ref[...] + mixed, lg_ref[...], lb_ref[...])
    x1_ref[...] = x1

    logits = _nt_dot(wrt_ref[...], x1, precision=lax.Precision.HIGHEST) + br_ref[...]
    tm = logits.shape[1]
    eidx = lax.broadcasted_iota(jnp.int32, (N_EXPERTS, 1), 0)
    rank = _topk_rank(logits, eidx)
    sel = rank < TOP_K
    mx = jnp.max(logits, axis=0, keepdims=True)
    p = jnp.where(sel, jnp.exp(logits - mx), 0.0)
    gates = p / jnp.sum(p, axis=0, keepdims=True)

    before = (lax.broadcasted_iota(jnp.int32, (tm, tm), 0)
              < lax.broadcasted_iota(jnp.int32, (tm, tm), 1)).astype(BF16)
    prefix = jnp.dot(sel.astype(BF16), before, preferred_element_type=F32).astype(jnp.int32)
    pos = cnt_ref[:, 0:1] + prefix
    cnt_ref[...] = cnt_ref[...] + jnp.sum(sel.astype(jnp.int32), axis=1, keepdims=True)

    pick = lambda vals, k, zero: jnp.sum(jnp.where(rank == k, vals, zero), axis=0, keepdims=True)
    pad_i = [jnp.zeros((8 - TOP_K, tm), jnp.int32)]
    idx_ref[...] = jnp.concatenate([pick(eidx, k, 0) for k in range(TOP_K)] + pad_i, axis=0)
    pos_ref[...] = jnp.concatenate([pick(pos, k, 0) for k in range(TOP_K)] + pad_i, axis=0)
    gate_rows = jnp.concatenate([pick(gates, k, 0.0) for k in range(TOP_K)]
                                + [jnp.zeros((LANES - TOP_K, tm), F32)], axis=0)
    gate_ref[...] = gate_rows.T


def _outproj(o_mla, o_moba, g_mla, g_moba, wo, x, ln_g, ln_b, wr_t, b_r, *, tm=512):
    s = x.shape[0]
    full = lambda a: pl.BlockSpec(a.shape, lambda i: (0,) * a.ndim)
    return pl.pallas_call(
        _outproj_kernel,
        out_shape=(jax.ShapeDtypeStruct((s, D_MODEL), F32),
                   jax.ShapeDtypeStruct((8, s), jnp.int32),
                   jax.ShapeDtypeStruct((8, s), jnp.int32),
                   jax.ShapeDtypeStruct((s, LANES), F32),
                   jax.ShapeDtypeStruct((N_EXPERTS, LANES), jnp.int32)),
        grid=(s // tm,),
        in_specs=[pl.BlockSpec((None, tm, MLA_WIDTH), lambda i: (*_flash_row_block(i, tm, s), 0)),
                  pl.BlockSpec((None, tm, MOBA_WIDTH), lambda i: (*_flash_row_block(i, tm, s), 0)),
                  full(g_mla), full(g_moba), full(wo),
                  pl.BlockSpec((tm, D_MODEL), lambda i: (i, 0)),
                  full(ln_g), full(ln_b), full(wr_t), full(b_r)],
        out_specs=(pl.BlockSpec((tm, D_MODEL), lambda i: (i, 0)),
                   pl.BlockSpec((8, tm), lambda i: (0, i)),
                   pl.BlockSpec((8, tm), lambda i: (0, i)),
                   pl.BlockSpec((tm, LANES), lambda i: (i, 0)),
                   pl.BlockSpec((N_EXPERTS, LANES), lambda i: (0, 0))),
        compiler_params=_params("arbitrary"),
        name="outproj_router",
    )(o_mla, o_moba, g_mla, g_moba, wo, x, ln_g, ln_b, wr_t, b_r)


def _ffn_kernel(ie_ref, nb_ref, tab_hbm, x1_hbm, wg_ref, wu_ref, bg_ref, bu_ref, wd_ref, bd_ref,
                y_hbm, tab_smem, xf_ref, xb_ref, y_ref, sem_tab, sem_in, sem_out,
                *, nf, n_tok, max_blocks):
    w = pl.program_id(0)
    s = pl.program_id(1)
    n_work = pl.num_programs(0)
    nb = nb_ref[w]
    nb_prev = nb_ref[jnp.maximum(w - 1, 0)]
    w_next = jnp.minimum(w + 1, n_work - 1)
    has_next = (w + 1 < n_work) & (nb_ref[w_next] > 0)
    sub = SUBLANES
    tiles_per_block = MOE_BLOCK // sub
    slot = lax.bitwise_and(w, 1)

    tab_len = tab_smem.shape[0] // 2

    def load_table(item, slot_):
        dst = tab_smem.at[pl.ds(pl.multiple_of(slot_ * tab_len, tab_len), tab_len)]
        cp = pltpu.make_async_copy(tab_hbm.at[item], dst, sem_tab)
        cp.start()
        cp.wait()

    def for_rows(n_blocks, slot_, fn):
        def body(b, carry):
            base = slot_ * tab_len + b * MOE_BLOCK
            for i in range(MOE_BLOCK):
                fn(b * tiles_per_block + i // sub, i % sub, tab_smem[base + i])
            return carry
        lax.fori_loop(0, n_blocks, body, 0)

    def wait_tiles(n_blocks, make_copy):
        def body(g, carry):
            make_copy().wait()
            return carry
        lax.fori_loop(0, n_blocks * tiles_per_block, body, 0)

    def hbm_row(ref, r):
        return ref.at[lax.shift_right_logical(r, 3), pl.ds(lax.bitwise_and(r, sub - 1), 1)]

    def start_gather(n_blocks, slot_):
        for_rows(n_blocks, slot_, lambda g, j, e: pltpu.make_async_copy(
            hbm_row(x1_hbm, lax.bitwise_and(e, n_tok - 1)), xf_ref.at[g, pl.ds(j, 1)],
            sem_in).start())

    def wait_scatter(n_blocks):
        wait_tiles(n_blocks, lambda: pltpu.make_async_copy(y_ref.at[0], y_hbm.at[0], sem_out))

    @pl.when((nb > 0) & (s == 0))
    def _():
        @pl.when(w == 0)
        def _():
            load_table(0, 0)
            start_gather(nb, 0)
            y_ref[...] = jnp.zeros(y_ref.shape, F32)
            dump = pltpu.make_async_copy(
                y_ref, y_hbm.at[pl.ds(TOP_K * n_tok // sub, MOE_ROWS // sub)], sem_out)
            dump.start()
            dump.wait()

        wait_tiles(nb, lambda: pltpu.make_async_copy(x1_hbm.at[0], xf_ref.at[0], sem_in))

        def cast(b, carry):
            tiles = pl.ds(b * tiles_per_block, tiles_per_block)
            rows = pl.ds(pl.multiple_of(b * MOE_BLOCK, MOE_BLOCK), MOE_BLOCK)
            xb_ref[rows, :] = xf_ref[tiles].reshape(MOE_BLOCK, D_MODEL).astype(BF16)
            return carry

        lax.fori_loop(0, nb, cast, 0)

        @pl.when(w > 0)
        def _():
            wait_scatter(nb_prev)

        bias = jnp.broadcast_to(bd_ref[...], (tiles_per_block, sub, D_MODEL))

        def init(b, carry):
            y_ref[pl.ds(b * tiles_per_block, tiles_per_block)] = bias
            return carry

        lax.fori_loop(0, nb, init, 0)

    def step(m):
        x = xb_ref[0:m, :]
        g = jnp.dot(x, wg_ref[...].astype(BF16), preferred_element_type=F32) + bg_ref[...]
        u = jnp.dot(x, wu_ref[...].astype(BF16), preferred_element_type=F32) + bu_ref[...]
        g = jnp.minimum(g, SWIGLU_LIMIT)
        u = jnp.clip(u, -SWIGLU_LIMIT, SWIGLU_LIMIT)
        act = ((u + 1.0) * (g * (1.0 / (1.0 + jnp.exp(-SWIGLU_ALPHA * g))))).astype(BF16)
        down = jnp.dot(act, wd_ref[...].astype(BF16), preferred_element_type=F32)
        y_ref[0:m // sub] = y_ref[0:m // sub] + down.reshape(m // sub, sub, D_MODEL)

    for k in range(1, max_blocks + 1):
        pl.when(nb == k)(functools.partial(step, k * MOE_BLOCK))

    @pl.when((nb > 0) & (s == nf - 1))
    def _():
        for_rows(nb, slot, lambda g, j, e: pltpu.make_async_copy(
            y_ref.at[g, pl.ds(j, 1)], hbm_row(y_hbm, e), sem_out).start())

        @pl.when(has_next)
        def _():
            load_table(w_next, 1 - slot)
            start_gather(nb_ref[w_next], 1 - slot)

        @pl.when(jnp.logical_not(has_next))
        def _():
            wait_scatter(nb)


def _ffn(item_e, item_nb, tab, x1, w_gate_up, b_gate_up, w_down, b_down, *, tf=256):
    n_work, tab_len = tab.shape
    n_tok = x1.shape[0]
    nf = D_FF // tf
    n_out = TOP_K * n_tok + MOE_ROWS
    assert n_tok & (n_tok - 1) == 0 and MOE_ROWS <= n_tok and MOE_ROWS % MOE_BLOCK == 0
    assert n_tok % SUBLANES == 0 and n_out % SUBLANES == 0 and tab_len >= MOE_ROWS

    def tile(w, s, nbr):
        return jnp.where(nbr[w] > 0, s, nf - 1)

    def colmap(off):
        return lambda w, s, ie, nbr: (ie[w], 0, off + tile(w, s, nbr))

    def rowmap(w, s, ie, nbr):
        return ie[w], tile(w, s, nbr), 0

    grid_spec = pltpu.PrefetchScalarGridSpec(
        num_scalar_prefetch=2,
        grid=(n_work, nf),
        in_specs=[
            pl.BlockSpec(memory_space=pl.ANY),
            pl.BlockSpec(memory_space=pl.ANY),
            pl.BlockSpec((None, D_MODEL, tf), colmap(0)),
            pl.BlockSpec((None, D_MODEL, tf), colmap(nf)),
            pl.BlockSpec((None, 1, tf), colmap(0)),
            pl.BlockSpec((None, 1, tf), colmap(nf)),
            pl.BlockSpec((None, tf, D_MODEL), rowmap),
            pl.BlockSpec((None, 1, D_MODEL), lambda w, s, ie, nbr: (ie[w], 0, 0)),
        ],
        out_specs=pl.BlockSpec(memory_space=pl.ANY),
        scratch_shapes=[pltpu.SMEM((2 * tab_len,), jnp.int32),
                        pltpu.VMEM((MOE_ROWS // SUBLANES, SUBLANES, D_MODEL), F32),
                        pltpu.VMEM((MOE_ROWS, D_MODEL), BF16),
                        pltpu.VMEM((MOE_ROWS // SUBLANES, SUBLANES, D_MODEL), F32),
                        pltpu.SemaphoreType.DMA, pltpu.SemaphoreType.DMA, pltpu.SemaphoreType.DMA],
    )
    y = pl.pallas_call(
        functools.partial(_ffn_kernel, nf=nf, n_tok=n_tok, max_blocks=MOE_ROWS // MOE_BLOCK),
        out_shape=jax.ShapeDtypeStruct((n_out // SUBLANES, SUBLANES, D_MODEL), F32),
        grid_spec=grid_spec,
        compiler_params=_params("arbitrary", "arbitrary"),
        name="experts",
    )(item_e, item_nb, tab, x1.reshape(n_tok // SUBLANES, SUBLANES, D_MODEL), w_gate_up, w_gate_up,
      b_gate_up, b_gate_up, w_down, b_down)
    return y.reshape(n_out, D_MODEL)


def _final_kernel(*refs):
    y_refs = refs[:TOP_K]
    x1_ref, gate_ref, g_ref, b_ref, o_ref = refs[TOP_K:]
    ffn = gate_ref[:, 0:1] * y_refs[0][...]
    for k in range(1, TOP_K):
        ffn = ffn + gate_ref[:, k:k + 1] * y_refs[k][...]
    o_ref[...] = _layer_norm(DEEPNORM_ALPHA * x1_ref[...] + ffn, g_ref[...], b_ref[...])


def _final(y, x1, gates, g, b, *, tm=256):
    s = x1.shape[0]
    plane = lambda k: pl.BlockSpec((tm, D_MODEL), lambda i: (k * (s // tm) + i, 0))
    return pl.pallas_call(
        _final_kernel,
        out_shape=jax.ShapeDtypeStruct((s, D_MODEL), F32),
        grid=(s // tm,),
        in_specs=[plane(k) for k in range(TOP_K)] + [
            pl.BlockSpec((tm, D_MODEL), lambda i: (i, 0)),
            pl.BlockSpec((tm, LANES), lambda i: (i, 0)),
            pl.BlockSpec((1, D_MODEL), lambda i: (0, 0)),
            pl.BlockSpec((1, D_MODEL), lambda i: (0, 0))],
        out_specs=pl.BlockSpec((tm, D_MODEL), lambda i: (i, 0)),
        compiler_params=_params("parallel"),
        name="combine_final_ln",
    )(*([y] * TOP_K), x1, gates, g, b)


def _inv_freq(dim, width):
    half = dim // 2
    lane = np.arange(LANES)
    f = ROPE_THETA ** (-(lane % half).astype(np.float32) * 2.0 / dim)
    return jnp.asarray(np.where(lane < width, f, 0.0).astype(np.float32)).reshape(1, LANES)


def _layer(x, pos, w_in, g_q_a, w_q_b, g_kv_a, w_kv_b, g_out_mla, g_out_moba, w_o, ln1_g, ln1_b,
           w_router, b_router, w_gate_up, b_gate_up, w_down, b_down, ln2_g, ln2_b):
    s = x.shape[0]
    mla_cols = Q_LORA + KV_LORA + QK_ROPE
    w_in_p = jnp.concatenate(
        [w_in[:, :mla_cols], jnp.zeros((D_MODEL, 1024 - mla_cols), w_in.dtype), w_in[:, mla_cols:]],
        axis=1).astype(BF16)
    wq = w_q_b.reshape(Q_LORA, MLA_HEADS, QK_NOPE + QK_ROPE)
    wqn = wq[:, :, :QK_NOPE].reshape(Q_LORA, MLA_HEADS * LANES).astype(BF16)
    wqr = jnp.pad(wq[:, :, QK_NOPE:], ((0, 0), (0, 0), (0, LANES - QK_ROPE))).reshape(
        Q_LORA, MLA_HEADS * LANES).astype(BF16)
    wkv = w_kv_b.reshape(KV_LORA, MLA_HEADS, QK_NOPE + V_HEAD)
    wkn = wkv[:, :, :QK_NOPE].reshape(KV_LORA, MLA_HEADS * LANES).astype(BF16)
    wv = wkv[:, :, QK_NOPE:].reshape(KV_LORA, MLA_WIDTH).astype(BF16)
    pos_col = pos.reshape(s, 1)
    row = lambda a: a.reshape(1, -1)

    h = _inproj(x, w_in_p)
    q_mla, k_mla, v_mla = _mla_prep(h, pos_col, _inv_freq(QK_ROPE, QK_ROPE), row(g_q_a),
                                    row(g_kv_a), wqn, wqr, wkn, wv)
    invf_moba = _inv_freq(MOBA_HEAD_DIM, LANES)
    k_moba, kbar = _moba_kprep(h, pos_col, invf_moba)
    q_moba = _moba_qprep(h, pos_col, invf_moba, kbar)
    o_mla = _flash(q_mla, k_mla, v_mla, 0, MLA_HEADS)
    o_moba = _flash(q_moba, k_moba, h, 3 * MOBA_WIDTH // LANES, MOBA_HEADS)

    x1, idx_t, pos_t, gates, cnt = _outproj(o_mla, o_moba, row(g_out_mla), row(g_out_moba),
                                            w_o.astype(BF16), x, row(ln1_g), row(ln1_b),
                                            w_router.T, b_router.reshape(N_EXPERTS, 1))

    n_work = -(-s * TOP_K // MOE_ROWS) + N_EXPERTS
    counts = cnt[:, 0]
    items_per_e = (counts + MOE_ROWS - 1) // MOE_ROWS
    item_end = jnp.cumsum(items_per_e)
    item_start = item_end - items_per_e
    w_idx = jnp.arange(n_work)
    item_e = jnp.minimum(jnp.sum(w_idx[:, None] >= item_end[None, :], axis=1), N_EXPERTS - 1)
    item_rows = jnp.clip(counts[item_e] - (w_idx - item_start[item_e]) * MOE_ROWS, 0, MOE_ROWS)
    item_nb = jnp.where(w_idx < item_end[-1], (item_rows + MOE_BLOCK - 1) // MOE_BLOCK, 0)
    e_k = idx_t[:TOP_K]
    p_k = pos_t[:TOP_K]
    start_k = jnp.sum(jnp.where(e_k[..., None] == jnp.arange(N_EXPERTS), item_start, 0), axis=-1)
    item_row = ((start_k + p_k // MOE_ROWS) * MOE_ROWS + p_k % MOE_ROWS).astype(jnp.int32)
    out_row = (jnp.arange(TOP_K, dtype=jnp.int32)[:, None] * s
               + jnp.arange(s, dtype=jnp.int32)[None, :])
    dump = TOP_K * s + jnp.arange(n_work * MOE_ROWS, dtype=jnp.int32) % MOE_ROWS
    tab = dump.at[item_row.reshape(-1)].set(out_row.reshape(-1))
    tab = jnp.pad(tab.reshape(n_work, MOE_ROWS), ((0, 0), (0, MOE_TAB_LEN - MOE_ROWS)))

    y = _ffn(item_e.astype(jnp.int32), item_nb.astype(jnp.int32), tab, x1, w_gate_up,
             b_gate_up.reshape(N_EXPERTS, 1, 2 * D_FF), w_down, b_down.reshape(N_EXPERTS, 1, D_MODEL))
    return _final(y, x1, gates, row(ln2_g), row(ln2_b))


def kernel(x, positions, w_in, g_q_a, w_q_b, g_kv_a, w_kv_b, g_out_mla, g_out_moba, w_o, ln1_g, ln1_b,
           w_router, b_router, w_gate_up, b_gate_up, w_down, b_down, ln2_g, ln2_b):
    b, s, d = x.shape
    assert b == 1 and s == SEQ and d == D_MODEL and w_in.shape[0] == DEPTH
    hcur = x[0]
    for l in range(DEPTH):
        hcur = _layer(hcur, positions[0], w_in[l], g_q_a[l], w_q_b[l], g_kv_a[l], w_kv_b[l],
                      g_out_mla[l], g_out_moba[l], w_o[l], ln1_g[l], ln1_b[l], w_router[l],
                      b_router[l], w_gate_up[l], b_gate_up[l], w_down[l], b_down[l], ln2_g[l],
                      ln2_b[l])
    return hcur[None]
```

```python
import functools

import numpy as np
import jax
import jax.numpy as jnp
from jax import lax
from jax.experimental import pallas as pl
from jax.experimental.pallas import tpu as pltpu

D_MODEL = 2048
SEQ = 8192
MLA_HEADS = 8
QK_NOPE = 128
QK_ROPE = 64
V_HEAD = 128
Q_LORA = 512
KV_LORA = 256
MLA_WIDTH = MLA_HEADS * V_HEAD
MOBA_HEADS = 8
MOBA_HEAD_DIM = 128
MOBA_WIDTH = MOBA_HEADS * MOBA_HEAD_DIM
MOBA_BLOCK = 256
MOBA_TOPK = 3
N_MOBA_BLOCKS = SEQ // MOBA_BLOCK
ROPE_THETA = 10000.0
N_EXPERTS = 32
TOP_K = 4
D_FF = D_MODEL
SWIGLU_LIMIT = 7.0
SWIGLU_ALPHA = 1.702
MOE_BLOCK = 256
MOE_ROWS = 1536
DEPTH = 1
DEEPNORM_ALPHA = float((2 * DEPTH) ** 0.25)
RMS_EPS = 1e-6
LN_EPS = 1e-5

LANES = 128
SUBLANES = 8
MOE_TAB_LEN = 2048
FLASH_HEADS_PER_STEP = 1
HEAD_SLOT = 2 * LANES
H_COLS = 4096
MASK_BIAS = -float(2 ** 17)
V7X_VMEM_LIMIT = 56 * 1024 * 1024

F32 = jnp.float32
BF16 = jnp.bfloat16


def _params(*semantics):
    return pltpu.CompilerParams(dimension_semantics=semantics, vmem_limit_bytes=V7X_VMEM_LIMIT)


def _rms(xf, g):
    return xf * lax.rsqrt(jnp.mean(xf * xf, axis=-1, keepdims=True) + RMS_EPS) * g


def _layer_norm(xf, g, b):
    mu = jnp.mean(xf, axis=-1, keepdims=True)
    xc = xf - mu
    var = jnp.mean(xc * xc, axis=-1, keepdims=True)
    return xc * lax.rsqrt(var + LN_EPS) * g + b


def _nt_dot(a, b, **kw):
    return lax.dot_general(a, b, (((1,), (1,)), ((), ())), preferred_element_type=F32, **kw)


def _topk_rank(vals, row_idx):
    n = vals.shape[0]
    rank = jnp.zeros(vals.shape, jnp.int32)
    for jp in range(n):
        vj = vals[jp:jp + 1, :]
        ahead = (vj > vals) | ((vj == vals) & (jp < row_idx))
        rank = rank + ahead.astype(jnp.int32)
    return rank


def _inproj_kernel(x_ref, w_ref, o_ref, xb_ref):
    @pl.when(pl.program_id(1) == 0)
    def _():
        xb_ref[...] = x_ref[...].astype(BF16)

    o_ref[...] = jnp.dot(xb_ref[...], w_ref[...], preferred_element_type=F32).astype(o_ref.dtype)


def _inproj(x, w_bf16, *, tm=1024, tn=1024):
    m, k = x.shape
    n = w_bf16.shape[1]
    return pl.pallas_call(
        _inproj_kernel,
        out_shape=jax.ShapeDtypeStruct((m, n), BF16),
        grid=(m // tm, n // tn),
        in_specs=[pl.BlockSpec((tm, k), lambda i, j: (i, 0)),
                  pl.BlockSpec((k, tn), lambda i, j: (0, j))],
        out_specs=pl.BlockSpec((tm, tn), lambda i, j: (i, j)),
        scratch_shapes=[pltpu.VMEM((tm, k), BF16)],
        compiler_params=_params("parallel", "arbitrary"),
        name="inproj",
    )(x, w_bf16)


def _rope_angle_kernel(pos_ref, invf_ref, cos_ref, sin_ref):
    ang = pos_ref[...].astype(F32) * invf_ref[...]
    cos_ref[...] = jnp.cos(ang)
    sin_ref[...] = jnp.sin(ang)


def _rope_angles(pos_col, *, tm=1024):
    s = pos_col.shape[0]
    lane = np.arange(LANES)
    moba_half, mla_half = MOBA_HEAD_DIM // 2, QK_ROPE // 2
    f_moba = ROPE_THETA ** (-(lane % moba_half).astype(np.float32) * 2.0 / MOBA_HEAD_DIM)
    f_mla = ROPE_THETA ** (-((lane - moba_half) % mla_half).astype(np.float32) * 2.0 / QK_ROPE)
    invf = np.where(lane < moba_half, f_moba, np.where(lane < moba_half + mla_half, f_mla, 0.0))
    invf = jnp.asarray(invf.astype(np.float32)).reshape(1, LANES)
    return pl.pallas_call(
        _rope_angle_kernel,
        out_shape=(jax.ShapeDtypeStruct((s, LANES), F32), jax.ShapeDtypeStruct((s, LANES), F32)),
        grid=(s // tm,),
        in_specs=[pl.BlockSpec((tm, 1), lambda i: (i, 0)), pl.BlockSpec((1, LANES), lambda i: (0, 0))],
        out_specs=(pl.BlockSpec((tm, LANES), lambda i: (i, 0)),
                   pl.BlockSpec((tm, LANES), lambda i: (i, 0))),
        compiler_params=_params("parallel"),
        name="rope_angles",
    )(pos_col, invf)


def _rope_tables_mla(cos_ref, sin_ref):
    c = cos_ref[...]
    s = sin_ref[...]
    lane = lax.broadcasted_iota(jnp.int32, c.shape, 1)
    half = QK_ROPE // 2
    lo = lane < half
    cos = jnp.where(lo, pltpu.roll(c, LANES // 2, 1), pltpu.roll(c, LANES - half, 1))
    sin_a = jnp.where(lo, -pltpu.roll(s, LANES // 2, 1), 0.0)
    sin_b = jnp.where((lane >= half) & (lane < QK_ROPE), pltpu.roll(s, LANES - half, 1), 0.0)
    return cos, sin_a, sin_b


def _rope_mla(t, cos, sin_a, sin_b):
    return (t * cos + pltpu.roll(t, LANES - QK_ROPE // 2, 1) * sin_a
            + pltpu.roll(t, QK_ROPE // 2, 1) * sin_b)


def _rope_tables_moba(cos_ref, sin_ref):
    c = cos_ref[...]
    s = sin_ref[...]
    lane = lax.broadcasted_iota(jnp.int32, c.shape, 1)
    lo = lane < MOBA_HEAD_DIM // 2
    half = MOBA_HEAD_DIM // 2
    return jnp.where(lo, c, pltpu.roll(c, half, 1)), jnp.where(lo, -s, pltpu.roll(s, half, 1))


def _rope_moba(t, cos, sin_signed):
    return t * cos + pltpu.roll(t, MOBA_HEAD_DIM // 2, 1) * sin_signed


def _mla_prep_kernel(h_ref, cos_ref, sin_ref, gq_ref, gkv_ref, wqn_ref, wqr_ref, wkn_ref, wv_ref,
                     q_ref, k_ref, v_ref):
    scale = (QK_NOPE + QK_ROPE) ** -0.5
    cos, sin_a, sin_b = _rope_tables_mla(cos_ref, sin_ref)
    hq = h_ref[:, 0:Q_LORA].astype(F32)
    hkv = h_ref[:, Q_LORA:Q_LORA + KV_LORA].astype(F32)
    hkr = h_ref[:, Q_LORA + KV_LORA:Q_LORA + KV_LORA + LANES].astype(F32)

    qn = _rms(hq, gq_ref[...]).astype(BF16)
    q_nope = jnp.dot(qn, wqn_ref[...], preferred_element_type=F32)
    q_rope = jnp.dot(qn, wqr_ref[...], preferred_element_type=F32)
    kvn = _rms(hkv, gkv_ref[...]).astype(BF16)
    k_nope = jnp.dot(kvn, wkn_ref[...], preferred_element_type=F32)
    v_ref[...] = jnp.dot(kvn, wv_ref[...], preferred_element_type=F32).astype(BF16)
    kpe = _rope_mla(hkr, cos, sin_a, sin_b).astype(BF16)
    for h in range(MLA_HEADS):
        lo = h * HEAD_SLOT
        hs = slice(h * LANES, (h + 1) * LANES)
        q_ref[:, lo:lo + LANES] = (q_nope[:, hs] * scale).astype(BF16)
        q_ref[:, lo + LANES:lo + HEAD_SLOT] = (
            _rope_mla(q_rope[:, hs], cos, sin_a, sin_b) * scale).astype(BF16)
        k_ref[:, lo:lo + LANES] = k_nope[:, hs].astype(BF16)
        k_ref[:, lo + LANES:lo + HEAD_SLOT] = kpe


def _mla_prep(h, cos, sin, gq, gkv, wqn, wqr, wkn, wv, *, tm=512):
    s = h.shape[0]
    full = lambda a: pl.BlockSpec(a.shape, lambda i: (0,) * a.ndim)
    return pl.pallas_call(
        _mla_prep_kernel,
        out_shape=(jax.ShapeDtypeStruct((s, MLA_HEADS * HEAD_SLOT), BF16),
                   jax.ShapeDtypeStruct((s, MLA_HEADS * HEAD_SLOT), BF16),
                   jax.ShapeDtypeStruct((s, MLA_WIDTH), BF16)),
        grid=(s // tm,),
        in_specs=[pl.BlockSpec((tm, 1024), lambda i: (i, 0)),
                  pl.BlockSpec((tm, LANES), lambda i: (i, 0)),
                  pl.BlockSpec((tm, LANES), lambda i: (i, 0)),
                  full(gq), full(gkv), full(wqn), full(wqr), full(wkn), full(wv)],
        out_specs=(pl.BlockSpec((tm, MLA_HEADS * HEAD_SLOT), lambda i: (i, 0)),
                   pl.BlockSpec((tm, MLA_HEADS * HEAD_SLOT), lambda i: (i, 0)),
                   pl.BlockSpec((tm, MLA_WIDTH), lambda i: (i, 0))),
        compiler_params=_params("parallel"),
        name="mla_prep",
    )(h, cos, sin, gq, gkv, wqn, wqr, wkn, wv)


def _moba_kprep_kernel(h_ref, cos_ref, sin_ref, k_ref, kbar_ref, *, rows):
    cos, sin_signed = _rope_tables_moba(cos_ref, sin_ref)
    row = pl.program_id(0) * rows + lax.broadcasted_iota(jnp.int32, (rows, LANES), 0)
    lane = lax.broadcasted_iota(jnp.int32, (rows, LANES), 1)
    onehot = (lane == row // MOBA_BLOCK).astype(BF16)
    for h in range(MOBA_HEADS):
        lo = h * HEAD_SLOT
        kr = _rope_moba(h_ref[:, h * LANES:(h + 1) * LANES].astype(F32), cos, sin_signed)
        k_ref[:, lo:lo + LANES] = kr.astype(BF16)
        k_ref[:, lo + LANES:lo + HEAD_SLOT] = onehot
        for b in range(rows // MOBA_BLOCK):
            kbar_ref[b:b + 1, h * LANES:(h + 1) * LANES] = jnp.mean(
                kr[b * MOBA_BLOCK:(b + 1) * MOBA_BLOCK], axis=0, keepdims=True)


def _moba_kprep(h, cos, sin, *, rows=2048):
    s = h.shape[0]
    return pl.pallas_call(
        functools.partial(_moba_kprep_kernel, rows=rows),
        out_shape=(jax.ShapeDtypeStruct((s, MOBA_HEADS * HEAD_SLOT), BF16),
                   jax.ShapeDtypeStruct((s // MOBA_BLOCK, MOBA_WIDTH), F32)),
        grid=(s // rows,),
        in_specs=[pl.BlockSpec((rows, MOBA_WIDTH), lambda i: (i, 2)),
                  pl.BlockSpec((rows, LANES), lambda i: (i, 0)),
                  pl.BlockSpec((rows, LANES), lambda i: (i, 0))],
        out_specs=(pl.BlockSpec((rows, MOBA_HEADS * HEAD_SLOT), lambda i: (i, 0)),
                   pl.BlockSpec((rows // MOBA_BLOCK, MOBA_WIDTH), lambda i: (i, 0))),
        compiler_params=_params("parallel"),
        name="moba_kprep",
    )(h, cos, sin)


def _top_mask(vals, row_idx, k):
    n = vals.shape[0]
    chosen = jnp.zeros(vals.shape, jnp.bool_)
    for _ in range(k):
        top = jnp.max(vals, axis=0, keepdims=True)
        first = jnp.min(jnp.where(vals == top, row_idx, n), axis=0, keepdims=True)
        pick = row_idx == first
        chosen = chosen | pick
        vals = jnp.where(pick, -jnp.inf, vals)
    return chosen


def _moba_qprep_kernel(h_ref, cos_ref, sin_ref, kbar_ref, q_ref, *, tm):
    scale = MOBA_HEAD_DIM ** -0.5
    cos, sin_signed = _rope_tables_moba(cos_ref, sin_ref)
    nb = N_MOBA_BLOCKS
    tok = pl.program_id(0) * tm + lax.broadcasted_iota(jnp.int32, (1, tm), 1)
    qblk = tok // MOBA_BLOCK
    blk = lax.broadcasted_iota(jnp.int32, (nb, 1), 0)
    past = blk < qblk
    own = blk == qblk
    for h in range(MOBA_HEADS):
        lo = h * HEAD_SLOT
        qr = _rope_moba(h_ref[:, h * LANES:(h + 1) * LANES].astype(F32), cos, sin_signed)
        gate = _nt_dot(kbar_ref[:, h * LANES:(h + 1) * LANES], qr,
                       precision=lax.Precision.HIGHEST)
        gate = jnp.where(past, gate, -jnp.inf)
        sel = past & _top_mask(gate, blk, MOBA_TOPK)
        bias = jnp.where(sel | own, 0.0, MASK_BIAS)
        bias = jnp.concatenate([bias, jnp.zeros((LANES - nb, tm), F32)], axis=0)
        q_ref[:, lo:lo + LANES] = (qr * scale).astype(BF16)
        q_ref[:, lo + LANES:lo + HEAD_SLOT] = bias.T.astype(BF16)


def _moba_qprep(h, cos, sin, kbar, *, tm=512):
    s = h.shape[0]
    return pl.pallas_call(
        functools.partial(_moba_qprep_kernel, tm=tm),
        out_shape=jax.ShapeDtypeStruct((s, MOBA_HEADS * HEAD_SLOT), BF16),
        grid=(s // tm,),
        in_specs=[pl.BlockSpec((tm, MOBA_WIDTH), lambda i: (i, 1)),
                  pl.BlockSpec((tm, LANES), lambda i: (i, 0)),
                  pl.BlockSpec((tm, LANES), lambda i: (i, 0)),
                  pl.BlockSpec(kbar.shape, lambda i: (0, 0))],
        out_specs=pl.BlockSpec((tm, MOBA_HEADS * HEAD_SLOT), lambda i: (i, 0)),
        compiler_params=_params("parallel"),
        name="moba_qprep",
    )(h, cos, sin, kbar)


def _flash_kernel(qa_ref, qb_ref, k_ref, v_ref, o_ref, *scratch, tq, nq, hp):
    p_idx = pl.program_id(1)
    nc = tq // LANES
    ones = jnp.ones((tq, LANES), BF16)
    heads = [scratch[3 * h:3 * h + 3] for h in range(hp)]
    for h, (q2_ref, m2_ref, acc2_ref) in enumerate(heads):
        q2_ref[0] = qa_ref[:, h * HEAD_SLOT:(h + 1) * HEAD_SLOT]
        q2_ref[1] = qb_ref[:, h * HEAD_SLOT:(h + 1) * HEAD_SLOT]
        m2_ref[...] = jnp.full(m2_ref.shape, -jnp.inf, F32)
        acc2_ref[...] = jnp.zeros(acc2_ref.shape, F32)

    def kv_rows(j):
        return pl.ds(pl.multiple_of(j * tq, tq), tq)

    def scores(h, sel, j):
        return _nt_dot(heads[h][0][sel], k_ref[kv_rows(j), h * HEAD_SLOT:(h + 1) * HEAD_SLOT])

    def consume(h, sel, j, s):
        _, m2_ref, acc2_ref = heads[h]
        v = v_ref[kv_rows(j), h * LANES:(h + 1) * LANES]
        chunks = [s[:, c * LANES:(c + 1) * LANES] for c in range(nc)]
        part = functools.reduce(jnp.maximum, chunks)
        m_old = m2_ref[sel]
        m_new = jnp.maximum(m_old, jnp.max(part, axis=1, keepdims=True))
        alpha = jnp.exp(m_old - m_new)
        p = jnp.concatenate([jnp.exp(ch - m_new).astype(BF16) for ch in chunks], axis=1)
        pv = jnp.dot(p, jnp.concatenate([v, ones], axis=1), preferred_element_type=F32)
        acc2_ref[sel] = jnp.concatenate([alpha, alpha], axis=1) * acc2_ref[sel] + pv
        m2_ref[sel] = m_new

    def diagonal(s):
        r = lax.broadcasted_iota(jnp.int32, s.shape, 0)
        c = lax.broadcasted_iota(jnp.int32, s.shape, 1)
        return jnp.where(c <= r, s, -jnp.inf)

    steps = []
    for t in range(nq - 1):
        sel = (t >= p_idx).astype(jnp.int32)
        steps += [(h, sel, t - sel * p_idx, False) for h in range(hp)]
    steps += [(h, 0, p_idx, True) for h in range(hp)]
    steps += [(h, 1, nq - 1 - p_idx, True) for h in range(hp)]

    ahead = hp
    pending = [scores(*st[:3]) for st in steps[:ahead]]
    for t, (h, sel, j, diag) in enumerate(steps):
        s = pending.pop(0)
        if t + ahead < len(steps):
            pending.append(scores(*steps[t + ahead][:3]))
        consume(h, sel, j, diagonal(s) if diag else s)
    for h, (_, _, acc2_ref) in enumerate(heads):
        for sel in range(2):
            o_ref[sel, :, h * LANES:(h + 1) * LANES] = (
                acc2_ref[sel, :, :LANES] / acc2_ref[sel, :, LANES:]).astype(o_ref.dtype)


def _flash(q, k, v, v_col0, n_heads, *, tq=512, hp=FLASH_HEADS_PER_STEP):
    s = q.shape[0]
    nq = s // tq
    assert n_heads % hp == 0 and v_col0 % hp == 0
    per_head = [pltpu.VMEM((2, tq, HEAD_SLOT), BF16), pltpu.VMEM((2, tq, LANES), F32),
                pltpu.VMEM((2, tq, 2 * LANES), F32)]
    return pl.pallas_call(
        functools.partial(_flash_kernel, tq=tq, nq=nq, hp=hp),
        out_shape=jax.ShapeDtypeStruct((2, s // 2, n_heads * LANES), BF16),
        grid=(n_heads // hp, nq // 2),
        in_specs=[pl.BlockSpec((tq, hp * HEAD_SLOT), lambda h, p: (p, h)),
                  pl.BlockSpec((tq, hp * HEAD_SLOT), lambda h, p: (nq - 1 - p, h)),
                  pl.BlockSpec((s, hp * HEAD_SLOT), lambda h, p: (0, h)),
                  pl.BlockSpec((s, hp * LANES), lambda h, p: (0, v_col0 // hp + h))],
        out_specs=pl.BlockSpec((2, tq, hp * LANES), lambda h, p: (0, p, h)),
        scratch_shapes=per_head * hp,
        compiler_params=_params("parallel", "parallel"),
        name="flash",
    )(q, q, k, v)


def _flash_row_block(i, tm, s, tq=512):
    per_tile = tq // tm
    tile = i // per_tile
    sub = i % per_tile
    nq = s // tq
    hi = (tile >= nq // 2).astype(jnp.int32)
    return hi, jnp.where(hi == 1, nq - 1 - tile, tile) * per_tile + sub


def _outproj_kernel(om_ref, ob_ref, gm_ref, gb_ref, wo_ref, x_ref, lg_ref, lb_ref, wrt_ref, br_ref,
                    x1_ref, idx_ref, pos_ref, gate_ref, cnt_ref):
    @pl.when(pl.program_id(0) == 0)
    def _():
        cnt_ref[...] = jnp.zeros(cnt_ref.shape, jnp.int32)

    a = jnp.concatenate([_rms(om_ref[...].astype(F32), gm_ref[...]).astype(BF16),
                         _rms(ob_ref[...].astype(F32), gb_ref[...]).astype(BF16)], axis=1)
    mixed = jnp.dot(a, wo_ref[...], preferred_element_type=F32)
    x1 = _layer_norm(DEEPNORM_ALPHA * x_ref[...] + mixed, lg_ref[...], lb_ref[...])
    x1_ref[...] = x1

    logits = _nt_dot(wrt_ref[...], x1, precision=lax.Precision.HIGHEST) + br_ref[...]
    tm = logits.shape[1]
    eidx = lax.broadcasted_iota(jnp.int32, (N_EXPERTS, 1), 0)
    rank = _topk_rank(logits, eidx)
    sel = rank < TOP_K
    mx = jnp.max(logits, axis=0, keepdims=True)
    p = jnp.where(sel, jnp.exp(logits - mx), 0.0)
    gates = p / jnp.sum(p, axis=0, keepdims=True)

    before = (lax.broadcasted_iota(jnp.int32, (tm, tm), 0)
              < lax.broadcasted_iota(jnp.int32, (tm, tm), 1)).astype(BF16)
    prefix = jnp.dot(sel.astype(BF16), before, preferred_element_type=F32).astype(jnp.int32)
    pos = cnt_ref[:, 0:1] + prefix
    cnt_ref[...] = cnt_ref[...] + jnp.sum(sel.astype(jnp.int32), axis=1, keepdims=True)

    pick = lambda vals, k, zero: jnp.sum(jnp.where(rank == k, vals, zero), axis=0, keepdims=True)
    pad_i = [jnp.zeros((8 - TOP_K, tm), jnp.int32)]
    idx_ref[...] = jnp.concatenate([pick(eidx, k, 0) for k in range(TOP_K)] + pad_i, axis=0)
    pos_ref[...] = jnp.concatenate([pick(pos, k, 0) for k in range(TOP_K)] + pad_i, axis=0)
    gate_rows = jnp.concatenate([pick(gates, k, 0.0) for k in range(TOP_K)]
                                + [jnp.zeros((LANES - TOP_K, tm), F32)], axis=0)
    gate_ref[...] = gate_rows.T


def _outproj(o_mla, o_moba, g_mla, g_moba, wo, x, ln_g, ln_b, wr_t, b_r, *, tm=512):
    s = x.shape[0]
    full = lambda a: pl.BlockSpec(a.shape, lambda i: (0,) * a.ndim)
    return pl.pallas_call(
        _outproj_kernel,
        out_shape=(jax.ShapeDtypeStruct((s, D_MODEL), F32),
                   jax.ShapeDtypeStruct((8, s), jnp.int32),
                   jax.ShapeDtypeStruct((8, s), jnp.int32),
                   jax.ShapeDtypeStruct((s, LANES), F32),
                   jax.ShapeDtypeStruct((N_EXPERTS, LANES), jnp.int32)),
        grid=(s // tm,),
        in_specs=[pl.BlockSpec((None, tm, MLA_WIDTH), lambda i: (*_flash_row_block(i, tm, s), 0)),
                  pl.BlockSpec((None, tm, MOBA_WIDTH), lambda i: (*_flash_row_block(i, tm, s), 0)),
                  full(g_mla), full(g_moba), full(wo),
                  pl.BlockSpec((tm, D_MODEL), lambda i: (i, 0)),
                  full(ln_g), full(ln_b), full(wr_t), full(b_r)],
        out_specs=(pl.BlockSpec((tm, D_MODEL), lambda i: (i, 0)),
                   pl.BlockSpec((8, tm), lambda i: (0, i)),
                   pl.BlockSpec((8, tm), lambda i: (0, i)),
                   pl.BlockSpec((tm, LANES), lambda i: (i, 0)),
                   pl.BlockSpec((N_EXPERTS, LANES), lambda i: (0, 0))),
        compiler_params=_params("arbitrary"),
        name="outproj_router",
    )(o_mla, o_moba, g_mla, g_moba, wo, x, ln_g, ln_b, wr_t, b_r)


def _ffn_kernel(ie_ref, nb_ref, tab_hbm, x1_hbm, wg_ref, wu_ref, bg_ref, bu_ref, wd_ref, bd_ref,
                y_hbm, tab_smem, xf_ref, xb_ref, y_ref, sem_tab, sem_in, sem_out,
                *, nf, n_tok, max_blocks):
    w = pl.program_id(0)
    s = pl.program_id(1)
    n_work = pl.num_programs(0)
    nb = nb_ref[w]
    nb_prev = nb_ref[jnp.maximum(w - 1, 0)]
    w_next = jnp.minimum(w + 1, n_work - 1)
    has_next = (w + 1 < n_work) & (nb_ref[w_next] > 0)
    sub = SUBLANES
    tiles_per_block = MOE_BLOCK // sub
    slot = lax.bitwise_and(w, 1)

    tab_len = tab_smem.shape[0] // 2

    def load_table(item, slot_):
        dst = tab_smem.at[pl.ds(pl.multiple_of(slot_ * tab_len, tab_len), tab_len)]
        cp = pltpu.make_async_copy(tab_hbm.at[item], dst, sem_tab)
        cp.start()
        cp.wait()

    def for_rows(n_blocks, slot_, fn):
        def body(b, carry):
            base = slot_ * tab_len + b * MOE_BLOCK
            for i in range(MOE_BLOCK):
                fn(b * tiles_per_block + i // sub, i % sub, tab_smem[base + i])
            return carry
        lax.fori_loop(0, n_blocks, body, 0)

    def wait_tiles(n_blocks, make_copy):
        def body(g, carry):
            make_copy().wait()
            return carry
        lax.fori_loop(0, n_blocks * tiles_per_block, body, 0)

    def hbm_row(ref, r):
        return ref.at[lax.shift_right_logical(r, 3), pl.ds(lax.bitwise_and(r, sub - 1), 1)]

    def start_gather(n_blocks, slot_):
        for_rows(n_blocks, slot_, lambda g, j, e: pltpu.make_async_copy(
            hbm_row(x1_hbm, lax.bitwise_and(e, n_tok - 1)), xf_ref.at[g, pl.ds(j, 1)],
            sem_in).start())

    def wait_scatter(n_blocks):
        wait_tiles(n_blocks, lambda: pltpu.make_async_copy(y_ref.at[0], y_hbm.at[0], sem_out))

    @pl.when((nb > 0) & (s == 0))
    def _():
        @pl.when(w == 0)
        def _():
            load_table(0, 0)
            start_gather(nb, 0)
            y_ref[...] = jnp.zeros(y_ref.shape, F32)
            dump = pltpu.make_async_copy(
                y_ref, y_hbm.at[pl.ds(TOP_K * n_tok // sub, MOE_ROWS // sub)], sem_out)
            dump.start()
            dump.wait()

        wait_tiles(nb, lambda: pltpu.make_async_copy(x1_hbm.at[0], xf_ref.at[0], sem_in))

        def cast(b, carry):
            tiles = pl.ds(b * tiles_per_block, tiles_per_block)
            rows = pl.ds(pl.multiple_of(b * MOE_BLOCK, MOE_BLOCK), MOE_BLOCK)
            xb_ref[rows, :] = xf_ref[tiles].reshape(MOE_BLOCK, D_MODEL).astype(BF16)
            return carry

        lax.fori_loop(0, nb, cast, 0)

        @pl.when(w > 0)
        def _():
            wait_scatter(nb_prev)

        bias = jnp.broadcast_to(bd_ref[...], (tiles_per_block, sub, D_MODEL))

        def init(b, carry):
            y_ref[pl.ds(b * tiles_per_block, tiles_per_block)] = bias
            return carry

        lax.fori_loop(0, nb, init, 0)

    def step(m):
        x = xb_ref[0:m, :]
        g = jnp.dot(x, wg_ref[...].astype(BF16), preferred_element_type=F32) + bg_ref[...]
        u = jnp.dot(x, wu_ref[...].astype(BF16), preferred_element_type=F32) + bu_ref[...]
        g = jnp.minimum(g, SWIGLU_LIMIT)
        u = jnp.clip(u, -SWIGLU_LIMIT, SWIGLU_LIMIT)
        act = ((u + 1.0) * (g * (1.0 / (1.0 + jnp.exp(-SWIGLU_ALPHA * g))))).astype(BF16)
        down = jnp.dot(act, wd_ref[...].astype(BF16), preferred_element_type=F32)
        y_ref[0:m // sub] = y_ref[0:m // sub] + down.reshape(m // sub, sub, D_MODEL)

    for k in range(1, max_blocks + 1):
        pl.when(nb == k)(functools.partial(step, k * MOE_BLOCK))

    @pl.when((nb > 0) & (s == nf - 1))
    def _():
        for_rows(nb, slot, lambda g, j, e: pltpu.make_async_copy(
            y_ref.at[g, pl.ds(j, 1)], hbm_row(y_hbm, e), sem_out).start())

        @pl.when(has_next)
        def _():
            load_table(w_next, 1 - slot)
            start_gather(nb_ref[w_next], 1 - slot)

        @pl.when(jnp.logical_not(has_next))
        def _():
            wait_scatter(nb)


def _ffn(item_e, item_nb, tab, x1, w_gate_up, b_gate_up, w_down, b_down, *, tf=256):
    n_work, tab_len = tab.shape
    n_tok = x1.shape[0]
    nf = D_FF // tf
    n_out = TOP_K * n_tok + MOE_ROWS
    assert n_tok & (n_tok - 1) == 0 and MOE_ROWS <= n_tok and MOE_ROWS % MOE_BLOCK == 0
    assert n_tok % SUBLANES == 0 and n_out % SUBLANES == 0 and tab_len >= MOE_ROWS

    def tile(w, s, nbr):
        return jnp.where(nbr[w] > 0, s, nf - 1)

    def colmap(off):
        return lambda w, s, ie, nbr: (ie[w], 0, off + tile(w, s, nbr))

    def rowmap(w, s, ie, nbr):
        return ie[w], tile(w, s, nbr), 0

    grid_spec = pltpu.PrefetchScalarGridSpec(
        num_scalar_prefetch=2,
        grid=(n_work, nf),
        in_specs=[
            pl.BlockSpec(memory_space=pl.ANY),
            pl.BlockSpec(memory_space=pl.ANY),
            pl.BlockSpec((None, D_MODEL, tf), colmap(0)),
            pl.BlockSpec((None, D_MODEL, tf), colmap(nf)),
            pl.BlockSpec((None, 1, tf), colmap(0)),
            pl.BlockSpec((None, 1, tf), colmap(nf)),
            pl.BlockSpec((None, tf, D_MODEL), rowmap),
            pl.BlockSpec((None, 1, D_MODEL), lambda w, s, ie, nbr: (ie[w], 0, 0)),
        ],
        out_specs=pl.BlockSpec(memory_space=pl.ANY),
        scratch_shapes=[pltpu.SMEM((2 * tab_len,), jnp.int32),
                        pltpu.VMEM((MOE_ROWS // SUBLANES, SUBLANES, D_MODEL), F32),
                        pltpu.VMEM((MOE_ROWS, D_MODEL), BF16),
                        pltpu.VMEM((MOE_ROWS // SUBLANES, SUBLANES, D_MODEL), F32),
                        pltpu.SemaphoreType.DMA, pltpu.SemaphoreType.DMA, pltpu.SemaphoreType.DMA],
    )
    y = pl.pallas_call(
        functools.partial(_ffn_kernel, nf=nf, n_tok=n_tok, max_blocks=MOE_ROWS // MOE_BLOCK),
        out_shape=jax.ShapeDtypeStruct((n_out // SUBLANES, SUBLANES, D_MODEL), F32),
        grid_spec=grid_spec,
        compiler_params=_params("arbitrary", "arbitrary"),
        name="experts",
    )(item_e, item_nb, tab, x1.reshape(n_tok // SUBLANES, SUBLANES, D_MODEL), w_gate_up, w_gate_up,
      b_gate_up, b_gate_up, w_down, b_down)
    return y.reshape(n_out, D_MODEL)


def _final_kernel(*refs):
    y_refs = refs[:TOP_K]
    x1_ref, gate_ref, g_ref, b_ref, o_ref = refs[TOP_K:]
    ffn = gate_ref[:, 0:1] * y_refs[0][...]
    for k in range(1, TOP_K):
        ffn = ffn + gate_ref[:, k:k + 1] * y_refs[k][...]
    o_ref[...] = _layer_norm(DEEPNORM_ALPHA * x1_ref[...] + ffn, g_ref[...], b_ref[...])


def _final(y, x1, gates, g, b, *, tm=256):
    s = x1.shape[0]
    plane = lambda k: pl.BlockSpec((tm, D_MODEL), lambda i: (k * (s // tm) + i, 0))
    return pl.pallas_call(
        _final_kernel,
        out_shape=jax.ShapeDtypeStruct((s, D_MODEL), F32),
        grid=(s // tm,),
        in_specs=[plane(k) for k in range(TOP_K)] + [
            pl.BlockSpec((tm, D_MODEL), lambda i: (i, 0)),
            pl.BlockSpec((tm, LANES), lambda i: (i, 0)),
            pl.BlockSpec((1, D_MODEL), lambda i: (0, 0)),
            pl.BlockSpec((1, D_MODEL), lambda i: (0, 0))],
        out_specs=pl.BlockSpec((tm, D_MODEL), lambda i: (i, 0)),
        compiler_params=_params("parallel"),
        name="combine_final_ln",
    )(*([y] * TOP_K), x1, gates, g, b)


def _layer(x, pos, w_in, g_q_a, w_q_b, g_kv_a, w_kv_b, g_out_mla, g_out_moba, w_o, ln1_g, ln1_b,
           w_router, b_router, w_gate_up, b_gate_up, w_down, b_down, ln2_g, ln2_b):
    s = x.shape[0]
    mla_cols = Q_LORA + KV_LORA + QK_ROPE
    w_in_p = jnp.concatenate(
        [w_in[:, :mla_cols], jnp.zeros((D_MODEL, 1024 - mla_cols), w_in.dtype), w_in[:, mla_cols:]],
        axis=1).astype(BF16)
    wq = w_q_b.reshape(Q_LORA, MLA_HEADS, QK_NOPE + QK_ROPE)
    wqn = wq[:, :, :QK_NOPE].reshape(Q_LORA, MLA_HEADS * LANES).astype(BF16)
    wqr = jnp.pad(wq[:, :, QK_NOPE:], ((0, 0), (0, 0), (0, LANES - QK_ROPE))).reshape(
        Q_LORA, MLA_HEADS * LANES).astype(BF16)
    wkv = w_kv_b.reshape(KV_LORA, MLA_HEADS, QK_NOPE + V_HEAD)
    wkn = wkv[:, :, :QK_NOPE].reshape(KV_LORA, MLA_HEADS * LANES).astype(BF16)
    wv = wkv[:, :, QK_NOPE:].reshape(KV_LORA, MLA_WIDTH).astype(BF16)
    row = lambda a: a.reshape(1, -1)

    cos, sin = _rope_angles(pos.reshape(s, 1))
    h = _inproj(x, w_in_p)
    q_mla, k_mla, v_mla = _mla_prep(h, cos, sin, row(g_q_a), row(g_kv_a), wqn, wqr, wkn, wv)
    k_moba, kbar = _moba_kprep(h, cos, sin)
    q_moba = _moba_qprep(h, cos, sin, kbar)
    o_mla = _flash(q_mla, k_mla, v_mla, 0, MLA_HEADS)
    o_moba = _flash(q_moba, k_moba, h, 3 * MOBA_WIDTH // LANES, MOBA_HEADS)

    x1, idx_t, pos_t, gates, cnt = _outproj(o_mla, o_moba, row(g_out_mla), row(g_out_moba),
                                            w_o.astype(BF16), x, row(ln1_g), row(ln1_b),
                                            w_router.T, b_router.reshape(N_EXPERTS, 1))

    n_work = -(-s * TOP_K // MOE_ROWS) + N_EXPERTS
    counts = cnt[:, 0]
    items_per_e = (counts + MOE_ROWS - 1) // MOE_ROWS
    item_end = jnp.cumsum(items_per_e)
    item_start = item_end - items_per_e
    w_idx = jnp.arange(n_work)
    item_e = jnp.minimum(jnp.sum(w_idx[:, None] >= item_end[None, :], axis=1), N_EXPERTS - 1)
    item_rows = jnp.clip(counts[item_e] - (w_idx - item_start[item_e]) * MOE_ROWS, 0, MOE_ROWS)
    item_nb = jnp.where(w_idx < item_end[-1], (item_rows + MOE_BLOCK - 1) // MOE_BLOCK, 0)
    e_k = idx_t[:TOP_K]
    p_k = pos_t[:TOP_K]
    start_k = jnp.sum(jnp.where(e_k[..., None] == jnp.arange(N_EXPERTS), item_start, 0), axis=-1)
    item_row = ((start_k + p_k // MOE_ROWS) * MOE_ROWS + p_k % MOE_ROWS).astype(jnp.int32)
    out_row = (jnp.arange(TOP_K, dtype=jnp.int32)[:, None] * s
               + jnp.arange(s, dtype=jnp.int32)[None, :])
    dump = TOP_K * s + jnp.arange(n_work * MOE_ROWS, dtype=jnp.int32) % MOE_ROWS
    tab = dump.at[item_row.reshape(-1)].set(out_row.reshape(-1))
    tab = jnp.pad(tab.reshape(n_work, MOE_ROWS), ((0, 0), (0, MOE_TAB_LEN - MOE_ROWS)))

    y = _ffn(item_e.astype(jnp.int32), item_nb.astype(jnp.int32), tab, x1, w_gate_up,
             b_gate_up.reshape(N_EXPERTS, 1, 2 * D_FF), w_down, b_down.reshape(N_EXPERTS, 1, D_MODEL))
    return _final(y, x1, gates, row(ln2_g), row(ln2_b))


def kernel(x, positions, w_in, g_q_a, w_q_b, g_kv_a, w_kv_b, g_out_mla, g_out_moba, w_o, ln1_g, ln1_b,
           w_router, b_router, w_gate_up, b_gate_up, w_down, b_down, ln2_g, ln2_b):
    b, s, d = x.shape
    assert b == 1 and s == SEQ and d == D_MODEL and w_in.shape[0] == DEPTH
    hcur = x[0]
    for l in range(DEPTH):
        hcur = _layer(hcur, positions[0], w_in[l], g_q_a[l], w_q_b[l], g_kv_a[l], w_kv_b[l],
                      g_out_mla[l], g_out_moba[l], w_o[l], ln1_g[l], ln1_b[l], w_router[l],
                      b_router[l], w_gate_up[l], b_gate_up[l], w_down[l], b_down[l], ln2_g[l],
                      ln2_b[l])
    return hcur[None]
```

```python
import functools

import numpy as np
import jax
import jax.numpy as jnp
from jax import lax
from jax.experimental import pallas as pl
from jax.experimental.pallas import tpu as pltpu

D_MODEL = 2048
SEQ = 8192
MLA_HEADS = 8
QK_NOPE = 128
QK_ROPE = 64
V_HEAD = 128
Q_LORA = 512
KV_LORA = 256
MLA_WIDTH = MLA_HEADS * V_HEAD
MOBA_HEADS = 8
MOBA_HEAD_DIM = 128
MOBA_WIDTH = MOBA_HEADS * MOBA_HEAD_DIM
MOBA_BLOCK = 256
MOBA_TOPK = 3
N_MOBA_BLOCKS = SEQ // MOBA_BLOCK
ROPE_THETA = 10000.0
N_EXPERTS = 32
TOP_K = 4
D_FF = D_MODEL
SWIGLU_LIMIT = 7.0
SWIGLU_ALPHA = 1.702
MOE_BLOCK = 256
MOE_ROWS = 1536
DEPTH = 1
DEEPNORM_ALPHA = float((2 * DEPTH) ** 0.25)
RMS_EPS = 1e-6
LN_EPS = 1e-5

LANES = 128
SUBLANES = 8
MOE_TAB_LEN = 2048
FLASH_HEADS_PER_STEP = 2
HEAD_SLOT = 2 * LANES
H_GROUP = 1024
MASK_BIAS = -float(2 ** 17)
V7X_VMEM_LIMIT = 56 * 1024 * 1024

F32 = jnp.float32
BF16 = jnp.bfloat16


def _params(*semantics):
    return pltpu.CompilerParams(dimension_semantics=semantics, vmem_limit_bytes=V7X_VMEM_LIMIT)


def _rms(xf, g):
    return xf * lax.rsqrt(jnp.mean(xf * xf, axis=-1, keepdims=True) + RMS_EPS) * g


def _layer_norm(xf, g, b):
    mu = jnp.mean(xf, axis=-1, keepdims=True)
    xc = xf - mu
    var = jnp.mean(xc * xc, axis=-1, keepdims=True)
    return xc * lax.rsqrt(var + LN_EPS) * g + b


def _nt_dot(a, b, **kw):
    return lax.dot_general(a, b, (((1,), (1,)), ((), ())), preferred_element_type=F32, **kw)


def _topk_rank(vals, row_idx):
    n = vals.shape[0]
    rank = jnp.zeros(vals.shape, jnp.int32)
    for jp in range(n):
        vj = vals[jp:jp + 1, :]
        ahead = (vj > vals) | ((vj == vals) & (jp < row_idx))
        rank = rank + ahead.astype(jnp.int32)
    return rank


def _inproj_kernel(x_ref, wt_ref, o_ref, xb_ref, *, mla_cols):
    j = pl.program_id(1)

    @pl.when(j == 0)
    def _():
        xb_ref[...] = x_ref[...].astype(BF16)

    h = _nt_dot(xb_ref[...], wt_ref[...].astype(BF16))
    col = lax.broadcasted_iota(jnp.int32, h.shape, 1)
    o_ref[...] = jnp.where((j > 0) | (col < mla_cols), h, 0.0).astype(o_ref.dtype)


def _inproj(x, w_t, *, tm=1024):
    m, k = x.shape
    mla_cols = Q_LORA + KV_LORA + QK_ROPE
    n_groups = 1 + 3 * MOBA_WIDTH // H_GROUP
    assert w_t.shape == (mla_cols + 3 * MOBA_WIDTH, k) and mla_cols <= H_GROUP
    assert mla_cols % SUBLANES == 0 and MOBA_WIDTH % H_GROUP == 0

    def wmap(i, j):
        pad_tiles = (H_GROUP - mla_cols) // SUBLANES
        return (j * (H_GROUP // SUBLANES) - pad_tiles * jnp.minimum(j, 1)) * SUBLANES, 0

    return pl.pallas_call(
        functools.partial(_inproj_kernel, mla_cols=mla_cols),
        out_shape=jax.ShapeDtypeStruct((m, n_groups * H_GROUP), BF16),
        grid=(m // tm, n_groups),
        in_specs=[pl.BlockSpec((tm, k), lambda i, j: (i, 0)),
                  pl.BlockSpec((pl.Element(H_GROUP), pl.Element(k)), wmap)],
        out_specs=pl.BlockSpec((tm, H_GROUP), lambda i, j: (i, j)),
        scratch_shapes=[pltpu.VMEM((tm, k), BF16)],
        compiler_params=_params("parallel", "arbitrary"),
        name="inproj",
    )(x, w_t)


def _rope_angle_kernel(pos_ref, invf_ref, cos_ref, sin_ref):
    ang = pos_ref[...].astype(F32) * invf_ref[...]
    cos_ref[...] = jnp.cos(ang)
    sin_ref[...] = jnp.sin(ang)


def _rope_angles(pos_col, *, tm=1024):
    s = pos_col.shape[0]
    lane = np.arange(LANES)
    moba_half, mla_half = MOBA_HEAD_DIM // 2, QK_ROPE // 2
    f_moba = ROPE_THETA ** (-(lane % moba_half).astype(np.float32) * 2.0 / MOBA_HEAD_DIM)
    f_mla = ROPE_THETA ** (-((lane - moba_half) % mla_half).astype(np.float32) * 2.0 / QK_ROPE)
    invf = np.where(lane < moba_half, f_moba, np.where(lane < moba_half + mla_half, f_mla, 0.0))
    invf = jnp.asarray(invf.astype(np.float32)).reshape(1, LANES)
    return pl.pallas_call(
        _rope_angle_kernel,
        out_shape=(jax.ShapeDtypeStruct((s, LANES), F32), jax.ShapeDtypeStruct((s, LANES), F32)),
        grid=(s // tm,),
        in_specs=[pl.BlockSpec((tm, 1), lambda i: (i, 0)), pl.BlockSpec((1, LANES), lambda i: (0, 0))],
        out_specs=(pl.BlockSpec((tm, LANES), lambda i: (i, 0)),
                   pl.BlockSpec((tm, LANES), lambda i: (i, 0))),
        compiler_params=_params("parallel"),
        name="rope_angles",
    )(pos_col, invf)


def _rope_tables_mla(cos_ref, sin_ref):
    c = cos_ref[...]
    s = sin_ref[...]
    lane = lax.broadcasted_iota(jnp.int32, c.shape, 1)
    half = QK_ROPE // 2
    lo = lane < half
    cos = jnp.where(lo, pltpu.roll(c, LANES // 2, 1), pltpu.roll(c, LANES - half, 1))
    sin_a = jnp.where(lo, -pltpu.roll(s, LANES // 2, 1), 0.0)
    sin_b = jnp.where((lane >= half) & (lane < QK_ROPE), pltpu.roll(s, LANES - half, 1), 0.0)
    return cos, sin_a, sin_b


def _rope_mla(t, cos, sin_a, sin_b):
    return (t * cos + pltpu.roll(t, LANES - QK_ROPE // 2, 1) * sin_a
            + pltpu.roll(t, QK_ROPE // 2, 1) * sin_b)


def _rope_tables_moba(cos_ref, sin_ref):
    c = cos_ref[...]
    s = sin_ref[...]
    lane = lax.broadcasted_iota(jnp.int32, c.shape, 1)
    lo = lane < MOBA_HEAD_DIM // 2
    half = MOBA_HEAD_DIM // 2
    return jnp.where(lo, c, pltpu.roll(c, half, 1)), jnp.where(lo, -s, pltpu.roll(s, half, 1))


def _rope_moba(t, cos, sin_signed):
    return t * cos + pltpu.roll(t, MOBA_HEAD_DIM // 2, 1) * sin_signed


def _mla_prep_kernel(h_ref, cos_ref, sin_ref, gq_ref, gkv_ref, wqn_ref, wqr_ref, wkn_ref, wv_ref,
                     q_ref, k_ref, v_ref):
    scale = (QK_NOPE + QK_ROPE) ** -0.5
    cos, sin_a, sin_b = _rope_tables_mla(cos_ref, sin_ref)
    hq = h_ref[:, 0:Q_LORA].astype(F32)
    hkv = h_ref[:, Q_LORA:Q_LORA + KV_LORA].astype(F32)
    hkr = h_ref[:, Q_LORA + KV_LORA:Q_LORA + KV_LORA + LANES].astype(F32)

    qn = _rms(hq, gq_ref[...]).astype(BF16)
    q_nope = jnp.dot(qn, wqn_ref[...], preferred_element_type=F32)
    q_rope = jnp.dot(qn, wqr_ref[...], preferred_element_type=F32)
    kvn = _rms(hkv, gkv_ref[...]).astype(BF16)
    k_nope = jnp.dot(kvn, wkn_ref[...], preferred_element_type=F32)
    v_ref[...] = jnp.dot(kvn, wv_ref[...], preferred_element_type=F32).astype(BF16)
    kpe = _rope_mla(hkr, cos, sin_a, sin_b).astype(BF16)
    for h in range(MLA_HEADS):
        lo = h * HEAD_SLOT
        hs = slice(h * LANES, (h + 1) * LANES)
        q_ref[:, lo:lo + LANES] = (q_nope[:, hs] * scale).astype(BF16)
        q_ref[:, lo + LANES:lo + HEAD_SLOT] = (
            _rope_mla(q_rope[:, hs], cos, sin_a, sin_b) * scale).astype(BF16)
        k_ref[:, lo:lo + LANES] = k_nope[:, hs].astype(BF16)
        k_ref[:, lo + LANES:lo + HEAD_SLOT] = kpe


def _mla_prep(h, cos, sin, gq, gkv, wqn, wqr, wkn, wv, *, tm=512):
    s = h.shape[0]
    full = lambda a: pl.BlockSpec(a.shape, lambda i: (0,) * a.ndim)
    return pl.pallas_call(
        _mla_prep_kernel,
        out_shape=(jax.ShapeDtypeStruct((s, MLA_HEADS * HEAD_SLOT), BF16),
                   jax.ShapeDtypeStruct((s, MLA_HEADS * HEAD_SLOT), BF16),
                   jax.ShapeDtypeStruct((s, MLA_WIDTH), BF16)),
        grid=(s // tm,),
        in_specs=[pl.BlockSpec((tm, 1024), lambda i: (i, 0)),
                  pl.BlockSpec((tm, LANES), lambda i: (i, 0)),
                  pl.BlockSpec((tm, LANES), lambda i: (i, 0)),
                  full(gq), full(gkv), full(wqn), full(wqr), full(wkn), full(wv)],
        out_specs=(pl.BlockSpec((tm, MLA_HEADS * HEAD_SLOT), lambda i: (i, 0)),
                   pl.BlockSpec((tm, MLA_HEADS * HEAD_SLOT), lambda i: (i, 0)),
                   pl.BlockSpec((tm, MLA_WIDTH), lambda i: (i, 0))),
        compiler_params=_params("parallel"),
        name="mla_prep",
    )(h, cos, sin, gq, gkv, wqn, wqr, wkn, wv)


def _moba_kprep_kernel(h_ref, cos_ref, sin_ref, k_ref, kbar_ref, *, rows):
    cos, sin_signed = _rope_tables_moba(cos_ref, sin_ref)
    row = pl.program_id(0) * rows + lax.broadcasted_iota(jnp.int32, (rows, LANES), 0)
    lane = lax.broadcasted_iota(jnp.int32, (rows, LANES), 1)
    onehot = (lane == row // MOBA_BLOCK).astype(BF16)
    for h in range(MOBA_HEADS):
        lo = h * HEAD_SLOT
        kr = _rope_moba(h_ref[:, h * LANES:(h + 1) * LANES].astype(F32), cos, sin_signed)
        k_ref[:, lo:lo + LANES] = kr.astype(BF16)
        k_ref[:, lo + LANES:lo + HEAD_SLOT] = onehot
        for b in range(rows // MOBA_BLOCK):
            kbar_ref[b:b + 1, h * LANES:(h + 1) * LANES] = jnp.mean(
                kr[b * MOBA_BLOCK:(b + 1) * MOBA_BLOCK], axis=0, keepdims=True)


def _moba_kprep(h, cos, sin, *, rows=2048):
    s = h.shape[0]
    return pl.pallas_call(
        functools.partial(_moba_kprep_kernel, rows=rows),
        out_shape=(jax.ShapeDtypeStruct((s, MOBA_HEADS * HEAD_SLOT), BF16),
                   jax.ShapeDtypeStruct((s // MOBA_BLOCK, MOBA_WIDTH), F32)),
        grid=(s // rows,),
        in_specs=[pl.BlockSpec((rows, MOBA_WIDTH), lambda i: (i, 2)),
                  pl.BlockSpec((rows, LANES), lambda i: (i, 0)),
                  pl.BlockSpec((rows, LANES), lambda i: (i, 0))],
        out_specs=(pl.BlockSpec((rows, MOBA_HEADS * HEAD_SLOT), lambda i: (i, 0)),
                   pl.BlockSpec((rows // MOBA_BLOCK, MOBA_WIDTH), lambda i: (i, 0))),
        compiler_params=_params("parallel"),
        name="moba_kprep",
    )(h, cos, sin)


def _top_mask(vals, row_idx, k):
    n = vals.shape[0]
    chosen = jnp.zeros(vals.shape, jnp.bool_)
    for _ in range(k):
        top = jnp.max(vals, axis=0, keepdims=True)
        first = jnp.min(jnp.where(vals == top, row_idx, n), axis=0, keepdims=True)
        pick = row_idx == first
        chosen = chosen | pick
        vals = jnp.where(pick, -jnp.inf, vals)
    return chosen


def _moba_qprep_kernel(h_ref, cos_ref, sin_ref, kbar_ref, q_ref, *, tm):
    scale = MOBA_HEAD_DIM ** -0.5
    cos, sin_signed = _rope_tables_moba(cos_ref, sin_ref)
    nb = N_MOBA_BLOCKS
    tok = pl.program_id(0) * tm + lax.broadcasted_iota(jnp.int32, (1, tm), 1)
    qblk = tok // MOBA_BLOCK
    blk = lax.broadcasted_iota(jnp.int32, (nb, 1), 0)
    past = blk < qblk
    own = blk == qblk
    for h in range(MOBA_HEADS):
        lo = h * HEAD_SLOT
        qr = _rope_moba(h_ref[:, h * LANES:(h + 1) * LANES].astype(F32), cos, sin_signed)
        gate = _nt_dot(kbar_ref[:, h * LANES:(h + 1) * LANES], qr,
                       precision=lax.Precision.HIGHEST)
        gate = jnp.where(past, gate, -jnp.inf)
        sel = past & _top_mask(gate, blk, MOBA_TOPK)
        bias = jnp.where(sel | own, 0.0, MASK_BIAS)
        bias = jnp.concatenate([bias, jnp.zeros((LANES - nb, tm), F32)], axis=0)
        q_ref[:, lo:lo + LANES] = (qr * scale).astype(BF16)
        q_ref[:, lo + LANES:lo + HEAD_SLOT] = bias.T.astype(BF16)


def _moba_qprep(h, cos, sin, kbar, *, tm=512):
    s = h.shape[0]
    return pl.pallas_call(
        functools.partial(_moba_qprep_kernel, tm=tm),
        out_shape=jax.ShapeDtypeStruct((s, MOBA_HEADS * HEAD_SLOT), BF16),
        grid=(s // tm,),
        in_specs=[pl.BlockSpec((tm, MOBA_WIDTH), lambda i: (i, 1)),
                  pl.BlockSpec((tm, LANES), lambda i: (i, 0)),
                  pl.BlockSpec((tm, LANES), lambda i: (i, 0)),
                  pl.BlockSpec(kbar.shape, lambda i: (0, 0))],
        out_specs=pl.BlockSpec((tm, MOBA_HEADS * HEAD_SLOT), lambda i: (i, 0)),
        compiler_params=_params("parallel"),
        name="moba_qprep",
    )(h, cos, sin, kbar)


def _flash_kernel(qa_ref, qb_ref, k_ref, v_ref, o_ref, *scratch, tq, nq, hp):
    p_idx = pl.program_id(1)
    nc = tq // LANES
    ones = jnp.ones((tq, LANES), BF16)
    heads = [scratch[3 * h:3 * h + 3] for h in range(hp)]
    for h, (q2_ref, m2_ref, acc2_ref) in enumerate(heads):
        q2_ref[0] = qa_ref[:, h * HEAD_SLOT:(h + 1) * HEAD_SLOT]
        q2_ref[1] = qb_ref[:, h * HEAD_SLOT:(h + 1) * HEAD_SLOT]
        m2_ref[...] = jnp.full(m2_ref.shape, -jnp.inf, F32)
        acc2_ref[...] = jnp.zeros(acc2_ref.shape, F32)

    def kv_rows(j):
        return pl.ds(pl.multiple_of(j * tq, tq), tq)

    def scores(h, sel, j):
        return _nt_dot(heads[h][0][sel], k_ref[kv_rows(j), h * HEAD_SLOT:(h + 1) * HEAD_SLOT])

    def consume(h, sel, j, s):
        _, m2_ref, acc2_ref = heads[h]
        v = v_ref[kv_rows(j), h * LANES:(h + 1) * LANES]
        chunks = [s[:, c * LANES:(c + 1) * LANES] for c in range(nc)]
        part = functools.reduce(jnp.maximum, chunks)
        m_old = m2_ref[sel]
        m_new = jnp.maximum(m_old, jnp.max(part, axis=1, keepdims=True))
        alpha = jnp.exp(m_old - m_new)
        p = jnp.concatenate([jnp.exp(ch - m_new).astype(BF16) for ch in chunks], axis=1)
        pv = jnp.dot(p, jnp.concatenate([v, ones], axis=1), preferred_element_type=F32)
        acc2_ref[sel] = jnp.concatenate([alpha, alpha], axis=1) * acc2_ref[sel] + pv
        m2_ref[sel] = m_new

    def diagonal(s):
        r = lax.broadcasted_iota(jnp.int32, s.shape, 0)
        c = lax.broadcasted_iota(jnp.int32, s.shape, 1)
        return jnp.where(c <= r, s, -jnp.inf)

    steps = []
    for t in range(nq - 1):
        sel = (t >= p_idx).astype(jnp.int32)
        steps += [(h, sel, t - sel * p_idx, False) for h in range(hp)]
    steps += [(h, 0, p_idx, True) for h in range(hp)]
    steps += [(h, 1, nq - 1 - p_idx, True) for h in range(hp)]

    ahead = hp
    pending = [scores(*st[:3]) for st in steps[:ahead]]
    for t, (h, sel, j, diag) in enumerate(steps):
        s = pending.pop(0)
        if t + ahead < len(steps):
            pending.append(scores(*steps[t + ahead][:3]))
        consume(h, sel, j, diagonal(s) if diag else s)
    for h, (_, _, acc2_ref) in enumerate(heads):
        for sel in range(2):
            o_ref[sel, :, h * LANES:(h + 1) * LANES] = (
                acc2_ref[sel, :, :LANES] / acc2_ref[sel, :, LANES:]).astype(o_ref.dtype)


def _flash(q, k, v, v_col0, n_heads, *, tq=512, hp=FLASH_HEADS_PER_STEP):
    s = q.shape[0]
    nq = s // tq
    assert n_heads % hp == 0 and v_col0 % hp == 0
    per_head = [pltpu.VMEM((2, tq, HEAD_SLOT), BF16), pltpu.VMEM((2, tq, LANES), F32),
                pltpu.VMEM((2, tq, 2 * LANES), F32)]
    return pl.pallas_call(
        functools.partial(_flash_kernel, tq=tq, nq=nq, hp=hp),
        out_shape=jax.ShapeDtypeStruct((2, s // 2, n_heads * LANES), BF16),
        grid=(n_heads // hp, nq // 2),
        in_specs=[pl.BlockSpec((tq, hp * HEAD_SLOT), lambda h, p: (p, h)),
                  pl.BlockSpec((tq, hp * HEAD_SLOT), lambda h, p: (nq - 1 - p, h)),
                  pl.BlockSpec((s, hp * HEAD_SLOT), lambda h, p: (0, h)),
                  pl.BlockSpec((s, hp * LANES), lambda h, p: (0, v_col0 // hp + h))],
        out_specs=pl.BlockSpec((2, tq, hp * LANES), lambda h, p: (0, p, h)),
        scratch_shapes=per_head * hp,
        compiler_params=_params("parallel", "parallel"),
        name="flash",
    )(q, q, k, v)


def _flash_row_block(i, tm, s, tq=512):
    per_tile = tq // tm
    tile = i // per_tile
    sub = i % per_tile
    nq = s // tq
    hi = (tile >= nq // 2).astype(jnp.int32)
    return hi, jnp.where(hi == 1, nq - 1 - tile, tile) * per_tile + sub


def _outproj_kernel(om_ref, ob_ref, gm_ref, gb_ref, wo_ref, x_ref, lg_ref, lb_ref, wrt_ref, br_ref,
                    x1_ref, idx_ref, pos_ref, gate_ref, cnt_ref):
    @pl.when(pl.program_id(0) == 0)
    def _():
        cnt_ref[...] = jnp.zeros(cnt_ref.shape, jnp.int32)

    a = jnp.concatenate([_rms(om_ref[...].astype(F32), gm_ref[...]).astype(BF16),
                         _rms(ob_ref[...].astype(F32), gb_ref[...]).astype(BF16)], axis=1)
    mixed = jnp.dot(a, wo_ref[...], preferred_element_type=F32)
    x1 = _layer_norm(DEEPNORM_ALPHA * x_ref[...] + mixed, lg_ref[...], lb_ref[...])
    x1_ref[...] = x1

    logits = _nt_dot(wrt_ref[...], x1, precision=lax.Precision.HIGHEST) + br_ref[...]
    tm = logits.shape[1]
    eidx = lax.broadcasted_iota(jnp.int32, (N_EXPERTS, 1), 0)
    rank = _topk_rank(logits, eidx)
    sel = rank < TOP_K
    mx = jnp.max(logits, axis=0, keepdims=True)
    p = jnp.where(sel, jnp.exp(logits - mx), 0.0)
    gates = p / jnp.sum(p, axis=0, keepdims=True)

    before = (lax.broadcasted_iota(jnp.int32, (tm, tm), 0)
              < lax.broadcasted_iota(jnp.int32, (tm, tm), 1)).astype(BF16)
    prefix = jnp.dot(sel.astype(BF16), before, preferred_element_type=F32).astype(jnp.int32)
    pos = cnt_ref[:, 0:1] + prefix
    cnt_ref[...] = cnt_ref[...] + jnp.sum(sel.astype(jnp.int32), axis=1, keepdims=True)

    pick = lambda vals, k, zero: jnp.sum(jnp.where(rank == k, vals, zero), axis=0, keepdims=True)
    pad_i = [jnp.zeros((8 - TOP_K, tm), jnp.int32)]
    idx_ref[...] = jnp.concatenate([pick(eidx, k, 0) for k in range(TOP_K)] + pad_i, axis=0)
    pos_ref[...] = jnp.concatenate([pick(pos, k, 0) for k in range(TOP_K)] + pad_i, axis=0)
    gate_rows = jnp.concatenate([pick(gates, k, 0.0) for k in range(TOP_K)]
                                + [jnp.zeros((LANES - TOP_K, tm), F32)], axis=0)
    gate_ref[...] = gate_rows.T


def _outproj(o_mla, o_moba, g_mla, g_moba, wo, x, ln_g, ln_b, wr_t, b_r, *, tm=512):
    s = x.shape[0]
    full = lambda a: pl.BlockSpec(a.shape, lambda i: (0,) * a.ndim)
    return pl.pallas_call(
        _outproj_kernel,
        out_shape=(jax.ShapeDtypeStruct((s, D_MODEL), F32),
                   jax.ShapeDtypeStruct((8, s), jnp.int32),
                   jax.ShapeDtypeStruct((8, s), jnp.int32),
                   jax.ShapeDtypeStruct((s, LANES), F32),
                   jax.ShapeDtypeStruct((N_EXPERTS, LANES), jnp.int32)),
        grid=(s // tm,),
        in_specs=[pl.BlockSpec((None, tm, MLA_WIDTH), lambda i: (*_flash_row_block(i, tm, s), 0)),
                  pl.BlockSpec((None, tm, MOBA_WIDTH), lambda i: (*_flash_row_block(i, tm, s), 0)),
                  full(g_mla), full(g_moba), full(wo),
                  pl.BlockSpec((tm, D_MODEL), lambda i: (i, 0)),
                  full(ln_g), full(ln_b), full(wr_t), full(b_r)],
        out_specs=(pl.BlockSpec((tm, D_MODEL), lambda i: (i, 0)),
                   pl.BlockSpec((8, tm), lambda i: (0, i)),
                   pl.BlockSpec((8, tm), lambda i: (0, i)),
                   pl.BlockSpec((tm, LANES), lambda i: (i, 0)),
                   pl.BlockSpec((N_EXPERTS, LANES), lambda i: (0, 0))),
        compiler_params=_params("arbitrary"),
        name="outproj_router",
    )(o_mla, o_moba, g_mla, g_moba, wo, x, ln_g, ln_b, wr_t, b_r)


def _ffn_kernel(ie_ref, nb_ref, tab_hbm, x1_hbm, wg_ref, wu_ref, bg_ref, bu_ref, wd_ref, bd_ref,
                y_hbm, tab_smem, xf_ref, xb_ref, y_ref, sem_tab, sem_in, sem_out,
                *, nf, n_tok, max_blocks):
    w = pl.program_id(0)
    s = pl.program_id(1)
    n_work = pl.num_programs(0)
    nb = nb_ref[w]
    nb_prev = nb_ref[jnp.maximum(w - 1, 0)]
    w_next = jnp.minimum(w + 1, n_work - 1)
    has_next = (w + 1 < n_work) & (nb_ref[w_next] > 0)
    sub = SUBLANES
    tiles_per_block = MOE_BLOCK // sub
    slot = lax.bitwise_and(w, 1)

    tab_len = tab_smem.shape[0] // 2

    def load_table(item, slot_):
        dst = tab_smem.at[pl.ds(pl.multiple_of(slot_ * tab_len, tab_len), tab_len)]
        cp = pltpu.make_async_copy(tab_hbm.at[item], dst, sem_tab)
        cp.start()
        cp.wait()

    def for_rows(n_blocks, slot_, fn):
        def body(b, carry):
            base = slot_ * tab_len + b * MOE_BLOCK
            for i in range(MOE_BLOCK):
                fn(b * tiles_per_block + i // sub, i % sub, tab_smem[base + i])
            return carry
        lax.fori_loop(0, n_blocks, body, 0)

    def wait_tiles(n_blocks, make_copy):
        def body(g, carry):
            make_copy().wait()
            return carry
        lax.fori_loop(0, n_blocks * tiles_per_block, body, 0)

    def hbm_row(ref, r):
        return ref.at[lax.shift_right_logical(r, 3), pl.ds(lax.bitwise_and(r, sub - 1), 1)]

    def start_gather(n_blocks, slot_):
        for_rows(n_blocks, slot_, lambda g, j, e: pltpu.make_async_copy(
            hbm_row(x1_hbm, lax.bitwise_and(e, n_tok - 1)), xf_ref.at[g, pl.ds(j, 1)],
            sem_in).start())

    def wait_scatter(n_blocks):
        wait_tiles(n_blocks, lambda: pltpu.make_async_copy(y_ref.at[0], y_hbm.at[0], sem_out))

    @pl.when((nb > 0) & (s == 0))
    def _():
        @pl.when(w == 0)
        def _():
            load_table(0, 0)
            start_gather(nb, 0)
            y_ref[...] = jnp.zeros(y_ref.shape, F32)
            dump = pltpu.make_async_copy(
                y_ref, y_hbm.at[pl.ds(TOP_K * n_tok // sub, MOE_ROWS // sub)], sem_out)
            dump.start()
            dump.wait()

        wait_tiles(nb, lambda: pltpu.make_async_copy(x1_hbm.at[0], xf_ref.at[0], sem_in))

        def cast(b, carry):
            tiles = pl.ds(b * tiles_per_block, tiles_per_block)
            rows = pl.ds(pl.multiple_of(b * MOE_BLOCK, MOE_BLOCK), MOE_BLOCK)
            xb_ref[rows, :] = xf_ref[tiles].reshape(MOE_BLOCK, D_MODEL).astype(BF16)
            return carry

        lax.fori_loop(0, nb, cast, 0)

        @pl.when(w > 0)
        def _():
            wait_scatter(nb_prev)

        bias = jnp.broadcast_to(bd_ref[...], (tiles_per_block, sub, D_MODEL))

        def init(b, carry):
            y_ref[pl.ds(b * tiles_per_block, tiles_per_block)] = bias
            return carry

        lax.fori_loop(0, nb, init, 0)

    def step(m):
        x = xb_ref[0:m, :]
        g = jnp.dot(x, wg_ref[...].astype(BF16), preferred_element_type=F32) + bg_ref[...]
        u = jnp.dot(x, wu_ref[...].astype(BF16), preferred_element_type=F32) + bu_ref[...]
        g = jnp.minimum(g, SWIGLU_LIMIT)
        u = jnp.clip(u, -SWIGLU_LIMIT, SWIGLU_LIMIT)
        act = ((u + 1.0) * (g * (1.0 / (1.0 + jnp.exp(-SWIGLU_ALPHA * g))))).astype(BF16)
        down = jnp.dot(act, wd_ref[...].astype(BF16), preferred_element_type=F32)
        y_ref[0:m // sub] = y_ref[0:m // sub] + down.reshape(m // sub, sub, D_MODEL)

    for k in range(1, max_blocks + 1):
        pl.when(nb == k)(functools.partial(step, k * MOE_BLOCK))

    @pl.when((nb > 0) & (s == nf - 1))
    def _():
        for_rows(nb, slot, lambda g, j, e: pltpu.make_async_copy(
            y_ref.at[g, pl.ds(j, 1)], hbm_row(y_hbm, e), sem_out).start())

        @pl.when(has_next)
        def _():
            load_table(w_next, 1 - slot)
            start_gather(nb_ref[w_next], 1 - slot)

        @pl.when(jnp.logical_not(has_next))
        def _():
            wait_scatter(nb)


def _ffn(item_e, item_nb, tab, x1, w_gate_up, b_gate_up, w_down, b_down, *, tf=256):
    n_work, tab_len = tab.shape
    n_tok = x1.shape[0]
    nf = D_FF // tf
    n_out = TOP_K * n_tok + MOE_ROWS
    assert n_tok & (n_tok - 1) == 0 and MOE_ROWS <= n_tok and MOE_ROWS % MOE_BLOCK == 0
    assert n_tok % SUBLANES == 0 and n_out % SUBLANES == 0 and tab_len >= MOE_ROWS

    def tile(w, s, nbr):
        return jnp.where(nbr[w] > 0, s, nf - 1)

    def colmap(off):
        return lambda w, s, ie, nbr: (ie[w], 0, off + tile(w, s, nbr))

    def rowmap(w, s, ie, nbr):
        return ie[w], tile(w, s, nbr), 0

    grid_spec = pltpu.PrefetchScalarGridSpec(
        num_scalar_prefetch=2,
        grid=(n_work, nf),
        in_specs=[
            pl.BlockSpec(memory_space=pl.ANY),
            pl.BlockSpec(memory_space=pl.ANY),
            pl.BlockSpec((None, D_MODEL, tf), colmap(0)),
            pl.BlockSpec((None, D_MODEL, tf), colmap(nf)),
            pl.BlockSpec((None, 1, tf), colmap(0)),
            pl.BlockSpec((None, 1, tf), colmap(nf)),
            pl.BlockSpec((None, tf, D_MODEL), rowmap),
            pl.BlockSpec((None, 1, D_MODEL), lambda w, s, ie, nbr: (ie[w], 0, 0)),
        ],
        out_specs=pl.BlockSpec(memory_space=pl.ANY),
        scratch_shapes=[pltpu.SMEM((2 * tab_len,), jnp.int32),
                        pltpu.VMEM((MOE_ROWS // SUBLANES, SUBLANES, D_MODEL), F32),
                        pltpu.VMEM((MOE_ROWS, D_MODEL), BF16),
                        pltpu.VMEM((MOE_ROWS // SUBLANES, SUBLANES, D_MODEL), F32),
                        pltpu.SemaphoreType.DMA, pltpu.SemaphoreType.DMA, pltpu.SemaphoreType.DMA],
    )
    y = pl.pallas_call(
        functools.partial(_ffn_kernel, nf=nf, n_tok=n_tok, max_blocks=MOE_ROWS // MOE_BLOCK),
        out_shape=jax.ShapeDtypeStruct((n_out // SUBLANES, SUBLANES, D_MODEL), F32),
        grid_spec=grid_spec,
        compiler_params=_params("arbitrary", "arbitrary"),
        name="experts",
    )(item_e, item_nb, tab, x1.reshape(n_tok // SUBLANES, SUBLANES, D_MODEL), w_gate_up, w_gate_up,
      b_gate_up, b_gate_up, w_down, b_down)
    return y.reshape(n_out, D_MODEL)


def _final_kernel(*refs):
    y_refs = refs[:TOP_K]
    x1_ref, gate_ref, g_ref, b_ref, o_ref = refs[TOP_K:]
    ffn = gate_ref[:, 0:1] * y_refs[0][...]
    for k in range(1, TOP_K):
        ffn = ffn + gate_ref[:, k:k + 1] * y_refs[k][...]
    o_ref[...] = _layer_norm(DEEPNORM_ALPHA * x1_ref[...] + ffn, g_ref[...], b_ref[...])


def _final(y, x1, gates, g, b, *, tm=256):
    s = x1.shape[0]
    plane = lambda k: pl.BlockSpec((tm, D_MODEL), lambda i: (k * (s // tm) + i, 0))
    return pl.pallas_call(
        _final_kernel,
        out_shape=jax.ShapeDtypeStruct((s, D_MODEL), F32),
        grid=(s // tm,),
        in_specs=[plane(k) for k in range(TOP_K)] + [
            pl.BlockSpec((tm, D_MODEL), lambda i: (i, 0)),
            pl.BlockSpec((tm, LANES), lambda i: (i, 0)),
            pl.BlockSpec((1, D_MODEL), lambda i: (0, 0)),
            pl.BlockSpec((1, D_MODEL), lambda i: (0, 0))],
        out_specs=pl.BlockSpec((tm, D_MODEL), lambda i: (i, 0)),
        compiler_params=_params("parallel"),
        name="combine_final_ln",
    )(*([y] * TOP_K), x1, gates, g, b)


def _layer(x, pos, w_in, g_q_a, w_q_b, g_kv_a, w_kv_b, g_out_mla, g_out_moba, w_o, ln1_g, ln1_b,
           w_router, b_router, w_gate_up, b_gate_up, w_down, b_down, ln2_g, ln2_b):
    s = x.shape[0]
    wq = w_q_b.reshape(Q_LORA, MLA_HEADS, QK_NOPE + QK_ROPE)
    wqn = wq[:, :, :QK_NOPE].reshape(Q_LORA, MLA_HEADS * LANES).astype(BF16)
    wqr = jnp.pad(wq[:, :, QK_NOPE:], ((0, 0), (0, 0), (0, LANES - QK_ROPE))).reshape(
        Q_LORA, MLA_HEADS * LANES).astype(BF16)
    wkv = w_kv_b.reshape(KV_LORA, MLA_HEADS, QK_NOPE + V_HEAD)
    wkn = wkv[:, :, :QK_NOPE].reshape(KV_LORA, MLA_HEADS * LANES).astype(BF16)
    wv = wkv[:, :, QK_NOPE:].reshape(KV_LORA, MLA_WIDTH).astype(BF16)
    row = lambda a: a.reshape(1, -1)

    cos, sin = _rope_angles(pos.reshape(s, 1))
    h = _inproj(x, w_in.T)
    q_mla, k_mla, v_mla = _mla_prep(h, cos, sin, row(g_q_a), row(g_kv_a), wqn, wqr, wkn, wv)
    k_moba, kbar = _moba_kprep(h, cos, sin)
    q_moba = _moba_qprep(h, cos, sin, kbar)
    o_mla = _flash(q_mla, k_mla, v_mla, 0, MLA_HEADS)
    o_moba = _flash(q_moba, k_moba, h, 3 * MOBA_WIDTH // LANES, MOBA_HEADS)

    x1, idx_t, pos_t, gates, cnt = _outproj(o_mla, o_moba, row(g_out_mla), row(g_out_moba),
                                            w_o.astype(BF16), x, row(ln1_g), row(ln1_b),
                                            w_router.T, b_router.reshape(N_EXPERTS, 1))

    n_work = -(-s * TOP_K // MOE_ROWS) + N_EXPERTS
    counts = cnt[:, 0]
    items_per_e = (counts + MOE_ROWS - 1) // MOE_ROWS
    item_end = jnp.cumsum(items_per_e)
    item_start = item_end - items_per_e
    w_idx = jnp.arange(n_work)
    item_e = jnp.minimum(jnp.sum(w_idx[:, None] >= item_end[None, :], axis=1), N_EXPERTS - 1)
    item_rows = jnp.clip(counts[item_e] - (w_idx - item_start[item_e]) * MOE_ROWS, 0, MOE_ROWS)
    item_nb = jnp.where(w_idx < item_end[-1], (item_rows + MOE_BLOCK - 1) // MOE_BLOCK, 0)
    e_k = idx_t[:TOP_K]
    p_k = pos_t[:TOP_K]
    start_k = jnp.sum(jnp.where(e_k[..., None] == jnp.arange(N_EXPERTS), item_start, 0), axis=-1)
    item_row = ((start_k + p_k // MOE_ROWS) * MOE_ROWS + p_k % MOE_ROWS).astype(jnp.int32)
    out_row = (jnp.arange(TOP_K, dtype=jnp.int32)[:, None] * s
               + jnp.arange(s, dtype=jnp.int32)[None, :])
    dump = TOP_K * s + jnp.arange(n_work * MOE_ROWS, dtype=jnp.int32) % MOE_ROWS
    tab = dump.at[item_row.reshape(-1)].set(out_row.reshape(-1))
    tab = jnp.pad(tab.reshape(n_work, MOE_ROWS), ((0, 0), (0, MOE_TAB_LEN - MOE_ROWS)))

    y = _ffn(item_e.astype(jnp.int32), item_nb.astype(jnp.int32), tab, x1, w_gate_up,
             b_gate_up.reshape(N_EXPERTS, 1, 2 * D_FF), w_down, b_down.reshape(N_EXPERTS, 1, D_MODEL))
    return _final(y, x1, gates, row(ln2_g), row(ln2_b))


def kernel(x, positions, w_in, g_q_a, w_q_b, g_kv_a, w_kv_b, g_out_mla, g_out_moba, w_o, ln1_g, ln1_b,
           w_router, b_router, w_gate_up, b_gate_up, w_down, b_down, ln2_g, ln2_b):
    b, s, d = x.shape
    assert b == 1 and s == SEQ and d == D_MODEL and w_in.shape[0] == DEPTH
    hcur = x[0]
    for l in range(DEPTH):
        hcur = _layer(hcur, positions[0], w_in[l], g_q_a[l], w_q_b[l], g_kv_a[l], w_kv_b[l],
                      g_out_mla[l], g_out_moba[l], w_o[l], ln1_g[l], ln1_b[l], w_router[l],
                      b_router[l], w_gate_up[l], b_gate_up[l], w_down[l], b_down[l], ln2_g[l],
                      ln2_b[l])
    return hcur[None]
```

```python
import functools

import numpy as np
import jax
import jax.numpy as jnp
from jax import lax
from jax.experimental import pallas as pl
from jax.experimental.pallas import tpu as pltpu

D_MODEL = 2048
SEQ = 8192
MLA_HEADS = 8
QK_NOPE = 128
QK_ROPE = 64
V_HEAD = 128
Q_LORA = 512
KV_LORA = 256
MLA_WIDTH = MLA_HEADS * V_HEAD
MOBA_HEADS = 8
MOBA_HEAD_DIM = 128
MOBA_WIDTH = MOBA_HEADS * MOBA_HEAD_DIM
MOBA_BLOCK = 256
MOBA_TOPK = 3
N_MOBA_BLOCKS = SEQ // MOBA_BLOCK
ROPE_THETA = 10000.0
N_EXPERTS = 32
TOP_K = 4
D_FF = D_MODEL
SWIGLU_LIMIT = 7.0
SWIGLU_ALPHA = 1.702
MOE_BLOCK = 256
MOE_ROWS = 1536
DEPTH = 1
DEEPNORM_ALPHA = float((2 * DEPTH) ** 0.25)
RMS_EPS = 1e-6
LN_EPS = 1e-5

LANES = 128
SUBLANES = 8
MOE_TAB_LEN = 2048
COMBINE_ROWS = 256
FLASH_HEADS_PER_STEP = 2
HEAD_SLOT = 2 * LANES
H_GROUP = 1024
MASK_BIAS = -float(2 ** 17)
V7X_VMEM_LIMIT = 56 * 1024 * 1024

F32 = jnp.float32
BF16 = jnp.bfloat16


def _params(*semantics):
    return pltpu.CompilerParams(dimension_semantics=semantics, vmem_limit_bytes=V7X_VMEM_LIMIT)


def _rms(xf, g):
    return xf * lax.rsqrt(jnp.mean(xf * xf, axis=-1, keepdims=True) + RMS_EPS) * g


def _layer_norm(xf, g, b):
    mu = jnp.mean(xf, axis=-1, keepdims=True)
    xc = xf - mu
    var = jnp.mean(xc * xc, axis=-1, keepdims=True)
    return xc * lax.rsqrt(var + LN_EPS) * g + b


def _nt_dot(a, b, **kw):
    return lax.dot_general(a, b, (((1,), (1,)), ((), ())), preferred_element_type=F32, **kw)


def _topk_rank(vals, row_idx):
    n = vals.shape[0]
    rank = jnp.zeros(vals.shape, jnp.int32)
    for jp in range(n):
        vj = vals[jp:jp + 1, :]
        ahead = (vj > vals) | ((vj == vals) & (jp < row_idx))
        rank = rank + ahead.astype(jnp.int32)
    return rank


def _inproj_kernel(x_ref, wt_ref, o_ref, xb_ref, *, mla_cols):
    j = pl.program_id(1)

    @pl.when(j == 0)
    def _():
        xb_ref[...] = x_ref[...].astype(BF16)

    h = _nt_dot(xb_ref[...], wt_ref[...].astype(BF16))
    col = lax.broadcasted_iota(jnp.int32, h.shape, 1)
    o_ref[...] = jnp.where((j > 0) | (col < mla_cols), h, 0.0).astype(o_ref.dtype)


def _inproj(x, w_t, *, tm=1024):
    m, k = x.shape
    mla_cols = Q_LORA + KV_LORA + QK_ROPE
    n_groups = 1 + 3 * MOBA_WIDTH // H_GROUP
    assert w_t.shape == (mla_cols + 3 * MOBA_WIDTH, k) and mla_cols <= H_GROUP
    assert mla_cols % SUBLANES == 0 and MOBA_WIDTH % H_GROUP == 0

    def wmap(i, j):
        pad_tiles = (H_GROUP - mla_cols) // SUBLANES
        return (j * (H_GROUP // SUBLANES) - pad_tiles * jnp.minimum(j, 1)) * SUBLANES, 0

    return pl.pallas_call(
        functools.partial(_inproj_kernel, mla_cols=mla_cols),
        out_shape=jax.ShapeDtypeStruct((m, n_groups * H_GROUP), BF16),
        grid=(m // tm, n_groups),
        in_specs=[pl.BlockSpec((tm, k), lambda i, j: (i, 0)),
                  pl.BlockSpec((pl.Element(H_GROUP), pl.Element(k)), wmap)],
        out_specs=pl.BlockSpec((tm, H_GROUP), lambda i, j: (i, j)),
        scratch_shapes=[pltpu.VMEM((tm, k), BF16)],
        compiler_params=_params("parallel", "arbitrary"),
        name="inproj",
    )(x, w_t)


def _rope_angle_kernel(pos_ref, invf_ref, cos_ref, sin_ref):
    ang = pos_ref[...].astype(F32) * invf_ref[...]
    cos_ref[...] = jnp.cos(ang)
    sin_ref[...] = jnp.sin(ang)


def _rope_angles(pos_col, *, tm=1024):
    s = pos_col.shape[0]
    lane = np.arange(LANES)
    moba_half, mla_half = MOBA_HEAD_DIM // 2, QK_ROPE // 2
    f_moba = ROPE_THETA ** (-(lane % moba_half).astype(np.float32) * 2.0 / MOBA_HEAD_DIM)
    f_mla = ROPE_THETA ** (-((lane - moba_half) % mla_half).astype(np.float32) * 2.0 / QK_ROPE)
    invf = np.where(lane < moba_half, f_moba, np.where(lane < moba_half + mla_half, f_mla, 0.0))
    invf = jnp.asarray(invf.astype(np.float32)).reshape(1, LANES)
    return pl.pallas_call(
        _rope_angle_kernel,
        out_shape=(jax.ShapeDtypeStruct((s, LANES), F32), jax.ShapeDtypeStruct((s, LANES), F32)),
        grid=(s // tm,),
        in_specs=[pl.BlockSpec((tm, 1), lambda i: (i, 0)), pl.BlockSpec((1, LANES), lambda i: (0, 0))],
        out_specs=(pl.BlockSpec((tm, LANES), lambda i: (i, 0)),
                   pl.BlockSpec((tm, LANES), lambda i: (i, 0))),
        compiler_params=_params("parallel"),
        name="rope_angles",
    )(pos_col, invf)


def _rope_tables_mla(cos_ref, sin_ref):
    c = cos_ref[...]
    s = sin_ref[...]
    lane = lax.broadcasted_iota(jnp.int32, c.shape, 1)
    half = QK_ROPE // 2
    lo = lane < half
    cos = jnp.where(lo, pltpu.roll(c, LANES // 2, 1), pltpu.roll(c, LANES - half, 1))
    sin_a = jnp.where(lo, -pltpu.roll(s, LANES // 2, 1), 0.0)
    sin_b = jnp.where((lane >= half) & (lane < QK_ROPE), pltpu.roll(s, LANES - half, 1), 0.0)
    return cos, sin_a, sin_b


def _rope_mla(t, cos, sin_a, sin_b):
    return (t * cos + pltpu.roll(t, LANES - QK_ROPE // 2, 1) * sin_a
            + pltpu.roll(t, QK_ROPE // 2, 1) * sin_b)


def _rope_tables_moba(cos_ref, sin_ref):
    c = cos_ref[...]
    s = sin_ref[...]
    lane = lax.broadcasted_iota(jnp.int32, c.shape, 1)
    lo = lane < MOBA_HEAD_DIM // 2
    half = MOBA_HEAD_DIM // 2
    return jnp.where(lo, c, pltpu.roll(c, half, 1)), jnp.where(lo, -s, pltpu.roll(s, half, 1))


def _rope_moba(t, cos, sin_signed):
    return t * cos + pltpu.roll(t, MOBA_HEAD_DIM // 2, 1) * sin_signed


def _mla_prep_kernel(h_ref, cos_ref, sin_ref, gq_ref, gkv_ref, wqn_ref, wqr_ref, wkn_ref, wv_ref,
                     q_ref, k_ref, v_ref):
    scale = (QK_NOPE + QK_ROPE) ** -0.5
    cos, sin_a, sin_b = _rope_tables_mla(cos_ref, sin_ref)
    hq = h_ref[:, 0:Q_LORA].astype(F32)
    hkv = h_ref[:, Q_LORA:Q_LORA + KV_LORA].astype(F32)
    hkr = h_ref[:, Q_LORA + KV_LORA:Q_LORA + KV_LORA + LANES].astype(F32)

    qn = _rms(hq, gq_ref[...]).astype(BF16)
    q_nope = jnp.dot(qn, wqn_ref[...], preferred_element_type=F32)
    q_rope = jnp.dot(qn, wqr_ref[...], preferred_element_type=F32)
    kvn = _rms(hkv, gkv_ref[...]).astype(BF16)
    k_nope = jnp.dot(kvn, wkn_ref[...], preferred_element_type=F32)
    v_ref[...] = jnp.dot(kvn, wv_ref[...], preferred_element_type=F32).astype(BF16)
    kpe = _rope_mla(hkr, cos, sin_a, sin_b).astype(BF16)
    for h in range(MLA_HEADS):
        lo = h * HEAD_SLOT
        hs = slice(h * LANES, (h + 1) * LANES)
        q_ref[:, lo:lo + LANES] = (q_nope[:, hs] * scale).astype(BF16)
        q_ref[:, lo + LANES:lo + HEAD_SLOT] = (
            _rope_mla(q_rope[:, hs], cos, sin_a, sin_b) * scale).astype(BF16)
        k_ref[:, lo:lo + LANES] = k_nope[:, hs].astype(BF16)
        k_ref[:, lo + LANES:lo + HEAD_SLOT] = kpe


def _mla_prep(h, cos, sin, gq, gkv, wqn, wqr, wkn, wv, *, tm=512):
    s = h.shape[0]
    full = lambda a: pl.BlockSpec(a.shape, lambda i: (0,) * a.ndim)
    return pl.pallas_call(
        _mla_prep_kernel,
        out_shape=(jax.ShapeDtypeStruct((s, MLA_HEADS * HEAD_SLOT), BF16),
                   jax.ShapeDtypeStruct((s, MLA_HEADS * HEAD_SLOT), BF16),
                   jax.ShapeDtypeStruct((s, MLA_WIDTH), BF16)),
        grid=(s // tm,),
        in_specs=[pl.BlockSpec((tm, 1024), lambda i: (i, 0)),
                  pl.BlockSpec((tm, LANES), lambda i: (i, 0)),
                  pl.BlockSpec((tm, LANES), lambda i: (i, 0)),
                  full(gq), full(gkv), full(wqn), full(wqr), full(wkn), full(wv)],
        out_specs=(pl.BlockSpec((tm, MLA_HEADS * HEAD_SLOT), lambda i: (i, 0)),
                   pl.BlockSpec((tm, MLA_HEADS * HEAD_SLOT), lambda i: (i, 0)),
                   pl.BlockSpec((tm, MLA_WIDTH), lambda i: (i, 0))),
        compiler_params=_params("parallel"),
        name="mla_prep",
    )(h, cos, sin, gq, gkv, wqn, wqr, wkn, wv)


def _moba_kprep_kernel(h_ref, cos_ref, sin_ref, k_ref, kbar_ref, *, rows):
    cos, sin_signed = _rope_tables_moba(cos_ref, sin_ref)
    row = pl.program_id(0) * rows + lax.broadcasted_iota(jnp.int32, (rows, LANES), 0)
    lane = lax.broadcasted_iota(jnp.int32, (rows, LANES), 1)
    onehot = (lane == row // MOBA_BLOCK).astype(BF16)
    for h in range(MOBA_HEADS):
        lo = h * HEAD_SLOT
        kr = _rope_moba(h_ref[:, h * LANES:(h + 1) * LANES].astype(F32), cos, sin_signed)
        k_ref[:, lo:lo + LANES] = kr.astype(BF16)
        k_ref[:, lo + LANES:lo + HEAD_SLOT] = onehot
        for b in range(rows // MOBA_BLOCK):
            kbar_ref[b:b + 1, h * LANES:(h + 1) * LANES] = jnp.mean(
                kr[b * MOBA_BLOCK:(b + 1) * MOBA_BLOCK], axis=0, keepdims=True)


def _moba_kprep(h, cos, sin, *, rows=2048):
    s = h.shape[0]
    return pl.pallas_call(
        functools.partial(_moba_kprep_kernel, rows=rows),
        out_shape=(jax.ShapeDtypeStruct((s, MOBA_HEADS * HEAD_SLOT), BF16),
                   jax.ShapeDtypeStruct((s // MOBA_BLOCK, MOBA_WIDTH), F32)),
        grid=(s // rows,),
        in_specs=[pl.BlockSpec((rows, MOBA_WIDTH), lambda i: (i, 2)),
                  pl.BlockSpec((rows, LANES), lambda i: (i, 0)),
                  pl.BlockSpec((rows, LANES), lambda i: (i, 0))],
        out_specs=(pl.BlockSpec((rows, MOBA_HEADS * HEAD_SLOT), lambda i: (i, 0)),
                   pl.BlockSpec((rows // MOBA_BLOCK, MOBA_WIDTH), lambda i: (i, 0))),
        compiler_params=_params("parallel"),
        name="moba_kprep",
    )(h, cos, sin)


def _top_mask(vals, row_idx, k):
    n = vals.shape[0]
    chosen = jnp.zeros(vals.shape, jnp.bool_)
    for _ in range(k):
        top = jnp.max(vals, axis=0, keepdims=True)
        first = jnp.min(jnp.where(vals == top, row_idx, n), axis=0, keepdims=True)
        pick = row_idx == first
        chosen = chosen | pick
        vals = jnp.where(pick, -jnp.inf, vals)
    return chosen


def _moba_qprep_kernel(h_ref, cos_ref, sin_ref, kbar_ref, q_ref, *, tm):
    scale = MOBA_HEAD_DIM ** -0.5
    cos, sin_signed = _rope_tables_moba(cos_ref, sin_ref)
    nb = N_MOBA_BLOCKS
    tok = pl.program_id(0) * tm + lax.broadcasted_iota(jnp.int32, (1, tm), 1)
    qblk = tok // MOBA_BLOCK
    blk = lax.broadcasted_iota(jnp.int32, (nb, 1), 0)
    past = blk < qblk
    own = blk == qblk
    for h in range(MOBA_HEADS):
        lo = h * HEAD_SLOT
        qr = _rope_moba(h_ref[:, h * LANES:(h + 1) * LANES].astype(F32), cos, sin_signed)
        gate = _nt_dot(kbar_ref[:, h * LANES:(h + 1) * LANES], qr,
                       precision=lax.Precision.HIGHEST)
        gate = jnp.where(past, gate, -jnp.inf)
        sel = past & _top_mask(gate, blk, MOBA_TOPK)
        bias = jnp.where(sel | own, 0.0, MASK_BIAS)
        bias = jnp.concatenate([bias, jnp.zeros((LANES - nb, tm), F32)], axis=0)
        q_ref[:, lo:lo + LANES] = (qr * scale).astype(BF16)
        q_ref[:, lo + LANES:lo + HEAD_SLOT] = bias.T.astype(BF16)


def _moba_qprep(h, cos, sin, kbar, *, tm=512):
    s = h.shape[0]
    return pl.pallas_call(
        functools.partial(_moba_qprep_kernel, tm=tm),
        out_shape=jax.ShapeDtypeStruct((s, MOBA_HEADS * HEAD_SLOT), BF16),
        grid=(s // tm,),
        in_specs=[pl.BlockSpec((tm, MOBA_WIDTH), lambda i: (i, 1)),
                  pl.BlockSpec((tm, LANES), lambda i: (i, 0)),
                  pl.BlockSpec((tm, LANES), lambda i: (i, 0)),
                  pl.BlockSpec(kbar.shape, lambda i: (0, 0))],
        out_specs=pl.BlockSpec((tm, MOBA_HEADS * HEAD_SLOT), lambda i: (i, 0)),
        compiler_params=_params("parallel"),
        name="moba_qprep",
    )(h, cos, sin, kbar)


def _flash_kernel(qa_ref, qb_ref, k_ref, v_ref, o_ref, *scratch, tq, nq, hp):
    p_idx = pl.program_id(1)
    nc = tq // LANES
    ones = jnp.ones((tq, LANES), BF16)
    heads = [scratch[3 * h:3 * h + 3] for h in range(hp)]
    for h, (q2_ref, m2_ref, acc2_ref) in enumerate(heads):
        q2_ref[0] = qa_ref[:, h * HEAD_SLOT:(h + 1) * HEAD_SLOT]
        q2_ref[1] = qb_ref[:, h * HEAD_SLOT:(h + 1) * HEAD_SLOT]
        m2_ref[...] = jnp.full(m2_ref.shape, -jnp.inf, F32)
        acc2_ref[...] = jnp.zeros(acc2_ref.shape, F32)

    def kv_rows(j):
        return pl.ds(pl.multiple_of(j * tq, tq), tq)

    def scores(h, sel, j):
        return _nt_dot(heads[h][0][sel], k_ref[kv_rows(j), h * HEAD_SLOT:(h + 1) * HEAD_SLOT])

    def consume(h, sel, j, s):
        _, m2_ref, acc2_ref = heads[h]
        v = v_ref[kv_rows(j), h * LANES:(h + 1) * LANES]
        chunks = [s[:, c * LANES:(c + 1) * LANES] for c in range(nc)]
        part = functools.reduce(jnp.maximum, chunks)
        m_old = m2_ref[sel]
        m_new = jnp.maximum(m_old, jnp.max(part, axis=1, keepdims=True))
        alpha = jnp.exp(m_old - m_new)
        p = jnp.concatenate([jnp.exp(ch - m_new).astype(BF16) for ch in chunks], axis=1)
        pv = jnp.dot(p, jnp.concatenate([v, ones], axis=1), preferred_element_type=F32)
        acc2_ref[sel] = jnp.concatenate([alpha, alpha], axis=1) * acc2_ref[sel] + pv
        m2_ref[sel] = m_new

    def diagonal(s):
        r = lax.broadcasted_iota(jnp.int32, s.shape, 0)
        c = lax.broadcasted_iota(jnp.int32, s.shape, 1)
        return jnp.where(c <= r, s, -jnp.inf)

    steps = []
    for t in range(nq - 1):
        sel = (t >= p_idx).astype(jnp.int32)
        steps += [(h, sel, t - sel * p_idx, False) for h in range(hp)]
    steps += [(h, 0, p_idx, True) for h in range(hp)]
    steps += [(h, 1, nq - 1 - p_idx, True) for h in range(hp)]

    ahead = hp
    pending = [scores(*st[:3]) for st in steps[:ahead]]
    for t, (h, sel, j, diag) in enumerate(steps):
        s = pending.pop(0)
        if t + ahead < len(steps):
            pending.append(scores(*steps[t + ahead][:3]))
        consume(h, sel, j, diagonal(s) if diag else s)
    for h, (_, _, acc2_ref) in enumerate(heads):
        for sel in range(2):
            o_ref[sel, :, h * LANES:(h + 1) * LANES] = (
                acc2_ref[sel, :, :LANES] / acc2_ref[sel, :, LANES:]).astype(o_ref.dtype)


def _flash(q, k, v, v_col0, n_heads, *, tq=512, hp=FLASH_HEADS_PER_STEP):
    s = q.shape[0]
    nq = s // tq
    assert n_heads % hp == 0 and v_col0 % hp == 0
    per_head = [pltpu.VMEM((2, tq, HEAD_SLOT), BF16), pltpu.VMEM((2, tq, LANES), F32),
                pltpu.VMEM((2, tq, 2 * LANES), F32)]
    return pl.pallas_call(
        functools.partial(_flash_kernel, tq=tq, nq=nq, hp=hp),
        out_shape=jax.ShapeDtypeStruct((2, s // 2, n_heads * LANES), BF16),
        grid=(n_heads // hp, nq // 2),
        in_specs=[pl.BlockSpec((tq, hp * HEAD_SLOT), lambda h, p: (p, h)),
                  pl.BlockSpec((tq, hp * HEAD_SLOT), lambda h, p: (nq - 1 - p, h)),
                  pl.BlockSpec((s, hp * HEAD_SLOT), lambda h, p: (0, h)),
                  pl.BlockSpec((s, hp * LANES), lambda h, p: (0, v_col0 // hp + h))],
        out_specs=pl.BlockSpec((2, tq, hp * LANES), lambda h, p: (0, p, h)),
        scratch_shapes=per_head * hp,
        compiler_params=_params("parallel", "parallel"),
        name="flash",
    )(q, q, k, v)


def _flash_row_block(i, tm, s, tq=512):
    per_tile = tq // tm
    tile = i // per_tile
    sub = i % per_tile
    nq = s // tq
    hi = (tile >= nq // 2).astype(jnp.int32)
    return hi, jnp.where(hi == 1, nq - 1 - tile, tile) * per_tile + sub


def _outproj_kernel(om_ref, ob_ref, gm_ref, gb_ref, wo_ref, x_ref, lg_ref, lb_ref, wrt_ref, br_ref,
                    x1_ref, idx_ref, pos_ref, gate_ref, cnt_ref):
    @pl.when(pl.program_id(0) == 0)
    def _():
        cnt_ref[...] = jnp.zeros(cnt_ref.shape, jnp.int32)

    a = jnp.concatenate([_rms(om_ref[...].astype(F32), gm_ref[...]).astype(BF16),
                         _rms(ob_ref[...].astype(F32), gb_ref[...]).astype(BF16)], axis=1)
    mixed = jnp.dot(a, wo_ref[...], preferred_element_type=F32)
    x1 = _layer_norm(DEEPNORM_ALPHA * x_ref[...] + mixed, lg_ref[...], lb_ref[...])
    x1_ref[...] = x1

    logits = _nt_dot(wrt_ref[...], x1, precision=lax.Precision.HIGHEST) + br_ref[...]
    tm = logits.shape[1]
    eidx = lax.broadcasted_iota(jnp.int32, (N_EXPERTS, 1), 0)
    rank = _topk_rank(logits, eidx)
    sel = rank < TOP_K
    mx = jnp.max(logits, axis=0, keepdims=True)
    p = jnp.where(sel, jnp.exp(logits - mx), 0.0)
    gates = p / jnp.sum(p, axis=0, keepdims=True)

    before = (lax.broadcasted_iota(jnp.int32, (tm, tm), 0)
              < lax.broadcasted_iota(jnp.int32, (tm, tm), 1)).astype(BF16)
    prefix = jnp.dot(sel.astype(BF16), before, preferred_element_type=F32).astype(jnp.int32)
    pos = cnt_ref[:, 0:1] + prefix
    cnt_ref[...] = cnt_ref[...] + jnp.sum(sel.astype(jnp.int32), axis=1, keepdims=True)

    pick = lambda vals, k, zero: jnp.sum(jnp.where(rank == k, vals, zero), axis=0, keepdims=True)
    pad_i = [jnp.zeros((8 - TOP_K, tm), jnp.int32)]
    idx_ref[...] = jnp.concatenate([pick(eidx, k, 0) for k in range(TOP_K)] + pad_i, axis=0)
    pos_ref[...] = jnp.concatenate([pick(pos, k, 0) for k in range(TOP_K)] + pad_i, axis=0)
    gate_rows = jnp.concatenate([pick(gates, k, 0.0) for k in range(TOP_K)]
                                + [jnp.zeros((LANES - TOP_K, tm), F32)], axis=0)
    gate_ref[...] = gate_rows.T


def _outproj(o_mla, o_moba, g_mla, g_moba, wo, x, ln_g, ln_b, wr_t, b_r, *, tm=512):
    s = x.shape[0]
    full = lambda a: pl.BlockSpec(a.shape, lambda i: (0,) * a.ndim)
    return pl.pallas_call(
        _outproj_kernel,
        out_shape=(jax.ShapeDtypeStruct((s, D_MODEL), F32),
                   jax.ShapeDtypeStruct((8, s), jnp.int32),
                   jax.ShapeDtypeStruct((8, s), jnp.int32),
                   jax.ShapeDtypeStruct((s, LANES), F32),
                   jax.ShapeDtypeStruct((N_EXPERTS, LANES), jnp.int32)),
        grid=(s // tm,),
        in_specs=[pl.BlockSpec((None, tm, MLA_WIDTH), lambda i: (*_flash_row_block(i, tm, s), 0)),
                  pl.BlockSpec((None, tm, MOBA_WIDTH), lambda i: (*_flash_row_block(i, tm, s), 0)),
                  full(g_mla), full(g_moba), full(wo),
                  pl.BlockSpec((tm, D_MODEL), lambda i: (i, 0)),
                  full(ln_g), full(ln_b), full(wr_t), full(b_r)],
        out_specs=(pl.BlockSpec((tm, D_MODEL), lambda i: (i, 0)),
                   pl.BlockSpec((8, tm), lambda i: (0, i)),
                   pl.BlockSpec((8, tm), lambda i: (0, i)),
                   pl.BlockSpec((tm, LANES), lambda i: (i, 0)),
                   pl.BlockSpec((N_EXPERTS, LANES), lambda i: (0, 0))),
        compiler_params=_params("arbitrary"),
        name="outproj_router",
    )(o_mla, o_moba, g_mla, g_moba, wo, x, ln_g, ln_b, wr_t, b_r)


def _ffn_kernel(ie_ref, nb_ref, tab_hbm, x1_hbm, wg_ref, wu_ref, bg_ref, bu_ref, wd_ref, bd_ref,
                y_hbm, tab_smem, xf_ref, xb_ref, y_ref, sem_tab, sem_in, sem_out,
                *, nf, n_tok, max_blocks):
    w = pl.program_id(0)
    s = pl.program_id(1)
    n_work = pl.num_programs(0)
    nb = nb_ref[w]
    nb_prev = nb_ref[jnp.maximum(w - 1, 0)]
    w_next = jnp.minimum(w + 1, n_work - 1)
    has_next = (w + 1 < n_work) & (nb_ref[w_next] > 0)
    sub = SUBLANES
    tiles_per_block = MOE_BLOCK // sub
    slot = lax.bitwise_and(w, 1)

    tab_len = tab_smem.shape[0] // 2

    def load_table(item, slot_):
        dst = tab_smem.at[pl.ds(pl.multiple_of(slot_ * tab_len, tab_len), tab_len)]
        cp = pltpu.make_async_copy(tab_hbm.at[item], dst, sem_tab)
        cp.start()
        cp.wait()

    def for_rows(n_blocks, slot_, fn):
        def body(b, carry):
            base = slot_ * tab_len + b * MOE_BLOCK
            for i in range(MOE_BLOCK):
                fn(b * tiles_per_block + i // sub, i % sub, tab_smem[base + i])
            return carry
        lax.fori_loop(0, n_blocks, body, 0)

    def wait_tiles(n_blocks, make_copy):
        def body(g, carry):
            make_copy().wait()
            return carry
        lax.fori_loop(0, n_blocks * tiles_per_block, body, 0)

    def hbm_row(ref, r):
        return ref.at[lax.shift_right_logical(r, 3), pl.ds(lax.bitwise_and(r, sub - 1), 1)]

    def start_gather(n_blocks, slot_):
        for_rows(n_blocks, slot_, lambda g, j, e: pltpu.make_async_copy(
            hbm_row(x1_hbm, e), xf_ref.at[g, pl.ds(j, 1)],
            sem_in).start())

    def out_copy(b):
        tiles = pl.ds(b * tiles_per_block, tiles_per_block)
        dst = pl.ds(w * (MOE_ROWS // sub) + b * tiles_per_block, tiles_per_block)
        return pltpu.make_async_copy(y_ref.at[tiles], y_hbm.at[dst], sem_out)

    def wait_out(n_blocks):
        def body(b, carry):
            out_copy(0).wait()
            return carry
        lax.fori_loop(0, n_blocks, body, 0)

    @pl.when((nb > 0) & (s == 0))
    def _():
        @pl.when(w == 0)
        def _():
            load_table(0, 0)
            start_gather(nb, 0)

        wait_tiles(nb, lambda: pltpu.make_async_copy(x1_hbm.at[0], xf_ref.at[0], sem_in))

        def cast(b, carry):
            tiles = pl.ds(b * tiles_per_block, tiles_per_block)
            rows = pl.ds(pl.multiple_of(b * MOE_BLOCK, MOE_BLOCK), MOE_BLOCK)
            xb_ref[rows, :] = xf_ref[tiles].reshape(MOE_BLOCK, D_MODEL).astype(BF16)
            return carry

        lax.fori_loop(0, nb, cast, 0)

        @pl.when(w > 0)
        def _():
            wait_out(nb_prev)

        bias = jnp.broadcast_to(bd_ref[...], (tiles_per_block, sub, D_MODEL))

        def init(b, carry):
            y_ref[pl.ds(b * tiles_per_block, tiles_per_block)] = bias
            return carry

        lax.fori_loop(0, nb, init, 0)

    def step(m):
        x = xb_ref[0:m, :]
        g = jnp.dot(x, wg_ref[...].astype(BF16), preferred_element_type=F32) + bg_ref[...]
        u = jnp.dot(x, wu_ref[...].astype(BF16), preferred_element_type=F32) + bu_ref[...]
        g = jnp.minimum(g, SWIGLU_LIMIT)
        u = jnp.clip(u, -SWIGLU_LIMIT, SWIGLU_LIMIT)
        act = ((u + 1.0) * (g * (1.0 / (1.0 + jnp.exp(-SWIGLU_ALPHA * g))))).astype(BF16)
        down = jnp.dot(act, wd_ref[...].astype(BF16), preferred_element_type=F32)
        y_ref[0:m // sub] = y_ref[0:m // sub] + down.reshape(m // sub, sub, D_MODEL)

    for k in range(1, max_blocks + 1):
        pl.when(nb == k)(functools.partial(step, k * MOE_BLOCK))

    @pl.when((nb > 0) & (s == nf - 1))
    def _():
        def start_out(b, carry):
            out_copy(b).start()
            return carry

        lax.fori_loop(0, nb, start_out, 0)

        @pl.when(has_next)
        def _():
            load_table(w_next, 1 - slot)
            start_gather(nb_ref[w_next], 1 - slot)

        @pl.when(jnp.logical_not(has_next))
        def _():
            wait_out(nb)


def _ffn(item_e, item_nb, tab, x1, w_gate_up, b_gate_up, w_down, b_down, *, tf=256):
    n_work, tab_len = tab.shape
    n_tok = x1.shape[0]
    nf = D_FF // tf
    assert n_tok & (n_tok - 1) == 0 and MOE_ROWS <= n_tok and MOE_ROWS % MOE_BLOCK == 0
    assert n_tok % SUBLANES == 0 and tab_len >= MOE_ROWS

    def tile(w, s, nbr):
        return jnp.where(nbr[w] > 0, s, nf - 1)

    def colmap(off):
        return lambda w, s, ie, nbr: (ie[w], 0, off + tile(w, s, nbr))

    def rowmap(w, s, ie, nbr):
        return ie[w], tile(w, s, nbr), 0

    grid_spec = pltpu.PrefetchScalarGridSpec(
        num_scalar_prefetch=2,
        grid=(n_work, nf),
        in_specs=[
            pl.BlockSpec(memory_space=pl.ANY),
            pl.BlockSpec(memory_space=pl.ANY),
            pl.BlockSpec((None, D_MODEL, tf), colmap(0)),
            pl.BlockSpec((None, D_MODEL, tf), colmap(nf)),
            pl.BlockSpec((None, 1, tf), colmap(0)),
            pl.BlockSpec((None, 1, tf), colmap(nf)),
            pl.BlockSpec((None, tf, D_MODEL), rowmap),
            pl.BlockSpec((None, 1, D_MODEL), lambda w, s, ie, nbr: (ie[w], 0, 0)),
        ],
        out_specs=pl.BlockSpec(memory_space=pl.ANY),
        scratch_shapes=[pltpu.SMEM((2 * tab_len,), jnp.int32),
                        pltpu.VMEM((MOE_ROWS // SUBLANES, SUBLANES, D_MODEL), F32),
                        pltpu.VMEM((MOE_ROWS, D_MODEL), BF16),
                        pltpu.VMEM((MOE_ROWS // SUBLANES, SUBLANES, D_MODEL), F32),
                        pltpu.SemaphoreType.DMA, pltpu.SemaphoreType.DMA, pltpu.SemaphoreType.DMA],
    )
    return pl.pallas_call(
        functools.partial(_ffn_kernel, nf=nf, n_tok=n_tok, max_blocks=MOE_ROWS // MOE_BLOCK),
        out_shape=jax.ShapeDtypeStruct((n_work * MOE_ROWS // SUBLANES, SUBLANES, D_MODEL), F32),
        grid_spec=grid_spec,
        compiler_params=_params("arbitrary", "arbitrary"),
        name="experts",
    )(item_e, item_nb, tab, x1.reshape(n_tok // SUBLANES, SUBLANES, D_MODEL), w_gate_up, w_gate_up,
      b_gate_up, b_gate_up, w_down, b_down)


def _final_kernel(yrow_hbm, y_hbm, x1_ref, gate_ref, g_ref, b_ref, o_ref,
                  tab_smem, ybuf_ref, sem_tab, sem_rows, *, tm):
    i = pl.program_id(0)
    n = pl.num_programs(0)
    slot = lax.bitwise_and(i, 1)
    n_rows = TOP_K * tm
    sub = SUBLANES

    def fetch(tile, slot_):
        dst = tab_smem.at[pl.ds(pl.multiple_of(slot_ * n_rows, n_rows), n_rows)]
        cp = pltpu.make_async_copy(yrow_hbm.at[tile], dst, sem_tab)
        cp.start()
        cp.wait()
        base = slot_ * n_rows
        for j in range(n_rows):
            r = tab_smem[base + j]
            pltpu.make_async_copy(
                y_hbm.at[lax.shift_right_logical(r, 3), pl.ds(lax.bitwise_and(r, sub - 1), 1)],
                ybuf_ref.at[slot_, j // sub, pl.ds(j % sub, 1)], sem_rows.at[slot_]).start()

    @pl.when(i == 0)
    def _():
        fetch(0, 0)

    @pl.when(i + 1 < n)
    def _():
        fetch(i + 1, 1 - slot)

    pltpu.make_async_copy(y_hbm.at[pl.ds(0, n_rows // sub)], ybuf_ref.at[slot],
                          sem_rows.at[slot]).wait()
    ffn = None
    for k in range(TOP_K):
        yk = ybuf_ref[slot, k * tm // sub:(k + 1) * tm // sub].reshape(tm, D_MODEL)
        term = gate_ref[:, k:k + 1] * yk
        ffn = term if ffn is None else ffn + term
    o_ref[...] = _layer_norm(DEEPNORM_ALPHA * x1_ref[...] + ffn, g_ref[...], b_ref[...])


def _final(yrow, y, x1, gates, g, b, *, tm=COMBINE_ROWS):
    s = x1.shape[0]
    n_rows = TOP_K * tm
    return pl.pallas_call(
        functools.partial(_final_kernel, tm=tm),
        out_shape=jax.ShapeDtypeStruct((s, D_MODEL), F32),
        grid=(s // tm,),
        in_specs=[pl.BlockSpec(memory_space=pl.ANY),
                  pl.BlockSpec(memory_space=pl.ANY),
                  pl.BlockSpec((tm, D_MODEL), lambda i: (i, 0)),
                  pl.BlockSpec((tm, LANES), lambda i: (i, 0)),
                  pl.BlockSpec((1, D_MODEL), lambda i: (0, 0)),
                  pl.BlockSpec((1, D_MODEL), lambda i: (0, 0))],
        out_specs=pl.BlockSpec((tm, D_MODEL), lambda i: (i, 0)),
        scratch_shapes=[pltpu.SMEM((2 * n_rows,), jnp.int32),
                        pltpu.VMEM((2, n_rows // SUBLANES, SUBLANES, D_MODEL), F32),
                        pltpu.SemaphoreType.DMA, pltpu.SemaphoreType.DMA((2,))],
        compiler_params=_params("arbitrary"),
        name="combine_final_ln",
    )(yrow, y, x1, gates, g, b)


def _layer(x, pos, w_in, g_q_a, w_q_b, g_kv_a, w_kv_b, g_out_mla, g_out_moba, w_o, ln1_g, ln1_b,
           w_router, b_router, w_gate_up, b_gate_up, w_down, b_down, ln2_g, ln2_b):
    s = x.shape[0]
    wq = w_q_b.reshape(Q_LORA, MLA_HEADS, QK_NOPE + QK_ROPE)
    wqn = wq[:, :, :QK_NOPE].reshape(Q_LORA, MLA_HEADS * LANES).astype(BF16)
    wqr = jnp.pad(wq[:, :, QK_NOPE:], ((0, 0), (0, 0), (0, LANES - QK_ROPE))).reshape(
        Q_LORA, MLA_HEADS * LANES).astype(BF16)
    wkv = w_kv_b.reshape(KV_LORA, MLA_HEADS, QK_NOPE + V_HEAD)
    wkn = wkv[:, :, :QK_NOPE].reshape(KV_LORA, MLA_HEADS * LANES).astype(BF16)
    wv = wkv[:, :, QK_NOPE:].reshape(KV_LORA, MLA_WIDTH).astype(BF16)
    row = lambda a: a.reshape(1, -1)

    cos, sin = _rope_angles(pos.reshape(s, 1))
    h = _inproj(x, w_in.T)
    q_mla, k_mla, v_mla = _mla_prep(h, cos, sin, row(g_q_a), row(g_kv_a), wqn, wqr, wkn, wv)
    k_moba, kbar = _moba_kprep(h, cos, sin)
    q_moba = _moba_qprep(h, cos, sin, kbar)
    o_mla = _flash(q_mla, k_mla, v_mla, 0, MLA_HEADS)
    o_moba = _flash(q_moba, k_moba, h, 3 * MOBA_WIDTH // LANES, MOBA_HEADS)

    x1, idx_t, pos_t, gates, cnt = _outproj(o_mla, o_moba, row(g_out_mla), row(g_out_moba),
                                            w_o.astype(BF16), x, row(ln1_g), row(ln1_b),
                                            w_router.T, b_router.reshape(N_EXPERTS, 1))

    n_work = -(-s * TOP_K // MOE_ROWS) + N_EXPERTS
    counts = cnt[:, 0]
    items_per_e = (counts + MOE_ROWS - 1) // MOE_ROWS
    item_end = jnp.cumsum(items_per_e)
    item_start = item_end - items_per_e
    w_idx = jnp.arange(n_work)
    item_e = jnp.minimum(jnp.sum(w_idx[:, None] >= item_end[None, :], axis=1), N_EXPERTS - 1)
    item_rows = jnp.clip(counts[item_e] - (w_idx - item_start[item_e]) * MOE_ROWS, 0, MOE_ROWS)
    item_nb = jnp.where(w_idx < item_end[-1], (item_rows + MOE_BLOCK - 1) // MOE_BLOCK, 0)
    e_k = idx_t[:TOP_K]
    p_k = pos_t[:TOP_K]
    start_k = jnp.sum(jnp.where(e_k[..., None] == jnp.arange(N_EXPERTS), item_start, 0), axis=-1)
    item_row = ((start_k + p_k // MOE_ROWS) * MOE_ROWS + p_k % MOE_ROWS).astype(jnp.int32)
    tok = jnp.broadcast_to(jnp.arange(s, dtype=jnp.int32), (TOP_K, s))
    filler = jnp.arange(n_work * MOE_ROWS, dtype=jnp.int32) % MOE_ROWS
    tab = filler.at[item_row.reshape(-1)].set(tok.reshape(-1))
    tab = jnp.pad(tab.reshape(n_work, MOE_ROWS), ((0, 0), (0, MOE_TAB_LEN - MOE_ROWS)))

    y = _ffn(item_e.astype(jnp.int32), item_nb.astype(jnp.int32), tab, x1, w_gate_up,
             b_gate_up.reshape(N_EXPERTS, 1, 2 * D_FF), w_down, b_down.reshape(N_EXPERTS, 1, D_MODEL))
    tm = COMBINE_ROWS
    yrow = item_row.reshape(TOP_K, s // tm, tm).transpose(1, 0, 2).reshape(s // tm, TOP_K * tm)
    return _final(yrow, y, x1, gates, row(ln2_g), row(ln2_b))


def kernel(x, positions, w_in, g_q_a, w_q_b, g_kv_a, w_kv_b, g_out_mla, g_out_moba, w_o, ln1_g, ln1_b,
           w_router, b_router, w_gate_up, b_gate_up, w_down, b_down, ln2_g, ln2_b):
    b, s, d = x.shape
    assert b == 1 and s == SEQ and d == D_MODEL and w_in.shape[0] == DEPTH
    hcur = x[0]
    for l in range(DEPTH):
        hcur = _layer(hcur, positions[0], w_in[l], g_q_a[l], w_q_b[l], g_kv_a[l], w_kv_b[l],
                      g_out_mla[l], g_out_moba[l], w_o[l], ln1_g[l], ln1_b[l], w_router[l],
                      b_router[l], w_gate_up[l], b_gate_up[l], w_down[l], b_down[l], ln2_g[l],
                      ln2_b[l])
    return hcur[None]
```

```python
import functools

import numpy as np
import jax
import jax.numpy as jnp
from jax import lax
from jax.experimental import pallas as pl
from jax.experimental.pallas import tpu as pltpu

D_MODEL = 2048
SEQ = 8192
MLA_HEADS = 8
QK_NOPE = 128
QK_ROPE = 64
V_HEAD = 128
Q_LORA = 512
KV_LORA = 256
MLA_WIDTH = MLA_HEADS * V_HEAD
MOBA_HEADS = 8
MOBA_HEAD_DIM = 128
MOBA_WIDTH = MOBA_HEADS * MOBA_HEAD_DIM
MOBA_BLOCK = 256
MOBA_TOPK = 3
N_MOBA_BLOCKS = SEQ // MOBA_BLOCK
ROPE_THETA = 10000.0
N_EXPERTS = 32
TOP_K = 4
D_FF = D_MODEL
SWIGLU_LIMIT = 7.0
SWIGLU_ALPHA = 1.702
MOE_BLOCK = 256
MOE_ROWS = 1536
DEPTH = 1
DEEPNORM_ALPHA = float((2 * DEPTH) ** 0.25)
RMS_EPS = 1e-6
LN_EPS = 1e-5

LANES = 128
SUBLANES = 8
MOE_TAB_LEN = 2048
COMBINE_ROWS = 256
FLASH_HEADS_PER_STEP = 1
HEAD_SLOT = 2 * LANES
H_GROUP = 1024
MASK_BIAS = -float(2 ** 17)
V7X_VMEM_LIMIT = 56 * 1024 * 1024

F32 = jnp.float32
BF16 = jnp.bfloat16


def _params(*semantics):
    return pltpu.CompilerParams(dimension_semantics=semantics, vmem_limit_bytes=V7X_VMEM_LIMIT)


def _rms(xf, g):
    return xf * lax.rsqrt(jnp.mean(xf * xf, axis=-1, keepdims=True) + RMS_EPS) * g


def _layer_norm(xf, g, b):
    mu = jnp.mean(xf, axis=-1, keepdims=True)
    xc = xf - mu
    var = jnp.mean(xc * xc, axis=-1, keepdims=True)
    return xc * lax.rsqrt(var + LN_EPS) * g + b


def _nt_dot(a, b, **kw):
    return lax.dot_general(a, b, (((1,), (1,)), ((), ())), preferred_element_type=F32, **kw)


def _topk_rank(vals, row_idx):
    n = vals.shape[0]
    rank = jnp.zeros(vals.shape, jnp.int32)
    for jp in range(n):
        vj = vals[jp:jp + 1, :]
        ahead = (vj > vals) | ((vj == vals) & (jp < row_idx))
        rank = rank + ahead.astype(jnp.int32)
    return rank


def _inproj_kernel(x_ref, wt_ref, o_ref, xb_ref, *, mla_cols):
    j = pl.program_id(1)

    @pl.when(j == 0)
    def _():
        xb_ref[...] = x_ref[...].astype(BF16)

    h = _nt_dot(xb_ref[...], wt_ref[...].astype(BF16))
    col = lax.broadcasted_iota(jnp.int32, h.shape, 1)
    o_ref[...] = jnp.where((j > 0) | (col < mla_cols), h, 0.0).astype(o_ref.dtype)


def _inproj(x, w_t, *, tm=1024):
    m, k = x.shape
    mla_cols = Q_LORA + KV_LORA + QK_ROPE
    n_groups = 1 + 3 * MOBA_WIDTH // H_GROUP
    assert w_t.shape == (mla_cols + 3 * MOBA_WIDTH, k) and mla_cols <= H_GROUP
    assert mla_cols % SUBLANES == 0 and MOBA_WIDTH % H_GROUP == 0

    def wmap(i, j):
        pad_tiles = (H_GROUP - mla_cols) // SUBLANES
        return (j * (H_GROUP // SUBLANES) - pad_tiles * jnp.minimum(j, 1)) * SUBLANES, 0

    return pl.pallas_call(
        functools.partial(_inproj_kernel, mla_cols=mla_cols),
        out_shape=jax.ShapeDtypeStruct((m, n_groups * H_GROUP), BF16),
        grid=(m // tm, n_groups),
        in_specs=[pl.BlockSpec((tm, k), lambda i, j: (i, 0)),
                  pl.BlockSpec((pl.Element(H_GROUP), pl.Element(k)), wmap)],
        out_specs=pl.BlockSpec((tm, H_GROUP), lambda i, j: (i, j)),
        scratch_shapes=[pltpu.VMEM((tm, k), BF16)],
        compiler_params=_params("parallel", "arbitrary"),
        name="inproj",
    )(x, w_t)


def _rope_angle_kernel(pos_ref, invf_ref, cos_ref, sin_ref):
    ang = pos_ref[...].astype(F32) * invf_ref[...]
    cos_ref[...] = jnp.cos(ang)
    sin_ref[...] = jnp.sin(ang)


def _rope_angles(pos_col, *, tm=1024):
    s = pos_col.shape[0]
    lane = np.arange(LANES)
    moba_half, mla_half = MOBA_HEAD_DIM // 2, QK_ROPE // 2
    f_moba = ROPE_THETA ** (-(lane % moba_half).astype(np.float32) * 2.0 / MOBA_HEAD_DIM)
    f_mla = ROPE_THETA ** (-((lane - moba_half) % mla_half).astype(np.float32) * 2.0 / QK_ROPE)
    invf = np.where(lane < moba_half, f_moba, np.where(lane < moba_half + mla_half, f_mla, 0.0))
    invf = jnp.asarray(invf.astype(np.float32)).reshape(1, LANES)
    return pl.pallas_call(
        _rope_angle_kernel,
        out_shape=(jax.ShapeDtypeStruct((s, LANES), F32), jax.ShapeDtypeStruct((s, LANES), F32)),
        grid=(s // tm,),
        in_specs=[pl.BlockSpec((tm, 1), lambda i: (i, 0)), pl.BlockSpec((1, LANES), lambda i: (0, 0))],
        out_specs=(pl.BlockSpec((tm, LANES), lambda i: (i, 0)),
                   pl.BlockSpec((tm, LANES), lambda i: (i, 0))),
        compiler_params=_params("parallel"),
        name="rope_angles",
    )(pos_col, invf)


def _rope_tables_mla(cos_ref, sin_ref):
    c = cos_ref[...]
    s = sin_ref[...]
    lane = lax.broadcasted_iota(jnp.int32, c.shape, 1)
    half = QK_ROPE // 2
    lo = lane < half
    cos = jnp.where(lo, pltpu.roll(c, LANES // 2, 1), pltpu.roll(c, LANES - half, 1))
    sin_a = jnp.where(lo, -pltpu.roll(s, LANES // 2, 1), 0.0)
    sin_b = jnp.where((lane >= half) & (lane < QK_ROPE), pltpu.roll(s, LANES - half, 1), 0.0)
    return cos, sin_a, sin_b


def _rope_mla(t, cos, sin_a, sin_b):
    return (t * cos + pltpu.roll(t, LANES - QK_ROPE // 2, 1) * sin_a
            + pltpu.roll(t, QK_ROPE // 2, 1) * sin_b)


def _rope_tables_moba(cos_ref, sin_ref):
    c = cos_ref[...]
    s = sin_ref[...]
    lane = lax.broadcasted_iota(jnp.int32, c.shape, 1)
    lo = lane < MOBA_HEAD_DIM // 2
    half = MOBA_HEAD_DIM // 2
    return jnp.where(lo, c, pltpu.roll(c, half, 1)), jnp.where(lo, -s, pltpu.roll(s, half, 1))


def _rope_moba(t, cos, sin_signed):
    return t * cos + pltpu.roll(t, MOBA_HEAD_DIM // 2, 1) * sin_signed


def _mla_prep_kernel(h_ref, cos_ref, sin_ref, gq_ref, gkv_ref, wqn_ref, wqr_ref, wkn_ref, wv_ref,
                     q_ref, k_ref, v_ref):
    scale = (QK_NOPE + QK_ROPE) ** -0.5
    cos, sin_a, sin_b = _rope_tables_mla(cos_ref, sin_ref)
    hq = h_ref[:, 0:Q_LORA].astype(F32)
    hkv = h_ref[:, Q_LORA:Q_LORA + KV_LORA].astype(F32)
    hkr = h_ref[:, Q_LORA + KV_LORA:Q_LORA + KV_LORA + LANES].astype(F32)

    qn = _rms(hq, gq_ref[...]).astype(BF16)
    q_nope = jnp.dot(qn, wqn_ref[...], preferred_element_type=F32)
    q_rope = jnp.dot(qn, wqr_ref[...], preferred_element_type=F32)
    kvn = _rms(hkv, gkv_ref[...]).astype(BF16)
    k_nope = jnp.dot(kvn, wkn_ref[...], preferred_element_type=F32)
    v_ref[...] = jnp.dot(kvn, wv_ref[...], preferred_element_type=F32).astype(BF16)
    kpe = _rope_mla(hkr, cos, sin_a, sin_b).astype(BF16)
    for h in range(MLA_HEADS):
        lo = h * HEAD_SLOT
        hs = slice(h * LANES, (h + 1) * LANES)
        q_ref[:, lo:lo + LANES] = (q_nope[:, hs] * scale).astype(BF16)
        q_ref[:, lo + LANES:lo + HEAD_SLOT] = (
            _rope_mla(q_rope[:, hs], cos, sin_a, sin_b) * scale).astype(BF16)
        k_ref[:, lo:lo + LANES] = k_nope[:, hs].astype(BF16)
        k_ref[:, lo + LANES:lo + HEAD_SLOT] = kpe


def _mla_prep(h, cos, sin, gq, gkv, wqn, wqr, wkn, wv, *, tm=512):
    s = h.shape[0]
    full = lambda a: pl.BlockSpec(a.shape, lambda i: (0,) * a.ndim)
    return pl.pallas_call(
        _mla_prep_kernel,
        out_shape=(jax.ShapeDtypeStruct((s, MLA_HEADS * HEAD_SLOT), BF16),
                   jax.ShapeDtypeStruct((s, MLA_HEADS * HEAD_SLOT), BF16),
                   jax.ShapeDtypeStruct((s, MLA_WIDTH), BF16)),
        grid=(s // tm,),
        in_specs=[pl.BlockSpec((tm, 1024), lambda i: (i, 0)),
                  pl.BlockSpec((tm, LANES), lambda i: (i, 0)),
                  pl.BlockSpec((tm, LANES), lambda i: (i, 0)),
                  full(gq), full(gkv), full(wqn), full(wqr), full(wkn), full(wv)],
        out_specs=(pl.BlockSpec((tm, MLA_HEADS * HEAD_SLOT), lambda i: (i, 0)),
                   pl.BlockSpec((tm, MLA_HEADS * HEAD_SLOT), lambda i: (i, 0)),
                   pl.BlockSpec((tm, MLA_WIDTH), lambda i: (i, 0))),
        compiler_params=_params("parallel"),
        name="mla_prep",
    )(h, cos, sin, gq, gkv, wqn, wqr, wkn, wv)


def _moba_kprep_kernel(h_ref, cos_ref, sin_ref, k_ref, kbar_ref, *, rows):
    cos, sin_signed = _rope_tables_moba(cos_ref, sin_ref)
    row = pl.program_id(0) * rows + lax.broadcasted_iota(jnp.int32, (rows, LANES), 0)
    lane = lax.broadcasted_iota(jnp.int32, (rows, LANES), 1)
    onehot = (lane == row // MOBA_BLOCK).astype(BF16)
    for h in range(MOBA_HEADS):
        lo = h * HEAD_SLOT
        kr = _rope_moba(h_ref[:, h * LANES:(h + 1) * LANES].astype(F32), cos, sin_signed)
        k_ref[:, lo:lo + LANES] = kr.astype(BF16)
        k_ref[:, lo + LANES:lo + HEAD_SLOT] = onehot
        for b in range(rows // MOBA_BLOCK):
            kbar_ref[b:b + 1, h * LANES:(h + 1) * LANES] = jnp.mean(
                kr[b * MOBA_BLOCK:(b + 1) * MOBA_BLOCK], axis=0, keepdims=True)


def _moba_kprep(h, cos, sin, *, rows=2048):
    s = h.shape[0]
    return pl.pallas_call(
        functools.partial(_moba_kprep_kernel, rows=rows),
        out_shape=(jax.ShapeDtypeStruct((s, MOBA_HEADS * HEAD_SLOT), BF16),
                   jax.ShapeDtypeStruct((s // MOBA_BLOCK, MOBA_WIDTH), F32)),
        grid=(s // rows,),
        in_specs=[pl.BlockSpec((rows, MOBA_WIDTH), lambda i: (i, 2)),
                  pl.BlockSpec((rows, LANES), lambda i: (i, 0)),
                  pl.BlockSpec((rows, LANES), lambda i: (i, 0))],
        out_specs=(pl.BlockSpec((rows, MOBA_HEADS * HEAD_SLOT), lambda i: (i, 0)),
                   pl.BlockSpec((rows // MOBA_BLOCK, MOBA_WIDTH), lambda i: (i, 0))),
        compiler_params=_params("parallel"),
        name="moba_kprep",
    )(h, cos, sin)


def _top_mask(vals, row_idx, k):
    n = vals.shape[0]
    chosen = jnp.zeros(vals.shape, jnp.bool_)
    for _ in range(k):
        top = jnp.max(vals, axis=0, keepdims=True)
        first = jnp.min(jnp.where(vals == top, row_idx, n), axis=0, keepdims=True)
        pick = row_idx == first
        chosen = chosen | pick
        vals = jnp.where(pick, -jnp.inf, vals)
    return chosen


def _moba_qprep_kernel(h_ref, cos_ref, sin_ref, kbar_ref, q_ref, *, tm):
    scale = MOBA_HEAD_DIM ** -0.5
    cos, sin_signed = _rope_tables_moba(cos_ref, sin_ref)
    nb = N_MOBA_BLOCKS
    tok = pl.program_id(0) * tm + lax.broadcasted_iota(jnp.int32, (1, tm), 1)
    qblk = tok // MOBA_BLOCK
    blk = lax.broadcasted_iota(jnp.int32, (nb, 1), 0)
    past = blk < qblk
    own = blk == qblk
    for h in range(MOBA_HEADS):
        lo = h * HEAD_SLOT
        qr = _rope_moba(h_ref[:, h * LANES:(h + 1) * LANES].astype(F32), cos, sin_signed)
        gate = _nt_dot(kbar_ref[:, h * LANES:(h + 1) * LANES], qr,
                       precision=lax.Precision.HIGHEST)
        gate = jnp.where(past, gate, -jnp.inf)
        sel = past & _top_mask(gate, blk, MOBA_TOPK)
        bias = jnp.where(sel | own, 0.0, MASK_BIAS)
        bias = jnp.concatenate([bias, jnp.zeros((LANES - nb, tm), F32)], axis=0)
        q_ref[:, lo:lo + LANES] = (qr * scale).astype(BF16)
        q_ref[:, lo + LANES:lo + HEAD_SLOT] = bias.T.astype(BF16)


def _moba_qprep(h, cos, sin, kbar, *, tm=512):
    s = h.shape[0]
    return pl.pallas_call(
        functools.partial(_moba_qprep_kernel, tm=tm),
        out_shape=jax.ShapeDtypeStruct((s, MOBA_HEADS * HEAD_SLOT), BF16),
        grid=(s // tm,),
        in_specs=[pl.BlockSpec((tm, MOBA_WIDTH), lambda i: (i, 1)),
                  pl.BlockSpec((tm, LANES), lambda i: (i, 0)),
                  pl.BlockSpec((tm, LANES), lambda i: (i, 0)),
                  pl.BlockSpec(kbar.shape, lambda i: (0, 0))],
        out_specs=pl.BlockSpec((tm, MOBA_HEADS * HEAD_SLOT), lambda i: (i, 0)),
        compiler_params=_params("parallel"),
        name="moba_qprep",
    )(h, cos, sin, kbar)


def _flash_kernel(qa_ref, qb_ref, k_ref, v_ref, o_ref, *scratch, tq, nq, hp):
    p_idx = pl.program_id(1)
    nc = tq // LANES
    ones = jnp.ones((tq, LANES), BF16)
    heads = [scratch[3 * h:3 * h + 3] for h in range(hp)]
    for h, (q2_ref, m2_ref, acc2_ref) in enumerate(heads):
        q2_ref[0] = qa_ref[:, h * HEAD_SLOT:(h + 1) * HEAD_SLOT]
        q2_ref[1] = qb_ref[:, h * HEAD_SLOT:(h + 1) * HEAD_SLOT]
        m2_ref[...] = jnp.full(m2_ref.shape, -jnp.inf, F32)
        acc2_ref[...] = jnp.zeros(acc2_ref.shape, F32)

    def kv_rows(j):
        return pl.ds(pl.multiple_of(j * tq, tq), tq)

    def scores(h, sel, j):
        return _nt_dot(heads[h][0][sel], k_ref[kv_rows(j), h * HEAD_SLOT:(h + 1) * HEAD_SLOT])

    def consume(h, sel, j, s):
        _, m2_ref, acc2_ref = heads[h]
        v = v_ref[kv_rows(j), h * LANES:(h + 1) * LANES]
        chunks = [s[:, c * LANES:(c + 1) * LANES] for c in range(nc)]
        part = functools.reduce(jnp.maximum, chunks)
        m_old = m2_ref[sel]
        m_new = jnp.maximum(m_old, jnp.max(part, axis=1, keepdims=True))
        alpha = jnp.exp(m_old - m_new)
        p = jnp.concatenate([jnp.exp(ch - m_new).astype(BF16) for ch in chunks], axis=1)
        pv = jnp.dot(p, jnp.concatenate([v, ones], axis=1), preferred_element_type=F32)
        acc2_ref[sel] = jnp.concatenate([alpha, alpha], axis=1) * acc2_ref[sel] + pv
        m2_ref[sel] = m_new

    def diagonal(s):
        r = lax.broadcasted_iota(jnp.int32, s.shape, 0)
        c = lax.broadcasted_iota(jnp.int32, s.shape, 1)
        return jnp.where(c <= r, s, -jnp.inf)

    steps = []
    for t in range(nq - 1):
        sel = (t >= p_idx).astype(jnp.int32)
        steps += [(h, sel, t - sel * p_idx, False) for h in range(hp)]
    steps += [(h, 0, p_idx, True) for h in range(hp)]
    steps += [(h, 1, nq - 1 - p_idx, True) for h in range(hp)]

    ahead = hp
    pending = [scores(*st[:3]) for st in steps[:ahead]]
    for t, (h, sel, j, diag) in enumerate(steps):
        s = pending.pop(0)
        if t + ahead < len(steps):
            pending.append(scores(*steps[t + ahead][:3]))
        consume(h, sel, j, diagonal(s) if diag else s)
    for h, (_, _, acc2_ref) in enumerate(heads):
        for sel in range(2):
            o_ref[sel, :, h * LANES:(h + 1) * LANES] = (
                acc2_ref[sel, :, :LANES] / acc2_ref[sel, :, LANES:]).astype(o_ref.dtype)


def _flash(q, k, v, v_col0, n_heads, *, tq=512, hp=FLASH_HEADS_PER_STEP):
    s = q.shape[0]
    nq = s // tq
    assert n_heads % hp == 0 and v_col0 % hp == 0
    per_head = [pltpu.VMEM((2, tq, HEAD_SLOT), BF16), pltpu.VMEM((2, tq, LANES), F32),
                pltpu.VMEM((2, tq, 2 * LANES), F32)]
    return pl.pallas_call(
        functools.partial(_flash_kernel, tq=tq, nq=nq, hp=hp),
        out_shape=jax.ShapeDtypeStruct((2, s // 2, n_heads * LANES), BF16),
        grid=(n_heads // hp, nq // 2),
        in_specs=[pl.BlockSpec((tq, hp * HEAD_SLOT), lambda h, p: (p, h)),
                  pl.BlockSpec((tq, hp * HEAD_SLOT), lambda h, p: (nq - 1 - p, h)),
                  pl.BlockSpec((s, hp * HEAD_SLOT), lambda h, p: (0, h)),
                  pl.BlockSpec((s, hp * LANES), lambda h, p: (0, v_col0 // hp + h))],
        out_specs=pl.BlockSpec((2, tq, hp * LANES), lambda h, p: (0, p, h)),
        scratch_shapes=per_head * hp,
        compiler_params=_params("parallel", "parallel"),
        name="flash",
    )(q, q, k, v)


def _flash_row_block(i, tm, s, tq=512):
    per_tile = tq // tm
    tile = i // per_tile
    sub = i % per_tile
    nq = s // tq
    hi = (tile >= nq // 2).astype(jnp.int32)
    return hi, jnp.where(hi == 1, nq - 1 - tile, tile) * per_tile + sub


def _outproj_kernel(om_ref, ob_ref, gm_ref, gb_ref, wo_ref, x_ref, lg_ref, lb_ref, wrt_ref, br_ref,
                    x1_ref, idx_ref, pos_ref, gate_ref, cnt_ref):
    @pl.when(pl.program_id(0) == 0)
    def _():
        cnt_ref[...] = jnp.zeros(cnt_ref.shape, jnp.int32)

    a = jnp.concatenate([_rms(om_ref[...].astype(F32), gm_ref[...]).astype(BF16),
                         _rms(ob_ref[...].astype(F32), gb_ref[...]).astype(BF16)], axis=1)
    mixed = jnp.dot(a, wo_ref[...], preferred_element_type=F32)
    x1 = _layer_norm(DEEPNORM_ALPHA * x_ref[...] + mixed, lg_ref[...], lb_ref[...])
    x1_ref[...] = x1

    logits = _nt_dot(wrt_ref[...], x1, precision=lax.Precision.HIGHEST) + br_ref[...]
    tm = logits.shape[1]
    eidx = lax.broadcasted_iota(jnp.int32, (N_EXPERTS, 1), 0)
    rank = _topk_rank(logits, eidx)
    sel = rank < TOP_K
    mx = jnp.max(logits, axis=0, keepdims=True)
    p = jnp.where(sel, jnp.exp(logits - mx), 0.0)
    gates = p / jnp.sum(p, axis=0, keepdims=True)

    before = (lax.broadcasted_iota(jnp.int32, (tm, tm), 0)
              < lax.broadcasted_iota(jnp.int32, (tm, tm), 1)).astype(BF16)
    prefix = jnp.dot(sel.astype(BF16), before, preferred_element_type=F32).astype(jnp.int32)
    pos = cnt_ref[:, 0:1] + prefix
    cnt_ref[...] = cnt_ref[...] + jnp.sum(sel.astype(jnp.int32), axis=1, keepdims=True)

    pick = lambda vals, k, zero: jnp.sum(jnp.where(rank == k, vals, zero), axis=0, keepdims=True)
    pad_i = [jnp.zeros((8 - TOP_K, tm), jnp.int32)]
    idx_ref[...] = jnp.concatenate([pick(eidx, k, 0) for k in range(TOP_K)] + pad_i, axis=0)
    pos_ref[...] = jnp.concatenate([pick(pos, k, 0) for k in range(TOP_K)] + pad_i, axis=0)
    gate_rows = jnp.concatenate([pick(gates, k, 0.0) for k in range(TOP_K)]
                                + [jnp.zeros((LANES - TOP_K, tm), F32)], axis=0)
    gate_ref[...] = gate_rows.T


def _outproj(o_mla, o_moba, g_mla, g_moba, wo, x, ln_g, ln_b, wr_t, b_r, *, tm=512):
    s = x.shape[0]
    full = lambda a: pl.BlockSpec(a.shape, lambda i: (0,) * a.ndim)
    return pl.pallas_call(
        _outproj_kernel,
        out_shape=(jax.ShapeDtypeStruct((s, D_MODEL), F32),
                   jax.ShapeDtypeStruct((8, s), jnp.int32),
                   jax.ShapeDtypeStruct((8, s), jnp.int32),
                   jax.ShapeDtypeStruct((s, LANES), F32),
                   jax.ShapeDtypeStruct((N_EXPERTS, LANES), jnp.int32)),
        grid=(s // tm,),
        in_specs=[pl.BlockSpec((None, tm, MLA_WIDTH), lambda i: (*_flash_row_block(i, tm, s), 0)),
                  pl.BlockSpec((None, tm, MOBA_WIDTH), lambda i: (*_flash_row_block(i, tm, s), 0)),
                  full(g_mla), full(g_moba), full(wo),
                  pl.BlockSpec((tm, D_MODEL), lambda i: (i, 0)),
                  full(ln_g), full(ln_b), full(wr_t), full(b_r)],
        out_specs=(pl.BlockSpec((tm, D_MODEL), lambda i: (i, 0)),
                   pl.BlockSpec((8, tm), lambda i: (0, i)),
                   pl.BlockSpec((8, tm), lambda i: (0, i)),
                   pl.BlockSpec((tm, LANES), lambda i: (i, 0)),
                   pl.BlockSpec((N_EXPERTS, LANES), lambda i: (0, 0))),
        compiler_params=_params("arbitrary"),
        name="outproj_router",
    )(o_mla, o_moba, g_mla, g_moba, wo, x, ln_g, ln_b, wr_t, b_r)


def _ffn_kernel(ie_ref, nb_ref, tab_hbm, x1_hbm, wg_ref, wu_ref, bg_ref, bu_ref, wd_ref, bd_ref,
                y_hbm, tab_smem, xf_ref, xb_ref, y_ref, sem_tab, sem_in, sem_out,
                *, nf, n_tok, max_blocks):
    w = pl.program_id(0)
    s = pl.program_id(1)
    n_work = pl.num_programs(0)
    nb = nb_ref[w]
    nb_prev = nb_ref[jnp.maximum(w - 1, 0)]
    w_next = jnp.minimum(w + 1, n_work - 1)
    has_next = (w + 1 < n_work) & (nb_ref[w_next] > 0)
    sub = SUBLANES
    tiles_per_block = MOE_BLOCK // sub
    slot = lax.bitwise_and(w, 1)

    tab_len = tab_smem.shape[0] // 2

    def load_table(item, slot_):
        dst = tab_smem.at[pl.ds(pl.multiple_of(slot_ * tab_len, tab_len), tab_len)]
        cp = pltpu.make_async_copy(tab_hbm.at[item], dst, sem_tab)
        cp.start()
        cp.wait()

    def for_rows(n_blocks, slot_, fn):
        def body(b, carry):
            base = slot_ * tab_len + b * MOE_BLOCK
            for i in range(MOE_BLOCK):
                fn(b * tiles_per_block + i // sub, i % sub, tab_smem[base + i])
            return carry
        lax.fori_loop(0, n_blocks, body, 0)

    def wait_tiles(n_blocks, make_copy):
        def body(g, carry):
            make_copy().wait()
            return carry
        lax.fori_loop(0, n_blocks * tiles_per_block, body, 0)

    def hbm_row(ref, r):
        return ref.at[lax.shift_right_logical(r, 3), pl.ds(lax.bitwise_and(r, sub - 1), 1)]

    def start_gather(n_blocks, slot_):
        for_rows(n_blocks, slot_, lambda g, j, e: pltpu.make_async_copy(
            hbm_row(x1_hbm, e), xf_ref.at[g, pl.ds(j, 1)],
            sem_in).start())

    def out_copy(b):
        tiles = pl.ds(b * tiles_per_block, tiles_per_block)
        dst = pl.ds(w * (MOE_ROWS // sub) + b * tiles_per_block, tiles_per_block)
        return pltpu.make_async_copy(y_ref.at[tiles], y_hbm.at[dst], sem_out)

    def wait_out(n_blocks):
        def body(b, carry):
            out_copy(0).wait()
            return carry
        lax.fori_loop(0, n_blocks, body, 0)

    @pl.when((nb > 0) & (s == 0))
    def _():
        @pl.when(w == 0)
        def _():
            load_table(0, 0)
            start_gather(nb, 0)

        wait_tiles(nb, lambda: pltpu.make_async_copy(x1_hbm.at[0], xf_ref.at[0], sem_in))

        def cast(b, carry):
            tiles = pl.ds(b * tiles_per_block, tiles_per_block)
            rows = pl.ds(pl.multiple_of(b * MOE_BLOCK, MOE_BLOCK), MOE_BLOCK)
            xb_ref[rows, :] = xf_ref[tiles].reshape(MOE_BLOCK, D_MODEL).astype(BF16)
            return carry

        lax.fori_loop(0, nb, cast, 0)

        @pl.when(w > 0)
        def _():
            wait_out(nb_prev)

        bias = jnp.broadcast_to(bd_ref[...], (tiles_per_block, sub, D_MODEL))

        def init(b, carry):
            y_ref[pl.ds(b * tiles_per_block, tiles_per_block)] = bias
            return carry

        lax.fori_loop(0, nb, init, 0)

    def step(m):
        x = xb_ref[0:m, :]
        g = jnp.dot(x, wg_ref[...].astype(BF16), preferred_element_type=F32) + bg_ref[...]
        u = jnp.dot(x, wu_ref[...].astype(BF16), preferred_element_type=F32) + bu_ref[...]
        g = jnp.minimum(g, SWIGLU_LIMIT)
        u = jnp.clip(u, -SWIGLU_LIMIT, SWIGLU_LIMIT)
        act = ((u + 1.0) * (g * (1.0 / (1.0 + jnp.exp(-SWIGLU_ALPHA * g))))).astype(BF16)
        down = jnp.dot(act, wd_ref[...].astype(BF16), preferred_element_type=F32)
        y_ref[0:m // sub] = y_ref[0:m // sub] + down.reshape(m // sub, sub, D_MODEL)

    for k in range(1, max_blocks + 1):
        pl.when(nb == k)(functools.partial(step, k * MOE_BLOCK))

    @pl.when((nb > 0) & (s == nf - 1))
    def _():
        def start_out(b, carry):
            out_copy(b).start()
            return carry

        lax.fori_loop(0, nb, start_out, 0)

        @pl.when(has_next)
        def _():
            load_table(w_next, 1 - slot)
            start_gather(nb_ref[w_next], 1 - slot)

        @pl.when(jnp.logical_not(has_next))
        def _():
            wait_out(nb)


def _ffn(item_e, item_nb, tab, x1, w_gate_up, b_gate_up, w_down, b_down, *, tf=256):
    n_work, tab_len = tab.shape
    n_tok = x1.shape[0]
    nf = D_FF // tf
    assert n_tok & (n_tok - 1) == 0 and MOE_ROWS <= n_tok and MOE_ROWS % MOE_BLOCK == 0
    assert n_tok % SUBLANES == 0 and tab_len >= MOE_ROWS

    def tile(w, s, nbr):
        return jnp.where(nbr[w] > 0, s, nf - 1)

    def colmap(off):
        return lambda w, s, ie, nbr: (ie[w], 0, off + tile(w, s, nbr))

    def rowmap(w, s, ie, nbr):
        return ie[w], tile(w, s, nbr), 0

    grid_spec = pltpu.PrefetchScalarGridSpec(
        num_scalar_prefetch=2,
        grid=(n_work, nf),
        in_specs=[
            pl.BlockSpec(memory_space=pl.ANY),
            pl.BlockSpec(memory_space=pl.ANY),
            pl.BlockSpec((None, D_MODEL, tf), colmap(0)),
            pl.BlockSpec((None, D_MODEL, tf), colmap(nf)),
            pl.BlockSpec((None, 1, tf), colmap(0)),
            pl.BlockSpec((None, 1, tf), colmap(nf)),
            pl.BlockSpec((None, tf, D_MODEL), rowmap),
            pl.BlockSpec((None, 1, D_MODEL), lambda w, s, ie, nbr: (ie[w], 0, 0)),
        ],
        out_specs=pl.BlockSpec(memory_space=pl.ANY),
        scratch_shapes=[pltpu.SMEM((2 * tab_len,), jnp.int32),
                        pltpu.VMEM((MOE_ROWS // SUBLANES, SUBLANES, D_MODEL), F32),
                        pltpu.VMEM((MOE_ROWS, D_MODEL), BF16),
                        pltpu.VMEM((MOE_ROWS // SUBLANES, SUBLANES, D_MODEL), F32),
                        pltpu.SemaphoreType.DMA, pltpu.SemaphoreType.DMA, pltpu.SemaphoreType.DMA],
    )
    return pl.pallas_call(
        functools.partial(_ffn_kernel, nf=nf, n_tok=n_tok, max_blocks=MOE_ROWS // MOE_BLOCK),
        out_shape=jax.ShapeDtypeStruct((n_work * MOE_ROWS // SUBLANES, SUBLANES, D_MODEL), F32),
        grid_spec=grid_spec,
        compiler_params=_params("arbitrary", "arbitrary"),
        name="experts",
    )(item_e, item_nb, tab, x1.reshape(n_tok // SUBLANES, SUBLANES, D_MODEL), w_gate_up, w_gate_up,
      b_gate_up, b_gate_up, w_down, b_down)


def _final_kernel(yrow_hbm, y_hbm, x1_ref, gate_ref, g_ref, b_ref, o_ref,
                  tab_smem, ybuf_ref, sem_tab, sem_rows, *, tm):
    i = pl.program_id(0)
    n = pl.num_programs(0)
    slot = lax.bitwise_and(i, 1)
    n_rows = TOP_K * tm
    sub = SUBLANES

    def fetch(tile, slot_):
        dst = tab_smem.at[pl.ds(pl.multiple_of(slot_ * n_rows, n_rows), n_rows)]
        cp = pltpu.make_async_copy(yrow_hbm.at[tile], dst, sem_tab)
        cp.start()
        cp.wait()
        base = slot_ * n_rows
        for j in range(n_rows):
            r = tab_smem[base + j]
            pltpu.make_async_copy(
                y_hbm.at[lax.shift_right_logical(r, 3), pl.ds(lax.bitwise_and(r, sub - 1), 1)],
                ybuf_ref.at[slot_, j // sub, pl.ds(j % sub, 1)], sem_rows.at[slot_]).start()

    @pl.when(i == 0)
    def _():
        fetch(0, 0)

    @pl.when(i + 1 < n)
    def _():
        fetch(i + 1, 1 - slot)

    pltpu.make_async_copy(y_hbm.at[pl.ds(0, n_rows // sub)], ybuf_ref.at[slot],
                          sem_rows.at[slot]).wait()
    ffn = None
    for k in range(TOP_K):
        yk = ybuf_ref[slot, k * tm // sub:(k + 1) * tm // sub].reshape(tm, D_MODEL)
        term = gate_ref[:, k:k + 1] * yk
        ffn = term if ffn is None else ffn + term
    o_ref[...] = _layer_norm(DEEPNORM_ALPHA * x1_ref[...] + ffn, g_ref[...], b_ref[...])


def _final(yrow, y, x1, gates, g, b, *, tm=COMBINE_ROWS):
    s = x1.shape[0]
    n_rows = TOP_K * tm
    return pl.pallas_call(
        functools.partial(_final_kernel, tm=tm),
        out_shape=jax.ShapeDtypeStruct((s, D_MODEL), F32),
        grid=(s // tm,),
        in_specs=[pl.BlockSpec(memory_space=pl.ANY),
                  pl.BlockSpec(memory_space=pl.ANY),
                  pl.BlockSpec((tm, D_MODEL), lambda i: (i, 0)),
                  pl.BlockSpec((tm, LANES), lambda i: (i, 0)),
                  pl.BlockSpec((1, D_MODEL), lambda i: (0, 0)),
                  pl.BlockSpec((1, D_MODEL), lambda i: (0, 0))],
        out_specs=pl.BlockSpec((tm, D_MODEL), lambda i: (i, 0)),
        scratch_shapes=[pltpu.SMEM((2 * n_rows,), jnp.int32),
                        pltpu.VMEM((2, n_rows // SUBLANES, SUBLANES, D_MODEL), F32),
                        pltpu.SemaphoreType.DMA, pltpu.SemaphoreType.DMA((2,))],
        compiler_params=_params("arbitrary"),
        name="combine_final_ln",
    )(yrow, y, x1, gates, g, b)


def _layer(x, pos, w_in, g_q_a, w_q_b, g_kv_a, w_kv_b, g_out_mla, g_out_moba, w_o, ln1_g, ln1_b,
           w_router, b_router, w_gate_up, b_gate_up, w_down, b_down, ln2_g, ln2_b):
    s = x.shape[0]
    wq = w_q_b.reshape(Q_LORA, MLA_HEADS, QK_NOPE + QK_ROPE)
    wqn = wq[:, :, :QK_NOPE].reshape(Q_LORA, MLA_HEADS * LANES).astype(BF16)
    wqr = jnp.pad(wq[:, :, QK_NOPE:], ((0, 0), (0, 0), (0, LANES - QK_ROPE))).reshape(
        Q_LORA, MLA_HEADS * LANES).astype(BF16)
    wkv = w_kv_b.reshape(KV_LORA, MLA_HEADS, QK_NOPE + V_HEAD)
    wkn = wkv[:, :, :QK_NOPE].reshape(KV_LORA, MLA_HEADS * LANES).astype(BF16)
    wv = wkv[:, :, QK_NOPE:].reshape(KV_LORA, MLA_WIDTH).astype(BF16)
    row = lambda a: a.reshape(1, -1)

    cos, sin = _rope_angles(pos.reshape(s, 1))
    h = _inproj(x, w_in.T)
    q_mla, k_mla, v_mla = _mla_prep(h, cos, sin, row(g_q_a), row(g_kv_a), wqn, wqr, wkn, wv)
    k_moba, kbar = _moba_kprep(h, cos, sin)
    q_moba = _moba_qprep(h, cos, sin, kbar)
    o_mla = _flash(q_mla, k_mla, v_mla, 0, MLA_HEADS)
    o_moba = _flash(q_moba, k_moba, h, 3 * MOBA_WIDTH // LANES, MOBA_HEADS)

    x1, idx_t, pos_t, gates, cnt = _outproj(o_mla, o_moba, row(g_out_mla), row(g_out_moba),
                                            w_o.astype(BF16), x, row(ln1_g), row(ln1_b),
                                            w_router.T, b_router.reshape(N_EXPERTS, 1))

    n_work = -(-s * TOP_K // MOE_ROWS) + N_EXPERTS
    counts = cnt[:, 0]
    items_per_e = (counts + MOE_ROWS - 1) // MOE_ROWS
    item_end = jnp.cumsum(items_per_e)
    item_start = item_end - items_per_e
    w_idx = jnp.arange(n_work)
    item_e = jnp.minimum(jnp.sum(w_idx[:, None] >= item_end[None, :], axis=1), N_EXPERTS - 1)
    item_rows = jnp.clip(counts[item_e] - (w_idx - item_start[item_e]) * MOE_ROWS, 0, MOE_ROWS)
    item_nb = jnp.where(w_idx < item_end[-1], (item_rows + MOE_BLOCK - 1) // MOE_BLOCK, 0)
    e_k = idx_t[:TOP_K]
    p_k = pos_t[:TOP_K]
    start_k = jnp.sum(jnp.where(e_k[..., None] == jnp.arange(N_EXPERTS), item_start, 0), axis=-1)
    item_row = ((start_k + p_k // MOE_ROWS) * MOE_ROWS + p_k % MOE_ROWS).astype(jnp.int32)
    tok = jnp.broadcast_to(jnp.arange(s, dtype=jnp.int32), (TOP_K, s))
    filler = jnp.arange(n_work * MOE_ROWS, dtype=jnp.int32) % MOE_ROWS
    tab = filler.at[item_row.reshape(-1)].set(tok.reshape(-1))
    tab = jnp.pad(tab.reshape(n_work, MOE_ROWS), ((0, 0), (0, MOE_TAB_LEN - MOE_ROWS)))

    y = _ffn(item_e.astype(jnp.int32), item_nb.astype(jnp.int32), tab, x1, w_gate_up,
             b_gate_up.reshape(N_EXPERTS, 1, 2 * D_FF), w_down, b_down.reshape(N_EXPERTS, 1, D_MODEL))
    tm = COMBINE_ROWS
    yrow = item_row.reshape(TOP_K, s // tm, tm).transpose(1, 0, 2).reshape(s // tm, TOP_K * tm)
    return _final(yrow, y, x1, gates, row(ln2_g), row(ln2_b))


def kernel(x, positions, w_in, g_q_a, w_q_b, g_kv_a, w_kv_b, g_out_mla, g_out_moba, w_o, ln1_g, ln1_b,
           w_router, b_router, w_gate_up, b_gate_up, w_down, b_down, ln2_g, ln2_b):
    b, s, d = x.shape
    assert b == 1 and s == SEQ and d == D_MODEL and w_in.shape[0] == DEPTH
    hcur = x[0]
    for l in range(DEPTH):
        hcur = _layer(hcur, positions[0], w_in[l], g_q_a[l], w_q_b[l], g_kv_a[l], w_kv_b[l],
                      g_out_mla[l], g_out_moba[l], w_o[l], ln1_g[l], ln1_b[l], w_router[l],
                      b_router[l], w_gate_up[l], b_gate_up[l], w_down[l], b_down[l], ln2_g[l],
                      ln2_b[l])
    return hcur[None]
```

```python
import functools

import numpy as np
import jax
import jax.numpy as jnp
from jax import lax
from jax.experimental import pallas as pl
from jax.experimental.pallas import tpu as pltpu

D_MODEL = 2048
SEQ = 8192
MLA_HEADS = 8
QK_NOPE = 128
QK_ROPE = 64
V_HEAD = 128
Q_LORA = 512
KV_LORA = 256
MLA_WIDTH = MLA_HEADS * V_HEAD
MOBA_HEADS = 8
MOBA_HEAD_DIM = 128
MOBA_WIDTH = MOBA_HEADS * MOBA_HEAD_DIM
MOBA_BLOCK = 256
MOBA_TOPK = 3
N_MOBA_BLOCKS = SEQ // MOBA_BLOCK
ROPE_THETA = 10000.0
N_EXPERTS = 32
TOP_K = 4
D_FF = D_MODEL
SWIGLU_LIMIT = 7.0
SWIGLU_ALPHA = 1.702
MOE_BLOCK = 256
MOE_ROWS = 1536
DEPTH = 1
DEEPNORM_ALPHA = float((2 * DEPTH) ** 0.25)
RMS_EPS = 1e-6
LN_EPS = 1e-5

LANES = 128
SUBLANES = 8
MOE_TAB_LEN = 2048
COMBINE_ROWS = 256
FLASH_HEADS_PER_STEP = 1
HEAD_SLOT = 2 * LANES
H_GROUP = 1024
MASK_BIAS = -float(2 ** 17)
V7X_VMEM_LIMIT = 56 * 1024 * 1024

F32 = jnp.float32
BF16 = jnp.bfloat16


def _params(*semantics):
    return pltpu.CompilerParams(dimension_semantics=semantics, vmem_limit_bytes=V7X_VMEM_LIMIT)


def _rms(xf, g):
    return xf * lax.rsqrt(jnp.mean(xf * xf, axis=-1, keepdims=True) + RMS_EPS) * g


def _layer_norm(xf, g, b):
    mu = jnp.mean(xf, axis=-1, keepdims=True)
    xc = xf - mu
    var = jnp.mean(xc * xc, axis=-1, keepdims=True)
    return xc * lax.rsqrt(var + LN_EPS) * g + b


def _nt_dot(a, b, **kw):
    return lax.dot_general(a, b, (((1,), (1,)), ((), ())), preferred_element_type=F32, **kw)


def _topk_rank(vals, row_idx):
    n = vals.shape[0]
    rank = jnp.zeros(vals.shape, jnp.int32)
    for jp in range(n):
        vj = vals[jp:jp + 1, :]
        ahead = (vj > vals) | ((vj == vals) & (jp < row_idx))
        rank = rank + ahead.astype(jnp.int32)
    return rank


def _inproj_kernel(x_ref, wt_ref, o_ref, xb_ref, *, mla_cols):
    j = pl.program_id(1)

    @pl.when(j == 0)
    def _():
        xb_ref[...] = x_ref[...].astype(BF16)

    h = _nt_dot(xb_ref[...], wt_ref[...].astype(BF16))
    col = lax.broadcasted_iota(jnp.int32, h.shape, 1)
    o_ref[...] = jnp.where((j > 0) | (col < mla_cols), h, 0.0).astype(o_ref.dtype)


def _inproj(x, w_t, *, tm=1024):
    m, k = x.shape
    mla_cols = Q_LORA + KV_LORA + QK_ROPE
    n_groups = 1 + 3 * MOBA_WIDTH // H_GROUP
    assert w_t.shape == (mla_cols + 3 * MOBA_WIDTH, k) and mla_cols <= H_GROUP
    assert mla_cols % SUBLANES == 0 and MOBA_WIDTH % H_GROUP == 0

    def wmap(i, j):
        pad_tiles = (H_GROUP - mla_cols) // SUBLANES
        return (j * (H_GROUP // SUBLANES) - pad_tiles * jnp.minimum(j, 1)) * SUBLANES, 0

    return pl.pallas_call(
        functools.partial(_inproj_kernel, mla_cols=mla_cols),
        out_shape=jax.ShapeDtypeStruct((m, n_groups * H_GROUP), BF16),
        grid=(m // tm, n_groups),
        in_specs=[pl.BlockSpec((tm, k), lambda i, j: (i, 0)),
                  pl.BlockSpec((pl.Element(H_GROUP), pl.Element(k)), wmap)],
        out_specs=pl.BlockSpec((tm, H_GROUP), lambda i, j: (i, j)),
        scratch_shapes=[pltpu.VMEM((tm, k), BF16)],
        compiler_params=_params("parallel", "arbitrary"),
        name="inproj",
    )(x, w_t)


def _rope_angle_kernel(pos_ref, invf_ref, cos_ref, sin_ref):
    ang = pos_ref[...].astype(F32) * invf_ref[...]
    cos_ref[...] = jnp.cos(ang)
    sin_ref[...] = jnp.sin(ang)


def _rope_angles(pos_col, *, tm=1024):
    s = pos_col.shape[0]
    lane = np.arange(LANES)
    moba_half, mla_half = MOBA_HEAD_DIM // 2, QK_ROPE // 2
    f_moba = ROPE_THETA ** (-(lane % moba_half).astype(np.float32) * 2.0 / MOBA_HEAD_DIM)
    f_mla = ROPE_THETA ** (-((lane - moba_half) % mla_half).astype(np.float32) * 2.0 / QK_ROPE)
    invf = np.where(lane < moba_half, f_moba, np.where(lane < moba_half + mla_half, f_mla, 0.0))
    invf = jnp.asarray(invf.astype(np.float32)).reshape(1, LANES)
    return pl.pallas_call(
        _rope_angle_kernel,
        out_shape=(jax.ShapeDtypeStruct((s, LANES), F32), jax.ShapeDtypeStruct((s, LANES), F32)),
        grid=(s // tm,),
        in_specs=[pl.BlockSpec((tm, 1), lambda i: (i, 0)), pl.BlockSpec((1, LANES), lambda i: (0, 0))],
        out_specs=(pl.BlockSpec((tm, LANES), lambda i: (i, 0)),
                   pl.BlockSpec((tm, LANES), lambda i: (i, 0))),
        compiler_params=_params("parallel"),
        name="rope_angles",
    )(pos_col, invf)


def _rope_tables_mla(cos_ref, sin_ref):
    c = cos_ref[...]
    s = sin_ref[...]
    lane = lax.broadcasted_iota(jnp.int32, c.shape, 1)
    half = QK_ROPE // 2
    lo = lane < half
    cos = jnp.where(lo, pltpu.roll(c, LANES // 2, 1), pltpu.roll(c, LANES - half, 1))
    sin_a = jnp.where(lo, -pltpu.roll(s, LANES // 2, 1), 0.0)
    sin_b = jnp.where((lane >= half) & (lane < QK_ROPE), pltpu.roll(s, LANES - half, 1), 0.0)
    return cos, sin_a, sin_b


def _rope_mla(t, cos, sin_a, sin_b):
    return (t * cos + pltpu.roll(t, LANES - QK_ROPE // 2, 1) * sin_a
            + pltpu.roll(t, QK_ROPE // 2, 1) * sin_b)


def _rope_tables_moba(cos_ref, sin_ref):
    c = cos_ref[...]
    s = sin_ref[...]
    lane = lax.broadcasted_iota(jnp.int32, c.shape, 1)
    lo = lane < MOBA_HEAD_DIM // 2
    half = MOBA_HEAD_DIM // 2
    return jnp.where(lo, c, pltpu.roll(c, half, 1)), jnp.where(lo, -s, pltpu.roll(s, half, 1))


def _rope_moba(t, cos, sin_signed):
    return t * cos + pltpu.roll(t, MOBA_HEAD_DIM // 2, 1) * sin_signed


def _mla_prep_kernel(h_ref, cos_ref, sin_ref, gq_ref, gkv_ref, wqn_ref, wqr_ref, wkn_ref, wv_ref,
                     q_ref, k_ref, v_ref):
    scale = (QK_NOPE + QK_ROPE) ** -0.5
    cos, sin_a, sin_b = _rope_tables_mla(cos_ref, sin_ref)
    hq = h_ref[:, 0:Q_LORA].astype(F32)
    hkv = h_ref[:, Q_LORA:Q_LORA + KV_LORA].astype(F32)
    hkr = h_ref[:, Q_LORA + KV_LORA:Q_LORA + KV_LORA + LANES].astype(F32)

    qn = _rms(hq, gq_ref[...]).astype(BF16)
    q_nope = jnp.dot(qn, wqn_ref[...], preferred_element_type=F32)
    q_rope = jnp.dot(qn, wqr_ref[...], preferred_element_type=F32)
    kvn = _rms(hkv, gkv_ref[...]).astype(BF16)
    k_nope = jnp.dot(kvn, wkn_ref[...], preferred_element_type=F32)
    v = jnp.dot(kvn, wv_ref[...], preferred_element_type=F32).astype(BF16)
    kpe = _rope_mla(hkr, cos, sin_a, sin_b).astype(BF16)
    for h in range(MLA_HEADS):
        lo = h * HEAD_SLOT
        hs = slice(h * LANES, (h + 1) * LANES)
        q_ref[:, lo:lo + LANES] = (q_nope[:, hs] * scale).astype(BF16)
        q_ref[:, lo + LANES:lo + HEAD_SLOT] = (
            _rope_mla(q_rope[:, hs], cos, sin_a, sin_b) * scale).astype(BF16)
        k_ref[h, :, 0:LANES] = k_nope[:, hs].astype(BF16)
        k_ref[h, :, LANES:HEAD_SLOT] = kpe
        v_ref[h] = v[:, hs]


def _mla_prep(h, cos, sin, gq, gkv, wqn, wqr, wkn, wv, *, tm=512):
    s = h.shape[0]
    full = lambda a: pl.BlockSpec(a.shape, lambda i: (0,) * a.ndim)
    return pl.pallas_call(
        _mla_prep_kernel,
        out_shape=(jax.ShapeDtypeStruct((s, MLA_HEADS * HEAD_SLOT), BF16),
                   jax.ShapeDtypeStruct((MLA_HEADS, s, HEAD_SLOT), BF16),
                   jax.ShapeDtypeStruct((MLA_HEADS, s, V_HEAD), BF16)),
        grid=(s // tm,),
        in_specs=[pl.BlockSpec((tm, 1024), lambda i: (i, 0)),
                  pl.BlockSpec((tm, LANES), lambda i: (i, 0)),
                  pl.BlockSpec((tm, LANES), lambda i: (i, 0)),
                  full(gq), full(gkv), full(wqn), full(wqr), full(wkn), full(wv)],
        out_specs=(pl.BlockSpec((tm, MLA_HEADS * HEAD_SLOT), lambda i: (i, 0)),
                   pl.BlockSpec((MLA_HEADS, tm, HEAD_SLOT), lambda i: (0, i, 0)),
                   pl.BlockSpec((MLA_HEADS, tm, V_HEAD), lambda i: (0, i, 0))),
        compiler_params=_params("parallel"),
        name="mla_prep",
    )(h, cos, sin, gq, gkv, wqn, wqr, wkn, wv)


def _moba_kprep_kernel(h_ref, hv_ref, cos_ref, sin_ref, k_ref, v_ref, kbar_ref, *, rows):
    cos, sin_signed = _rope_tables_moba(cos_ref, sin_ref)
    row = pl.program_id(0) * rows + lax.broadcasted_iota(jnp.int32, (rows, LANES), 0)
    lane = lax.broadcasted_iota(jnp.int32, (rows, LANES), 1)
    onehot = (lane == row // MOBA_BLOCK).astype(BF16)
    for h in range(MOBA_HEADS):
        kr = _rope_moba(h_ref[:, h * LANES:(h + 1) * LANES].astype(F32), cos, sin_signed)
        k_ref[h, :, 0:LANES] = kr.astype(BF16)
        k_ref[h, :, LANES:HEAD_SLOT] = onehot
        v_ref[h] = hv_ref[:, h * LANES:(h + 1) * LANES]
        for b in range(rows // MOBA_BLOCK):
            kbar_ref[b:b + 1, h * LANES:(h + 1) * LANES] = jnp.mean(
                kr[b * MOBA_BLOCK:(b + 1) * MOBA_BLOCK], axis=0, keepdims=True)


def _moba_kprep(h, cos, sin, *, rows=2048):
    s = h.shape[0]
    return pl.pallas_call(
        functools.partial(_moba_kprep_kernel, rows=rows),
        out_shape=(jax.ShapeDtypeStruct((MOBA_HEADS, s, HEAD_SLOT), BF16),
                   jax.ShapeDtypeStruct((MOBA_HEADS, s, MOBA_HEAD_DIM), BF16),
                   jax.ShapeDtypeStruct((s // MOBA_BLOCK, MOBA_WIDTH), F32)),
        grid=(s // rows,),
        in_specs=[pl.BlockSpec((rows, MOBA_WIDTH), lambda i: (i, 2)),
                  pl.BlockSpec((rows, MOBA_WIDTH), lambda i: (i, 3)),
                  pl.BlockSpec((rows, LANES), lambda i: (i, 0)),
                  pl.BlockSpec((rows, LANES), lambda i: (i, 0))],
        out_specs=(pl.BlockSpec((MOBA_HEADS, rows, HEAD_SLOT), lambda i: (0, i, 0)),
                   pl.BlockSpec((MOBA_HEADS, rows, MOBA_HEAD_DIM), lambda i: (0, i, 0)),
                   pl.BlockSpec((rows // MOBA_BLOCK, MOBA_WIDTH), lambda i: (i, 0))),
        compiler_params=_params("parallel"),
        name="moba_kprep",
    )(h, h, cos, sin)


def _top_mask(vals, row_idx, k):
    n = vals.shape[0]
    chosen = jnp.zeros(vals.shape, jnp.bool_)
    for _ in range(k):
        top = jnp.max(vals, axis=0, keepdims=True)
        first = jnp.min(jnp.where(vals == top, row_idx, n), axis=0, keepdims=True)
        pick = row_idx == first
        chosen = chosen | pick
        vals = jnp.where(pick, -jnp.inf, vals)
    return chosen


def _moba_qprep_kernel(h_ref, cos_ref, sin_ref, kbar_ref, q_ref, *, tm):
    scale = MOBA_HEAD_DIM ** -0.5
    cos, sin_signed = _rope_tables_moba(cos_ref, sin_ref)
    nb = N_MOBA_BLOCKS
    tok = pl.program_id(0) * tm + lax.broadcasted_iota(jnp.int32, (1, tm), 1)
    qblk = tok // MOBA_BLOCK
    blk = lax.broadcasted_iota(jnp.int32, (nb, 1), 0)
    past = blk < qblk
    own = blk == qblk
    for h in range(MOBA_HEADS):
        lo = h * HEAD_SLOT
        qr = _rope_moba(h_ref[:, h * LANES:(h + 1) * LANES].astype(F32), cos, sin_signed)
        gate = _nt_dot(kbar_ref[:, h * LANES:(h + 1) * LANES], qr,
                       precision=lax.Precision.HIGHEST)
        gate = jnp.where(past, gate, -jnp.inf)
        sel = past & _top_mask(gate, blk, MOBA_TOPK)
        bias = jnp.where(sel | own, 0.0, MASK_BIAS)
        bias = jnp.concatenate([bias, jnp.zeros((LANES - nb, tm), F32)], axis=0)
        q_ref[:, lo:lo + LANES] = (qr * scale).astype(BF16)
        q_ref[:, lo + LANES:lo + HEAD_SLOT] = bias.T.astype(BF16)


def _moba_qprep(h, cos, sin, kbar, *, tm=512):
    s = h.shape[0]
    return pl.pallas_call(
        functools.partial(_moba_qprep_kernel, tm=tm),
        out_shape=jax.ShapeDtypeStruct((s, MOBA_HEADS * HEAD_SLOT), BF16),
        grid=(s // tm,),
        in_specs=[pl.BlockSpec((tm, MOBA_WIDTH), lambda i: (i, 1)),
                  pl.BlockSpec((tm, LANES), lambda i: (i, 0)),
                  pl.BlockSpec((tm, LANES), lambda i: (i, 0)),
                  pl.BlockSpec(kbar.shape, lambda i: (0, 0))],
        out_specs=pl.BlockSpec((tm, MOBA_HEADS * HEAD_SLOT), lambda i: (i, 0)),
        compiler_params=_params("parallel"),
        name="moba_qprep",
    )(h, cos, sin, kbar)


def _flash_kernel(qa_ref, qb_ref, k_ref, v_ref, o_ref, *scratch, tq, nq, hp):
    p_idx = pl.program_id(1)
    nc = tq // LANES
    ones = jnp.ones((tq, LANES), BF16)
    heads = [scratch[3 * h:3 * h + 3] for h in range(hp)]
    for h, (q2_ref, m2_ref, acc2_ref) in enumerate(heads):
        q2_ref[0] = qa_ref[:, h * HEAD_SLOT:(h + 1) * HEAD_SLOT]
        q2_ref[1] = qb_ref[:, h * HEAD_SLOT:(h + 1) * HEAD_SLOT]
        m2_ref[...] = jnp.full(m2_ref.shape, -jnp.inf, F32)
        acc2_ref[...] = jnp.zeros(acc2_ref.shape, F32)

    def kv_rows(j):
        return pl.ds(pl.multiple_of(j * tq, tq), tq)

    def scores(h, sel, j):
        return _nt_dot(heads[h][0][sel], k_ref[h, kv_rows(j), :])

    def consume(h, sel, j, s):
        _, m2_ref, acc2_ref = heads[h]
        v = v_ref[h, kv_rows(j), :]
        chunks = [s[:, c * LANES:(c + 1) * LANES] for c in range(nc)]
        part = functools.reduce(jnp.maximum, chunks)
        m_old = m2_ref[sel]
        m_new = jnp.maximum(m_old, jnp.max(part, axis=1, keepdims=True))
        alpha = jnp.exp(m_old - m_new)
        p = jnp.concatenate([jnp.exp(ch - m_new).astype(BF16) for ch in chunks], axis=1)
        pv = jnp.dot(p, jnp.concatenate([v, ones], axis=1), preferred_element_type=F32)
        acc2_ref[sel] = jnp.concatenate([alpha, alpha], axis=1) * acc2_ref[sel] + pv
        m2_ref[sel] = m_new

    def diagonal(s):
        r = lax.broadcasted_iota(jnp.int32, s.shape, 0)
        c = lax.broadcasted_iota(jnp.int32, s.shape, 1)
        return jnp.where(c <= r, s, -jnp.inf)

    steps = []
    for t in range(nq - 1):
        sel = (t >= p_idx).astype(jnp.int32)
        steps += [(h, sel, t - sel * p_idx, False) for h in range(hp)]
    steps += [(h, 0, p_idx, True) for h in range(hp)]
    steps += [(h, 1, nq - 1 - p_idx, True) for h in range(hp)]

    ahead = hp
    pending = [scores(*st[:3]) for st in steps[:ahead]]
    for t, (h, sel, j, diag) in enumerate(steps):
        s = pending.pop(0)
        if t + ahead < len(steps):
            pending.append(scores(*steps[t + ahead][:3]))
        consume(h, sel, j, diagonal(s) if diag else s)
    for h, (_, _, acc2_ref) in enumerate(heads):
        for sel in range(2):
            o_ref[sel, :, h * LANES:(h + 1) * LANES] = (
                acc2_ref[sel, :, :LANES] / acc2_ref[sel, :, LANES:]).astype(o_ref.dtype)


def _flash(q, k, v, *, tq=512, hp=FLASH_HEADS_PER_STEP):
    s = q.shape[0]
    n_heads = k.shape[0]
    nq = s // tq
    assert n_heads % hp == 0
    per_head = [pltpu.VMEM((2, tq, HEAD_SLOT), BF16), pltpu.VMEM((2, tq, LANES), F32),
                pltpu.VMEM((2, tq, 2 * LANES), F32)]
    return pl.pallas_call(
        functools.partial(_flash_kernel, tq=tq, nq=nq, hp=hp),
        out_shape=jax.ShapeDtypeStruct((2, s // 2, n_heads * LANES), BF16),
        grid=(n_heads // hp, nq // 2),
        in_specs=[pl.BlockSpec((tq, hp * HEAD_SLOT), lambda h, p: (p, h)),
                  pl.BlockSpec((tq, hp * HEAD_SLOT), lambda h, p: (nq - 1 - p, h)),
                  pl.BlockSpec((hp, s, HEAD_SLOT), lambda h, p: (h, 0, 0)),
                  pl.BlockSpec((hp, s, LANES), lambda h, p: (h, 0, 0))],
        out_specs=pl.BlockSpec((2, tq, hp * LANES), lambda h, p: (0, p, h)),
        scratch_shapes=per_head * hp,
        compiler_params=_params("parallel", "parallel"),
        name="flash",
    )(q, q, k, v)


def _flash_row_block(i, tm, s, tq=512):
    per_tile = tq // tm
    tile = i // per_tile
    sub = i % per_tile
    nq = s // tq
    hi = (tile >= nq // 2).astype(jnp.int32)
    return hi, jnp.where(hi == 1, nq - 1 - tile, tile) * per_tile + sub


def _outproj_kernel(om_ref, ob_ref, gm_ref, gb_ref, wo_ref, x_ref, lg_ref, lb_ref, wrt_ref, br_ref,
                    x1_ref, idx_ref, pos_ref, gate_ref, cnt_ref):
    @pl.when(pl.program_id(0) == 0)
    def _():
        cnt_ref[...] = jnp.zeros(cnt_ref.shape, jnp.int32)

    a = jnp.concatenate([_rms(om_ref[...].astype(F32), gm_ref[...]).astype(BF16),
                         _rms(ob_ref[...].astype(F32), gb_ref[...]).astype(BF16)], axis=1)
    mixed = jnp.dot(a, wo_ref[...], preferred_element_type=F32)
    x1 = _layer_norm(DEEPNORM_ALPHA * x_ref[...] + mixed, lg_ref[...], lb_ref[...])
    x1_ref[...] = x1

    logits = _nt_dot(wrt_ref[...], x1, precision=lax.Precision.HIGHEST) + br_ref[...]
    tm = logits.shape[1]
    eidx = lax.broadcasted_iota(jnp.int32, (N_EXPERTS, 1), 0)
    rank = _topk_rank(logits, eidx)
    sel = rank < TOP_K
    mx = jnp.max(logits, axis=0, keepdims=True)
    p = jnp.where(sel, jnp.exp(logits - mx), 0.0)
    gates = p / jnp.sum(p, axis=0, keepdims=True)

    before = (lax.broadcasted_iota(jnp.int32, (tm, tm), 0)
              < lax.broadcasted_iota(jnp.int32, (tm, tm), 1)).astype(BF16)
    prefix = jnp.dot(sel.astype(BF16), before, preferred_element_type=F32).astype(jnp.int32)
    pos = cnt_ref[:, 0:1] + prefix
    cnt_ref[...] = cnt_ref[...] + jnp.sum(sel.astype(jnp.int32), axis=1, keepdims=True)

    pick = lambda vals, k, zero: jnp.sum(jnp.where(rank == k, vals, zero), axis=0, keepdims=True)
    pad_i = [jnp.zeros((8 - TOP_K, tm), jnp.int32)]
    idx_ref[...] = jnp.concatenate([pick(eidx, k, 0) for k in range(TOP_K)] + pad_i, axis=0)
    pos_ref[...] = jnp.concatenate([pick(pos, k, 0) for k in range(TOP_K)] + pad_i, axis=0)
    gate_rows = jnp.concatenate([pick(gates, k, 0.0) for k in range(TOP_K)]
                                + [jnp.zeros((LANES - TOP_K, tm), F32)], axis=0)
    gate_ref[...] = gate_rows.T


def _outproj(o_mla, o_moba, g_mla, g_moba, wo, x, ln_g, ln_b, wr_t, b_r, *, tm=512):
    s = x.shape[0]
    full = lambda a: pl.BlockSpec(a.shape, lambda i: (0,) * a.ndim)
    return pl.pallas_call(
        _outproj_kernel,
        out_shape=(jax.ShapeDtypeStruct((s, D_MODEL), F32),
                   jax.ShapeDtypeStruct((8, s), jnp.int32),
                   jax.ShapeDtypeStruct((8, s), jnp.int32),
                   jax.ShapeDtypeStruct((s, LANES), F32),
                   jax.ShapeDtypeStruct((N_EXPERTS, LANES), jnp.int32)),
        grid=(s // tm,),
        in_specs=[pl.BlockSpec((None, tm, MLA_WIDTH), lambda i: (*_flash_row_block(i, tm, s), 0)),
                  pl.BlockSpec((None, tm, MOBA_WIDTH), lambda i: (*_flash_row_block(i, tm, s), 0)),
                  full(g_mla), full(g_moba), full(wo),
                  pl.BlockSpec((tm, D_MODEL), lambda i: (i, 0)),
                  full(ln_g), full(ln_b), full(wr_t), full(b_r)],
        out_specs=(pl.BlockSpec((tm, D_MODEL), lambda i: (i, 0)),
                   pl.BlockSpec((8, tm), lambda i: (0, i)),
                   pl.BlockSpec((8, tm), lambda i: (0, i)),
                   pl.BlockSpec((tm, LANES), lambda i: (i, 0)),
                   pl.BlockSpec((N_EXPERTS, LANES), lambda i: (0, 0))),
        compiler_params=_params("arbitrary"),
        name="outproj_router",
    )(o_mla, o_moba, g_mla, g_moba, wo, x, ln_g, ln_b, wr_t, b_r)


def _ffn_kernel(ie_ref, nb_ref, tab_hbm, x1_hbm, wg_ref, wu_ref, bg_ref, bu_ref, wd_ref, bd_ref,
                y_hbm, tab_smem, xf_ref, xb_ref, y_ref, sem_tab, sem_in, sem_out,
                *, nf, n_tok, max_blocks):
    w = pl.program_id(0)
    s = pl.program_id(1)
    n_work = pl.num_programs(0)
    nb = nb_ref[w]
    nb_prev = nb_ref[jnp.maximum(w - 1, 0)]
    w_next = jnp.minimum(w + 1, n_work - 1)
    has_next = (w + 1 < n_work) & (nb_ref[w_next] > 0)
    sub = SUBLANES
    tiles_per_block = MOE_BLOCK // sub
    slot = lax.bitwise_and(w, 1)

    tab_len = tab_smem.shape[0] // 2

    def load_table(item, slot_):
        dst = tab_smem.at[pl.ds(pl.multiple_of(slot_ * tab_len, tab_len), tab_len)]
        cp = pltpu.make_async_copy(tab_hbm.at[item], dst, sem_tab)
        cp.start()
        cp.wait()

    def for_rows(n_blocks, slot_, fn):
        def body(b, carry):
            base = slot_ * tab_len + b * MOE_BLOCK
            for i in range(MOE_BLOCK):
                fn(b * tiles_per_block + i // sub, i % sub, tab_smem[base + i])
            return carry
        lax.fori_loop(0, n_blocks, body, 0)

    def wait_tiles(n_blocks, make_copy):
        def body(g, carry):
            make_copy().wait()
            return carry
        lax.fori_loop(0, n_blocks * tiles_per_block, body, 0)

    def hbm_row(ref, r):
        return ref.at[lax.shift_right_logical(r, 3), pl.ds(lax.bitwise_and(r, sub - 1), 1)]

    def start_gather(n_blocks, slot_):
        for_rows(n_blocks, slot_, lambda g, j, e: pltpu.make_async_copy(
            hbm_row(x1_hbm, e), xf_ref.at[g, pl.ds(j, 1)],
            sem_in).start())

    def out_copy(b):
        tiles = pl.ds(b * tiles_per_block, tiles_per_block)
        dst = pl.ds(w * (MOE_ROWS // sub) + b * tiles_per_block, tiles_per_block)
        return pltpu.make_async_copy(y_ref.at[tiles], y_hbm.at[dst], sem_out)

    def wait_out(n_blocks):
        def body(b, carry):
            out_copy(0).wait()
            return carry
        lax.fori_loop(0, n_blocks, body, 0)

    @pl.when((nb > 0) & (s == 0))
    def _():
        @pl.when(w == 0)
        def _():
            load_table(0, 0)
            start_gather(nb, 0)

        wait_tiles(nb, lambda: pltpu.make_async_copy(x1_hbm.at[0], xf_ref.at[0], sem_in))

        def cast(b, carry):
            tiles = pl.ds(b * tiles_per_block, tiles_per_block)
            rows = pl.ds(pl.multiple_of(b * MOE_BLOCK, MOE_BLOCK), MOE_BLOCK)
            xb_ref[rows, :] = xf_ref[tiles].reshape(MOE_BLOCK, D_MODEL).astype(BF16)
            return carry

        lax.fori_loop(0, nb, cast, 0)

        @pl.when(w > 0)
        def _():
            wait_out(nb_prev)

        bias = jnp.broadcast_to(bd_ref[...], (tiles_per_block, sub, D_MODEL))

        def init(b, carry):
            y_ref[pl.ds(b * tiles_per_block, tiles_per_block)] = bias
            return carry

        lax.fori_loop(0, nb, init, 0)

    def step(m):
        x = xb_ref[0:m, :]
        g = jnp.dot(x, wg_ref[...].astype(BF16), preferred_element_type=F32) + bg_ref[...]
        u = jnp.dot(x, wu_ref[...].astype(BF16), preferred_element_type=F32) + bu_ref[...]
        g = jnp.minimum(g, SWIGLU_LIMIT)
        u = jnp.clip(u, -SWIGLU_LIMIT, SWIGLU_LIMIT)
        act = ((u + 1.0) * (g * (1.0 / (1.0 + jnp.exp(-SWIGLU_ALPHA * g))))).astype(BF16)
        down = jnp.dot(act, wd_ref[...].astype(BF16), preferred_element_type=F32)
        y_ref[0:m // sub] = y_ref[0:m // sub] + down.reshape(m // sub, sub, D_MODEL)

    for k in range(1, max_blocks + 1):
        pl.when(nb == k)(functools.partial(step, k * MOE_BLOCK))

    @pl.when((nb > 0) & (s == nf - 1))
    def _():
        def start_out(b, carry):
            out_copy(b).start()
            return carry

        lax.fori_loop(0, nb, start_out, 0)

        @pl.when(has_next)
        def _():
            load_table(w_next, 1 - slot)
            start_gather(nb_ref[w_next], 1 - slot)

        @pl.when(jnp.logical_not(has_next))
        def _():
            wait_out(nb)


def _ffn(item_e, item_nb, tab, x1, w_gate_up, b_gate_up, w_down, b_down, *, tf=256):
    n_work, tab_len = tab.shape
    n_tok = x1.shape[0]
    nf = D_FF // tf
    assert n_tok & (n_tok - 1) == 0 and MOE_ROWS <= n_tok and MOE_ROWS % MOE_BLOCK == 0
    assert n_tok % SUBLANES == 0 and tab_len >= MOE_ROWS

    def tile(w, s, nbr):
        return jnp.where(nbr[w] > 0, s, nf - 1)

    def colmap(off):
        return lambda w, s, ie, nbr: (ie[w], 0, off + tile(w, s, nbr))

    def rowmap(w, s, ie, nbr):
        return ie[w], tile(w, s, nbr), 0

    grid_spec = pltpu.PrefetchScalarGridSpec(
        num_scalar_prefetch=2,
        grid=(n_work, nf),
        in_specs=[
            pl.BlockSpec(memory_space=pl.ANY),
            pl.BlockSpec(memory_space=pl.ANY),
            pl.BlockSpec((None, D_MODEL, tf), colmap(0)),
            pl.BlockSpec((None, D_MODEL, tf), colmap(nf)),
            pl.BlockSpec((None, 1, tf), colmap(0)),
            pl.BlockSpec((None, 1, tf), colmap(nf)),
            pl.BlockSpec((None, tf, D_MODEL), rowmap),
            pl.BlockSpec((None, 1, D_MODEL), lambda w, s, ie, nbr: (ie[w], 0, 0)),
        ],
        out_specs=pl.BlockSpec(memory_space=pl.ANY),
        scratch_shapes=[pltpu.SMEM((2 * tab_len,), jnp.int32),
                        pltpu.VMEM((MOE_ROWS // SUBLANES, SUBLANES, D_MODEL), F32),
                        pltpu.VMEM((MOE_ROWS, D_MODEL), BF16),
                        pltpu.VMEM((MOE_ROWS // SUBLANES, SUBLANES, D_MODEL), F32),
                        pltpu.SemaphoreType.DMA, pltpu.SemaphoreType.DMA, pltpu.SemaphoreType.DMA],
    )
    return pl.pallas_call(
        functools.partial(_ffn_kernel, nf=nf, n_tok=n_tok, max_blocks=MOE_ROWS // MOE_BLOCK),
        out_shape=jax.ShapeDtypeStruct((n_work * MOE_ROWS // SUBLANES, SUBLANES, D_MODEL), F32),
        grid_spec=grid_spec,
        compiler_params=_params("arbitrary", "arbitrary"),
        name="experts",
    )(item_e, item_nb, tab, x1.reshape(n_tok // SUBLANES, SUBLANES, D_MODEL), w_gate_up, w_gate_up,
      b_gate_up, b_gate_up, w_down, b_down)


def _final_kernel(yrow_hbm, y_hbm, x1_ref, gate_ref, g_ref, b_ref, o_ref,
                  tab_smem, ybuf_ref, sem_tab, sem_rows, *, tm):
    i = pl.program_id(0)
    n = pl.num_programs(0)
    slot = lax.bitwise_and(i, 1)
    n_rows = TOP_K * tm
    sub = SUBLANES

    def fetch(tile, slot_):
        dst = tab_smem.at[pl.ds(pl.multiple_of(slot_ * n_rows, n_rows), n_rows)]
        cp = pltpu.make_async_copy(yrow_hbm.at[tile], dst, sem_tab)
        cp.start()
        cp.wait()
        base = slot_ * n_rows
        for j in range(n_rows):
            r = tab_smem[base + j]
            pltpu.make_async_copy(
                y_hbm.at[lax.shift_right_logical(r, 3), pl.ds(lax.bitwise_and(r, sub - 1), 1)],
                ybuf_ref.at[slot_, j // sub, pl.ds(j % sub, 1)], sem_rows.at[slot_]).start()

    @pl.when(i == 0)
    def _():
        fetch(0, 0)

    @pl.when(i + 1 < n)
    def _():
        fetch(i + 1, 1 - slot)

    pltpu.make_async_copy(y_hbm.at[pl.ds(0, n_rows // sub)], ybuf_ref.at[slot],
                          sem_rows.at[slot]).wait()
    ffn = None
    for k in range(TOP_K):
        yk = ybuf_ref[slot, k * tm // sub:(k + 1) * tm // sub].reshape(tm, D_MODEL)
        term = gate_ref[:, k:k + 1] * yk
        ffn = term if ffn is None else ffn + term
    o_ref[...] = _layer_norm(DEEPNORM_ALPHA * x1_ref[...] + ffn, g_ref[...], b_ref[...])


def _final(yrow, y, x1, gates, g, b, *, tm=COMBINE_ROWS):
    s = x1.shape[0]
    n_rows = TOP_K * tm
    return pl.pallas_call(
        functools.partial(_final_kernel, tm=tm),
        out_shape=jax.ShapeDtypeStruct((s, D_MODEL), F32),
        grid=(s // tm,),
        in_specs=[pl.BlockSpec(memory_space=pl.ANY),
                  pl.BlockSpec(memory_space=pl.ANY),
                  pl.BlockSpec((tm, D_MODEL), lambda i: (i, 0)),
                  pl.BlockSpec((tm, LANES), lambda i: (i, 0)),
                  pl.BlockSpec((1, D_MODEL), lambda i: (0, 0)),
                  pl.BlockSpec((1, D_MODEL), lambda i: (0, 0))],
        out_specs=pl.BlockSpec((tm, D_MODEL), lambda i: (i, 0)),
        scratch_shapes=[pltpu.SMEM((2 * n_rows,), jnp.int32),
                        pltpu.VMEM((2, n_rows // SUBLANES, SUBLANES, D_MODEL), F32),
                        pltpu.SemaphoreType.DMA, pltpu.SemaphoreType.DMA((2,))],
        compiler_params=_params("arbitrary"),
        name="combine_final_ln",
    )(yrow, y, x1, gates, g, b)


def _layer(x, pos, w_in, g_q_a, w_q_b, g_kv_a, w_kv_b, g_out_mla, g_out_moba, w_o, ln1_g, ln1_b,
           w_router, b_router, w_gate_up, b_gate_up, w_down, b_down, ln2_g, ln2_b):
    s = x.shape[0]
    wq = w_q_b.reshape(Q_LORA, MLA_HEADS, QK_NOPE + QK_ROPE)
    wqn = wq[:, :, :QK_NOPE].reshape(Q_LORA, MLA_HEADS * LANES).astype(BF16)
    wqr = jnp.pad(wq[:, :, QK_NOPE:], ((0, 0), (0, 0), (0, LANES - QK_ROPE))).reshape(
        Q_LORA, MLA_HEADS * LANES).astype(BF16)
    wkv = w_kv_b.reshape(KV_LORA, MLA_HEADS, QK_NOPE + V_HEAD)
    wkn = wkv[:, :, :QK_NOPE].reshape(KV_LORA, MLA_HEADS * LANES).astype(BF16)
    wv = wkv[:, :, QK_NOPE:].reshape(KV_LORA, MLA_WIDTH).astype(BF16)
    row = lambda a: a.reshape(1, -1)

    cos, sin = _rope_angles(pos.reshape(s, 1))
    h = _inproj(x, w_in.T)
    q_mla, k_mla, v_mla = _mla_prep(h, cos, sin, row(g_q_a), row(g_kv_a), wqn, wqr, wkn, wv)
    k_moba, v_moba, kbar = _moba_kprep(h, cos, sin)
    q_moba = _moba_qprep(h, cos, sin, kbar)
    o_mla = _flash(q_mla, k_mla, v_mla)
    o_moba = _flash(q_moba, k_moba, v_moba)

    x1, idx_t, pos_t, gates, cnt = _outproj(o_mla, o_moba, row(g_out_mla), row(g_out_moba),
                                            w_o.astype(BF16), x, row(ln1_g), row(ln1_b),
                                            w_router.T, b_router.reshape(N_EXPERTS, 1))

    n_work = -(-s * TOP_K // MOE_ROWS) + N_EXPERTS
    counts = cnt[:, 0]
    items_per_e = (counts + MOE_ROWS - 1) // MOE_ROWS
    item_end = jnp.cumsum(items_per_e)
    item_start = item_end - items_per_e
    w_idx = jnp.arange(n_work)
    item_e = jnp.minimum(jnp.sum(w_idx[:, None] >= item_end[None, :], axis=1), N_EXPERTS - 1)
    item_rows = jnp.clip(counts[item_e] - (w_idx - item_start[item_e]) * MOE_ROWS, 0, MOE_ROWS)
    item_nb = jnp.where(w_idx < item_end[-1], (item_rows + MOE_BLOCK - 1) // MOE_BLOCK, 0)
    e_k = idx_t[:TOP_K]
    p_k = pos_t[:TOP_K]
    start_k = jnp.sum(jnp.where(e_k[..., None] == jnp.arange(N_EXPERTS), item_start, 0), axis=-1)
    item_row = ((start_k + p_k // MOE_ROWS) * MOE_ROWS + p_k % MOE_ROWS).astype(jnp.int32)
    tok = jnp.broadcast_to(jnp.arange(s, dtype=jnp.int32), (TOP_K, s))
    filler = jnp.arange(n_work * MOE_ROWS, dtype=jnp.int32) % MOE_ROWS
    tab = filler.at[item_row.reshape(-1)].set(tok.reshape(-1))
    tab = jnp.pad(tab.reshape(n_work, MOE_ROWS), ((0, 0), (0, MOE_TAB_LEN - MOE_ROWS)))

    y = _ffn(item_e.astype(jnp.int32), item_nb.astype(jnp.int32), tab, x1, w_gate_up,
             b_gate_up.reshape(N_EXPERTS, 1, 2 * D_FF), w_down, b_down.reshape(N_EXPERTS, 1, D_MODEL))
    tm = COMBINE_ROWS
    yrow = item_row.reshape(TOP_K, s // tm, tm).transpose(1, 0, 2).reshape(s // tm, TOP_K * tm)
    return _final(yrow, y, x1, gates, row(ln2_g), row(ln2_b))


def kernel(x, positions, w_in, g_q_a, w_q_b, g_kv_a, w_kv_b, g_out_mla, g_out_moba, w_o, ln1_g, ln1_b,
           w_router, b_router, w_gate_up, b_gate_up, w_down, b_down, ln2_g, ln2_b):
    b, s, d = x.shape
    assert b == 1 and s == SEQ and d == D_MODEL and w_in.shape[0] == DEPTH
    hcur = x[0]
    for l in range(DEPTH):
        hcur = _layer(hcur, positions[0], w_in[l], g_q_a[l], w_q_b[l], g_kv_a[l], w_kv_b[l],
                      g_out_mla[l], g_out_moba[l], w_o[l], ln1_g[l], ln1_b[l], w_router[l],
                      b_router[l], w_gate_up[l], b_gate_up[l], w_down[l], b_down[l], ln2_g[l],
                      ln2_b[l])
    return hcur[None]
```

```python
import functools

import numpy as np
import jax
import jax.numpy as jnp
from jax import lax
from jax.experimental import pallas as pl
from jax.experimental.pallas import tpu as pltpu

D_MODEL = 2048
SEQ = 8192
MLA_HEADS = 8
QK_NOPE = 128
QK_ROPE = 64
V_HEAD = 128
Q_LORA = 512
KV_LORA = 256
MLA_WIDTH = MLA_HEADS * V_HEAD
MOBA_HEADS = 8
MOBA_HEAD_DIM = 128
MOBA_WIDTH = MOBA_HEADS * MOBA_HEAD_DIM
MOBA_BLOCK = 256
MOBA_TOPK = 3
N_MOBA_BLOCKS = SEQ // MOBA_BLOCK
ROPE_THETA = 10000.0
N_EXPERTS = 32
TOP_K = 4
D_FF = D_MODEL
SWIGLU_LIMIT = 7.0
SWIGLU_ALPHA = 1.702
MOE_BLOCK = 256
MOE_ROWS = 1536
DEPTH = 1
DEEPNORM_ALPHA = float((2 * DEPTH) ** 0.25)
RMS_EPS = 1e-6
LN_EPS = 1e-5

LANES = 128
SUBLANES = 8
MOE_TAB_LEN = 2048
COMBINE_ROWS = 256
FLASH_HEADS_PER_STEP = 2
HEAD_SLOT = 2 * LANES
H_GROUP = 1024
MASK_BIAS = -float(2 ** 17)
V7X_VMEM_LIMIT = 56 * 1024 * 1024

F32 = jnp.float32
BF16 = jnp.bfloat16


def _params(*semantics):
    return pltpu.CompilerParams(dimension_semantics=semantics, vmem_limit_bytes=V7X_VMEM_LIMIT)


def _rms(xf, g):
    return xf * lax.rsqrt(jnp.mean(xf * xf, axis=-1, keepdims=True) + RMS_EPS) * g


def _layer_norm(xf, g, b):
    mu = jnp.mean(xf, axis=-1, keepdims=True)
    xc = xf - mu
    var = jnp.mean(xc * xc, axis=-1, keepdims=True)
    return xc * lax.rsqrt(var + LN_EPS) * g + b


def _nt_dot(a, b, **kw):
    return lax.dot_general(a, b, (((1,), (1,)), ((), ())), preferred_element_type=F32, **kw)


def _topk_rank(vals, row_idx):
    n = vals.shape[0]
    rank = jnp.zeros(vals.shape, jnp.int32)
    for jp in range(n):
        vj = vals[jp:jp + 1, :]
        ahead = (vj > vals) | ((vj == vals) & (jp < row_idx))
        rank = rank + ahead.astype(jnp.int32)
    return rank


def _inproj_kernel(x_ref, wt_ref, o_ref, xb_ref, *, mla_cols):
    j = pl.program_id(1)

    @pl.when(j == 0)
    def _():
        xb_ref[...] = x_ref[...].astype(BF16)

    h = _nt_dot(xb_ref[...], wt_ref[...].astype(BF16))
    col = lax.broadcasted_iota(jnp.int32, h.shape, 1)
    o_ref[...] = jnp.where((j > 0) | (col < mla_cols), h, 0.0).astype(o_ref.dtype)


def _inproj(x, w_t, *, tm=1024):
    m, k = x.shape
    mla_cols = Q_LORA + KV_LORA + QK_ROPE
    n_groups = 1 + 3 * MOBA_WIDTH // H_GROUP
    assert w_t.shape == (mla_cols + 3 * MOBA_WIDTH, k) and mla_cols <= H_GROUP
    assert mla_cols % SUBLANES == 0 and MOBA_WIDTH % H_GROUP == 0

    def wmap(i, j):
        pad_tiles = (H_GROUP - mla_cols) // SUBLANES
        return (j * (H_GROUP // SUBLANES) - pad_tiles * jnp.minimum(j, 1)) * SUBLANES, 0

    return pl.pallas_call(
        functools.partial(_inproj_kernel, mla_cols=mla_cols),
        out_shape=jax.ShapeDtypeStruct((m, n_groups * H_GROUP), BF16),
        grid=(m // tm, n_groups),
        in_specs=[pl.BlockSpec((tm, k), lambda i, j: (i, 0)),
                  pl.BlockSpec((pl.Element(H_GROUP), pl.Element(k)), wmap)],
        out_specs=pl.BlockSpec((tm, H_GROUP), lambda i, j: (i, j)),
        scratch_shapes=[pltpu.VMEM((tm, k), BF16)],
        compiler_params=_params("parallel", "arbitrary"),
        name="inproj",
    )(x, w_t)


def _rope_angle_kernel(pos_ref, invf_ref, cos_ref, sin_ref):
    ang = pos_ref[...].astype(F32) * invf_ref[...]
    cos_ref[...] = jnp.cos(ang)
    sin_ref[...] = jnp.sin(ang)


def _rope_angles(pos_col, *, tm=1024):
    s = pos_col.shape[0]
    lane = np.arange(LANES)
    moba_half, mla_half = MOBA_HEAD_DIM // 2, QK_ROPE // 2
    f_moba = ROPE_THETA ** (-(lane % moba_half).astype(np.float32) * 2.0 / MOBA_HEAD_DIM)
    f_mla = ROPE_THETA ** (-((lane - moba_half) % mla_half).astype(np.float32) * 2.0 / QK_ROPE)
    invf = np.where(lane < moba_half, f_moba, np.where(lane < moba_half + mla_half, f_mla, 0.0))
    invf = jnp.asarray(invf.astype(np.float32)).reshape(1, LANES)
    return pl.pallas_call(
        _rope_angle_kernel,
        out_shape=(jax.ShapeDtypeStruct((s, LANES), F32), jax.ShapeDtypeStruct((s, LANES), F32)),
        grid=(s // tm,),
        in_specs=[pl.BlockSpec((tm, 1), lambda i: (i, 0)), pl.BlockSpec((1, LANES), lambda i: (0, 0))],
        out_specs=(pl.BlockSpec((tm, LANES), lambda i: (i, 0)),
                   pl.BlockSpec((tm, LANES), lambda i: (i, 0))),
        compiler_params=_params("parallel"),
        name="rope_angles",
    )(pos_col, invf)


def _rope_tables_mla(cos_ref, sin_ref):
    c = cos_ref[...]
    s = sin_ref[...]
    lane = lax.broadcasted_iota(jnp.int32, c.shape, 1)
    half = QK_ROPE // 2
    lo = lane < half
    cos = jnp.where(lo, pltpu.roll(c, LANES // 2, 1), pltpu.roll(c, LANES - half, 1))
    sin_a = jnp.where(lo, -pltpu.roll(s, LANES // 2, 1), 0.0)
    sin_b = jnp.where((lane >= half) & (lane < QK_ROPE), pltpu.roll(s, LANES - half, 1), 0.0)
    return cos, sin_a, sin_b


def _rope_mla(t, cos, sin_a, sin_b):
    return (t * cos + pltpu.roll(t, LANES - QK_ROPE // 2, 1) * sin_a
            + pltpu.roll(t, QK_ROPE // 2, 1) * sin_b)


def _rope_tables_moba(cos_ref, sin_ref):
    c = cos_ref[...]
    s = sin_ref[...]
    lane = lax.broadcasted_iota(jnp.int32, c.shape, 1)
    lo = lane < MOBA_HEAD_DIM // 2
    half = MOBA_HEAD_DIM // 2
    return jnp.where(lo, c, pltpu.roll(c, half, 1)), jnp.where(lo, -s, pltpu.roll(s, half, 1))


def _rope_moba(t, cos, sin_signed):
    return t * cos + pltpu.roll(t, MOBA_HEAD_DIM // 2, 1) * sin_signed


def _mla_prep_kernel(h_ref, cos_ref, sin_ref, gq_ref, gkv_ref, wqn_ref, wqr_ref, wkn_ref, wv_ref,
                     q_ref, k_ref, v_ref):
    scale = (QK_NOPE + QK_ROPE) ** -0.5
    cos, sin_a, sin_b = _rope_tables_mla(cos_ref, sin_ref)
    hq = h_ref[:, 0:Q_LORA].astype(F32)
    hkv = h_ref[:, Q_LORA:Q_LORA + KV_LORA].astype(F32)
    hkr = h_ref[:, Q_LORA + KV_LORA:Q_LORA + KV_LORA + LANES].astype(F32)

    qn = _rms(hq, gq_ref[...]).astype(BF16)
    q_nope = jnp.dot(qn, wqn_ref[...], preferred_element_type=F32)
    q_rope = jnp.dot(qn, wqr_ref[...], preferred_element_type=F32)
    kvn = _rms(hkv, gkv_ref[...]).astype(BF16)
    k_nope = jnp.dot(kvn, wkn_ref[...], preferred_element_type=F32)
    v = jnp.dot(kvn, wv_ref[...], preferred_element_type=F32).astype(BF16)
    kpe = _rope_mla(hkr, cos, sin_a, sin_b).astype(BF16)
    for h in range(MLA_HEADS):
        lo = h * HEAD_SLOT
        hs = slice(h * LANES, (h + 1) * LANES)
        q_ref[:, lo:lo + LANES] = (q_nope[:, hs] * scale).astype(BF16)
        q_ref[:, lo + LANES:lo + HEAD_SLOT] = (
            _rope_mla(q_rope[:, hs], cos, sin_a, sin_b) * scale).astype(BF16)
        k_ref[h, :, 0:LANES] = k_nope[:, hs].astype(BF16)
        k_ref[h, :, LANES:HEAD_SLOT] = kpe
        v_ref[h] = v[:, hs]


def _mla_prep(h, cos, sin, gq, gkv, wqn, wqr, wkn, wv, *, tm=512):
    s = h.shape[0]
    full = lambda a: pl.BlockSpec(a.shape, lambda i: (0,) * a.ndim)
    return pl.pallas_call(
        _mla_prep_kernel,
        out_shape=(jax.ShapeDtypeStruct((s, MLA_HEADS * HEAD_SLOT), BF16),
                   jax.ShapeDtypeStruct((MLA_HEADS, s, HEAD_SLOT), BF16),
                   jax.ShapeDtypeStruct((MLA_HEADS, s, V_HEAD), BF16)),
        grid=(s // tm,),
        in_specs=[pl.BlockSpec((tm, 1024), lambda i: (i, 0)),
                  pl.BlockSpec((tm, LANES), lambda i: (i, 0)),
                  pl.BlockSpec((tm, LANES), lambda i: (i, 0)),
                  full(gq), full(gkv), full(wqn), full(wqr), full(wkn), full(wv)],
        out_specs=(pl.BlockSpec((tm, MLA_HEADS * HEAD_SLOT), lambda i: (i, 0)),
                   pl.BlockSpec((MLA_HEADS, tm, HEAD_SLOT), lambda i: (0, i, 0)),
                   pl.BlockSpec((MLA_HEADS, tm, V_HEAD), lambda i: (0, i, 0))),
        compiler_params=_params("parallel"),
        name="mla_prep",
    )(h, cos, sin, gq, gkv, wqn, wqr, wkn, wv)


def _moba_kprep_kernel(h_ref, hv_ref, cos_ref, sin_ref, k_ref, v_ref, kbar_ref, *, rows):
    cos, sin_signed = _rope_tables_moba(cos_ref, sin_ref)
    row = pl.program_id(0) * rows + lax.broadcasted_iota(jnp.int32, (rows, LANES), 0)
    lane = lax.broadcasted_iota(jnp.int32, (rows, LANES), 1)
    onehot = (lane == row // MOBA_BLOCK).astype(BF16)
    for h in range(MOBA_HEADS):
        kr = _rope_moba(h_ref[:, h * LANES:(h + 1) * LANES].astype(F32), cos, sin_signed)
        k_ref[h, :, 0:LANES] = kr.astype(BF16)
        k_ref[h, :, LANES:HEAD_SLOT] = onehot
        v_ref[h] = hv_ref[:, h * LANES:(h + 1) * LANES]
        for b in range(rows // MOBA_BLOCK):
            kbar_ref[b:b + 1, h * LANES:(h + 1) * LANES] = jnp.mean(
                kr[b * MOBA_BLOCK:(b + 1) * MOBA_BLOCK], axis=0, keepdims=True)


def _moba_kprep(h, cos, sin, *, rows=2048):
    s = h.shape[0]
    return pl.pallas_call(
        functools.partial(_moba_kprep_kernel, rows=rows),
        out_shape=(jax.ShapeDtypeStruct((MOBA_HEADS, s, HEAD_SLOT), BF16),
                   jax.ShapeDtypeStruct((MOBA_HEADS, s, MOBA_HEAD_DIM), BF16),
                   jax.ShapeDtypeStruct((s // MOBA_BLOCK, MOBA_WIDTH), F32)),
        grid=(s // rows,),
        in_specs=[pl.BlockSpec((rows, MOBA_WIDTH), lambda i: (i, 2)),
                  pl.BlockSpec((rows, MOBA_WIDTH), lambda i: (i, 3)),
                  pl.BlockSpec((rows, LANES), lambda i: (i, 0)),
                  pl.BlockSpec((rows, LANES), lambda i: (i, 0))],
        out_specs=(pl.BlockSpec((MOBA_HEADS, rows, HEAD_SLOT), lambda i: (0, i, 0)),
                   pl.BlockSpec((MOBA_HEADS, rows, MOBA_HEAD_DIM), lambda i: (0, i, 0)),
                   pl.BlockSpec((rows // MOBA_BLOCK, MOBA_WIDTH), lambda i: (i, 0))),
        compiler_params=_params("parallel"),
        name="moba_kprep",
    )(h, h, cos, sin)


def _top_mask(vals, row_idx, k):
    n = vals.shape[0]
    chosen = jnp.zeros(vals.shape, jnp.bool_)
    for _ in range(k):
        top = jnp.max(vals, axis=0, keepdims=True)
        first = jnp.min(jnp.where(vals == top, row_idx, n), axis=0, keepdims=True)
        pick = row_idx == first
        chosen = chosen | pick
        vals = jnp.where(pick, -jnp.inf, vals)
    return chosen


def _moba_qprep_kernel(h_ref, cos_ref, sin_ref, kbar_ref, q_ref, *, tm):
    scale = MOBA_HEAD_DIM ** -0.5
    cos, sin_signed = _rope_tables_moba(cos_ref, sin_ref)
    nb = N_MOBA_BLOCKS
    tok = pl.program_id(0) * tm + lax.broadcasted_iota(jnp.int32, (1, tm), 1)
    qblk = tok // MOBA_BLOCK
    blk = lax.broadcasted_iota(jnp.int32, (nb, 1), 0)
    past = blk < qblk
    own = blk == qblk
    for h in range(MOBA_HEADS):
        lo = h * HEAD_SLOT
        qr = _rope_moba(h_ref[:, h * LANES:(h + 1) * LANES].astype(F32), cos, sin_signed)
        gate = _nt_dot(kbar_ref[:, h * LANES:(h + 1) * LANES], qr,
                       precision=lax.Precision.HIGHEST)
        gate = jnp.where(past, gate, -jnp.inf)
        sel = past & _top_mask(gate, blk, MOBA_TOPK)
        bias = jnp.where(sel | own, 0.0, MASK_BIAS)
        bias = jnp.concatenate([bias, jnp.zeros((LANES - nb, tm), F32)], axis=0)
        q_ref[:, lo:lo + LANES] = (qr * scale).astype(BF16)
        q_ref[:, lo + LANES:lo + HEAD_SLOT] = bias.T.astype(BF16)


def _moba_qprep(h, cos, sin, kbar, *, tm=512):
    s = h.shape[0]
    return pl.pallas_call(
        functools.partial(_moba_qprep_kernel, tm=tm),
        out_shape=jax.ShapeDtypeStruct((s, MOBA_HEADS * HEAD_SLOT), BF16),
        grid=(s // tm,),
        in_specs=[pl.BlockSpec((tm, MOBA_WIDTH), lambda i: (i, 1)),
                  pl.BlockSpec((tm, LANES), lambda i: (i, 0)),
                  pl.BlockSpec((tm, LANES), lambda i: (i, 0)),
                  pl.BlockSpec(kbar.shape, lambda i: (0, 0))],
        out_specs=pl.BlockSpec((tm, MOBA_HEADS * HEAD_SLOT), lambda i: (i, 0)),
        compiler_params=_params("parallel"),
        name="moba_qprep",
    )(h, cos, sin, kbar)


def _flash_kernel(qa_ref, qb_ref, k_ref, v_ref, o_ref, *scratch, tq, nq, hp):
    p_idx = pl.program_id(1)
    nc = tq // LANES
    ones = jnp.ones((tq, LANES), BF16)
    heads = [scratch[3 * h:3 * h + 3] for h in range(hp)]
    for h, (q2_ref, m2_ref, acc2_ref) in enumerate(heads):
        q2_ref[0] = qa_ref[:, h * HEAD_SLOT:(h + 1) * HEAD_SLOT]
        q2_ref[1] = qb_ref[:, h * HEAD_SLOT:(h + 1) * HEAD_SLOT]
        m2_ref[...] = jnp.full(m2_ref.shape, -jnp.inf, F32)
        acc2_ref[...] = jnp.zeros(acc2_ref.shape, F32)

    def kv_rows(j):
        return pl.ds(pl.multiple_of(j * tq, tq), tq)

    def scores(h, sel, j):
        return _nt_dot(heads[h][0][sel], k_ref[h, kv_rows(j), :])

    def consume(h, sel, j, s):
        _, m2_ref, acc2_ref = heads[h]
        v = v_ref[h, kv_rows(j), :]
        chunks = [s[:, c * LANES:(c + 1) * LANES] for c in range(nc)]
        part = functools.reduce(jnp.maximum, chunks)
        m_old = m2_ref[sel]
        m_new = jnp.maximum(m_old, jnp.max(part, axis=1, keepdims=True))
        alpha = jnp.exp(m_old - m_new)
        p = jnp.concatenate([jnp.exp(ch - m_new).astype(BF16) for ch in chunks], axis=1)
        pv = jnp.dot(p, jnp.concatenate([v, ones], axis=1), preferred_element_type=F32)
        acc2_ref[sel] = jnp.concatenate([alpha, alpha], axis=1) * acc2_ref[sel] + pv
        m2_ref[sel] = m_new

    def diagonal(s):
        r = lax.broadcasted_iota(jnp.int32, s.shape, 0)
        c = lax.broadcasted_iota(jnp.int32, s.shape, 1)
        return jnp.where(c <= r, s, -jnp.inf)

    steps = []
    for t in range(nq - 1):
        sel = (t >= p_idx).astype(jnp.int32)
        steps += [(h, sel, t - sel * p_idx, False) for h in range(hp)]
    steps += [(h, 0, p_idx, True) for h in range(hp)]
    steps += [(h, 1, nq - 1 - p_idx, True) for h in range(hp)]

    ahead = hp
    pending = [scores(*st[:3]) for st in steps[:ahead]]
    for t, (h, sel, j, diag) in enumerate(steps):
        s = pending.pop(0)
        if t + ahead < len(steps):
            pending.append(scores(*steps[t + ahead][:3]))
        consume(h, sel, j, diagonal(s) if diag else s)
    for h, (_, _, acc2_ref) in enumerate(heads):
        for sel in range(2):
            o_ref[sel, :, h * LANES:(h + 1) * LANES] = (
                acc2_ref[sel, :, :LANES] / acc2_ref[sel, :, LANES:]).astype(o_ref.dtype)


def _flash(q, k, v, *, tq=512, hp=FLASH_HEADS_PER_STEP):
    s = q.shape[0]
    n_heads = k.shape[0]
    nq = s // tq
    assert n_heads % hp == 0
    per_head = [pltpu.VMEM((2, tq, HEAD_SLOT), BF16), pltpu.VMEM((2, tq, LANES), F32),
                pltpu.VMEM((2, tq, 2 * LANES), F32)]
    return pl.pallas_call(
        functools.partial(_flash_kernel, tq=tq, nq=nq, hp=hp),
        out_shape=jax.ShapeDtypeStruct((2, s // 2, n_heads * LANES), BF16),
        grid=(n_heads // hp, nq // 2),
        in_specs=[pl.BlockSpec((tq, hp * HEAD_SLOT), lambda h, p: (p, h)),
                  pl.BlockSpec((tq, hp * HEAD_SLOT), lambda h, p: (nq - 1 - p, h)),
                  pl.BlockSpec((hp, s, HEAD_SLOT), lambda h, p: (h, 0, 0)),
                  pl.BlockSpec((hp, s, LANES), lambda h, p: (h, 0, 0))],
        out_specs=pl.BlockSpec((2, tq, hp * LANES), lambda h, p: (0, p, h)),
        scratch_shapes=per_head * hp,
        compiler_params=_params("parallel", "parallel"),
        name="flash",
    )(q, q, k, v)


def _flash_row_block(i, tm, s, tq=512):
    per_tile = tq // tm
    tile = i // per_tile
    sub = i % per_tile
    nq = s // tq
    hi = (tile >= nq // 2).astype(jnp.int32)
    return hi, jnp.where(hi == 1, nq - 1 - tile, tile) * per_tile + sub


def _outproj_kernel(om_ref, ob_ref, gm_ref, gb_ref, wo_ref, x_ref, lg_ref, lb_ref, wrt_ref, br_ref,
                    x1_ref, idx_ref, pos_ref, gate_ref, cnt_ref):
    @pl.when(pl.program_id(0) == 0)
    def _():
        cnt_ref[...] = jnp.zeros(cnt_ref.shape, jnp.int32)

    a = jnp.concatenate([_rms(om_ref[...].astype(F32), gm_ref[...]).astype(BF16),
                         _rms(ob_ref[...].astype(F32), gb_ref[...]).astype(BF16)], axis=1)
    mixed = jnp.dot(a, wo_ref[...], preferred_element_type=F32)
    x1 = _layer_norm(DEEPNORM_ALPHA * x_ref[...] + mixed, lg_ref[...], lb_ref[...])
    x1_ref[...] = x1

    logits = _nt_dot(wrt_ref[...], x1, precision=lax.Precision.HIGHEST) + br_ref[...]
    tm = logits.shape[1]
    eidx = lax.broadcasted_iota(jnp.int32, (N_EXPERTS, 1), 0)
    rank = _topk_rank(logits, eidx)
    sel = rank < TOP_K
    mx = jnp.max(logits, axis=0, keepdims=True)
    p = jnp.where(sel, jnp.exp(logits - mx), 0.0)
    gates = p / jnp.sum(p, axis=0, keepdims=True)

    before = (lax.broadcasted_iota(jnp.int32, (tm, tm), 0)
              < lax.broadcasted_iota(jnp.int32, (tm, tm), 1)).astype(BF16)
    prefix = jnp.dot(sel.astype(BF16), before, preferred_element_type=F32).astype(jnp.int32)
    pos = cnt_ref[:, 0:1] + prefix
    cnt_ref[...] = cnt_ref[...] + jnp.sum(sel.astype(jnp.int32), axis=1, keepdims=True)

    pick = lambda vals, k, zero: jnp.sum(jnp.where(rank == k, vals, zero), axis=0, keepdims=True)
    pad_i = [jnp.zeros((8 - TOP_K, tm), jnp.int32)]
    idx_ref[...] = jnp.concatenate([pick(eidx, k, 0) for k in range(TOP_K)] + pad_i, axis=0)
    pos_ref[...] = jnp.concatenate([pick(pos, k, 0) for k in range(TOP_K)] + pad_i, axis=0)
    gate_rows = jnp.concatenate([pick(gates, k, 0.0) for k in range(TOP_K)]
                                + [jnp.zeros((LANES - TOP_K, tm), F32)], axis=0)
    gate_ref[...] = gate_rows.T


def _outproj(o_mla, o_moba, g_mla, g_moba, wo, x, ln_g, ln_b, wr_t, b_r, *, tm=512):
    s = x.shape[0]
    full = lambda a: pl.BlockSpec(a.shape, lambda i: (0,) * a.ndim)
    return pl.pallas_call(
        _outproj_kernel,
        out_shape=(jax.ShapeDtypeStruct((s, D_MODEL), F32),
                   jax.ShapeDtypeStruct((8, s), jnp.int32),
                   jax.ShapeDtypeStruct((8, s), jnp.int32),
                   jax.ShapeDtypeStruct((s, LANES), F32),
                   jax.ShapeDtypeStruct((N_EXPERTS, LANES), jnp.int32)),
        grid=(s // tm,),
        in_specs=[pl.BlockSpec((None, tm, MLA_WIDTH), lambda i: (*_flash_row_block(i, tm, s), 0)),
                  pl.BlockSpec((None, tm, MOBA_WIDTH), lambda i: (*_flash_row_block(i, tm, s), 0)),
                  full(g_mla), full(g_moba), full(wo),
                  pl.BlockSpec((tm, D_MODEL), lambda i: (i, 0)),
                  full(ln_g), full(ln_b), full(wr_t), full(b_r)],
        out_specs=(pl.BlockSpec((tm, D_MODEL), lambda i: (i, 0)),
                   pl.BlockSpec((8, tm), lambda i: (0, i)),
                   pl.BlockSpec((8, tm), lambda i: (0, i)),
                   pl.BlockSpec((tm, LANES), lambda i: (i, 0)),
                   pl.BlockSpec((N_EXPERTS, LANES), lambda i: (0, 0))),
        compiler_params=_params("arbitrary"),
        name="outproj_router",
    )(o_mla, o_moba, g_mla, g_moba, wo, x, ln_g, ln_b, wr_t, b_r)


def _ffn_kernel(ie_ref, nb_ref, tab_hbm, x1_hbm, wg_ref, wu_ref, bg_ref, bu_ref, wd_ref, bd_ref,
                y_hbm, tab_smem, xf_ref, xb_ref, y_ref, sem_tab, sem_in, sem_out,
                *, nf, n_tok, max_blocks):
    w = pl.program_id(0)
    s = pl.program_id(1)
    n_work = pl.num_programs(0)
    nb = nb_ref[w]
    nb_prev = nb_ref[jnp.maximum(w - 1, 0)]
    w_next = jnp.minimum(w + 1, n_work - 1)
    has_next = (w + 1 < n_work) & (nb_ref[w_next] > 0)
    sub = SUBLANES
    tiles_per_block = MOE_BLOCK // sub
    slot = lax.bitwise_and(w, 1)

    tab_len = tab_smem.shape[0] // 2

    def load_table(item, slot_):
        dst = tab_smem.at[pl.ds(pl.multiple_of(slot_ * tab_len, tab_len), tab_len)]
        cp = pltpu.make_async_copy(tab_hbm.at[item], dst, sem_tab)
        cp.start()
        cp.wait()

    def for_rows(n_blocks, slot_, fn):
        def body(b, carry):
            base = slot_ * tab_len + b * MOE_BLOCK
            for i in range(MOE_BLOCK):
                fn(b * tiles_per_block + i // sub, i % sub, tab_smem[base + i])
            return carry
        lax.fori_loop(0, n_blocks, body, 0)

    def wait_tiles(n_blocks, make_copy):
        def body(g, carry):
            make_copy().wait()
            return carry
        lax.fori_loop(0, n_blocks * tiles_per_block, body, 0)

    def hbm_row(ref, r):
        return ref.at[lax.shift_right_logical(r, 3), pl.ds(lax.bitwise_and(r, sub - 1), 1)]

    def start_gather(n_blocks, slot_):
        for_rows(n_blocks, slot_, lambda g, j, e: pltpu.make_async_copy(
            hbm_row(x1_hbm, e), xf_ref.at[g, pl.ds(j, 1)],
            sem_in).start())

    def out_copy(b):
        tiles = pl.ds(b * tiles_per_block, tiles_per_block)
        dst = pl.ds(w * (MOE_ROWS // sub) + b * tiles_per_block, tiles_per_block)
        return pltpu.make_async_copy(y_ref.at[tiles], y_hbm.at[dst], sem_out)

    def wait_out(n_blocks):
        def body(b, carry):
            out_copy(0).wait()
            return carry
        lax.fori_loop(0, n_blocks, body, 0)

    @pl.when((nb > 0) & (s == 0))
    def _():
        @pl.when(w == 0)
        def _():
            load_table(0, 0)
            start_gather(nb, 0)

        wait_tiles(nb, lambda: pltpu.make_async_copy(x1_hbm.at[0], xf_ref.at[0], sem_in))

        def cast(b, carry):
            tiles = pl.ds(b * tiles_per_block, tiles_per_block)
            rows = pl.ds(pl.multiple_of(b * MOE_BLOCK, MOE_BLOCK), MOE_BLOCK)
            xb_ref[rows, :] = xf_ref[tiles].reshape(MOE_BLOCK, D_MODEL).astype(BF16)
            return carry

        lax.fori_loop(0, nb, cast, 0)

        @pl.when(w > 0)
        def _():
            wait_out(nb_prev)

        bias = jnp.broadcast_to(bd_ref[...], (tiles_per_block, sub, D_MODEL))

        def init(b, carry):
            y_ref[pl.ds(b * tiles_per_block, tiles_per_block)] = bias
            return carry

        lax.fori_loop(0, nb, init, 0)

    def step(m):
        x = xb_ref[0:m, :]
        g = jnp.dot(x, wg_ref[...].astype(BF16), preferred_element_type=F32) + bg_ref[...]
        u = jnp.dot(x, wu_ref[...].astype(BF16), preferred_element_type=F32) + bu_ref[...]
        g = jnp.minimum(g, SWIGLU_LIMIT)
        u = jnp.clip(u, -SWIGLU_LIMIT, SWIGLU_LIMIT)
        act = ((u + 1.0) * (g * (1.0 / (1.0 + jnp.exp(-SWIGLU_ALPHA * g))))).astype(BF16)
        down = jnp.dot(act, wd_ref[...].astype(BF16), preferred_element_type=F32)
        y_ref[0:m // sub] = y_ref[0:m // sub] + down.reshape(m // sub, sub, D_MODEL)

    for k in range(1, max_blocks + 1):
        pl.when(nb == k)(functools.partial(step, k * MOE_BLOCK))

    @pl.when((nb > 0) & (s == nf - 1))
    def _():
        def start_out(b, carry):
            out_copy(b).start()
            return carry

        lax.fori_loop(0, nb, start_out, 0)

        @pl.when(has_next)
        def _():
            load_table(w_next, 1 - slot)
            start_gather(nb_ref[w_next], 1 - slot)

        @pl.when(jnp.logical_not(has_next))
        def _():
            wait_out(nb)


def _ffn(item_e, item_nb, tab, x1, w_gate_up, b_gate_up, w_down, b_down, *, tf=256):
    n_work, tab_len = tab.shape
    n_tok = x1.shape[0]
    nf = D_FF // tf
    assert n_tok & (n_tok - 1) == 0 and MOE_ROWS <= n_tok and MOE_ROWS % MOE_BLOCK == 0
    assert n_tok % SUBLANES == 0 and tab_len >= MOE_ROWS

    def tile(w, s, nbr):
        return jnp.where(nbr[w] > 0, s, nf - 1)

    def colmap(off):
        return lambda w, s, ie, nbr: (ie[w], 0, off + tile(w, s, nbr))

    def rowmap(w, s, ie, nbr):
        return ie[w], tile(w, s, nbr), 0

    grid_spec = pltpu.PrefetchScalarGridSpec(
        num_scalar_prefetch=2,
        grid=(n_work, nf),
        in_specs=[
            pl.BlockSpec(memory_space=pl.ANY),
            pl.BlockSpec(memory_space=pl.ANY),
            pl.BlockSpec((None, D_MODEL, tf), colmap(0)),
            pl.BlockSpec((None, D_MODEL, tf), colmap(nf)),
            pl.BlockSpec((None, 1, tf), colmap(0)),
            pl.BlockSpec((None, 1, tf), colmap(nf)),
            pl.BlockSpec((None, tf, D_MODEL), rowmap),
            pl.BlockSpec((None, 1, D_MODEL), lambda w, s, ie, nbr: (ie[w], 0, 0)),
        ],
        out_specs=pl.BlockSpec(memory_space=pl.ANY),
        scratch_shapes=[pltpu.SMEM((2 * tab_len,), jnp.int32),
                        pltpu.VMEM((MOE_ROWS // SUBLANES, SUBLANES, D_MODEL), F32),
                        pltpu.VMEM((MOE_ROWS, D_MODEL), BF16),
                        pltpu.VMEM((MOE_ROWS // SUBLANES, SUBLANES, D_MODEL), F32),
                        pltpu.SemaphoreType.DMA, pltpu.SemaphoreType.DMA, pltpu.SemaphoreType.DMA],
    )
    return pl.pallas_call(
        functools.partial(_ffn_kernel, nf=nf, n_tok=n_tok, max_blocks=MOE_ROWS // MOE_BLOCK),
        out_shape=jax.ShapeDtypeStruct((n_work * MOE_ROWS // SUBLANES, SUBLANES, D_MODEL), F32),
        grid_spec=grid_spec,
        compiler_params=_params("arbitrary", "arbitrary"),
        name="experts",
    )(item_e, item_nb, tab, x1.reshape(n_tok // SUBLANES, SUBLANES, D_MODEL), w_gate_up, w_gate_up,
      b_gate_up, b_gate_up, w_down, b_down)


def _final_kernel(yrow_hbm, y_hbm, x1_ref, gate_ref, g_ref, b_ref, o_ref,
                  tab_smem, ybuf_ref, sem_tab, sem_rows, *, tm):
    i = pl.program_id(0)
    n = pl.num_programs(0)
    slot = lax.bitwise_and(i, 1)
    n_rows = TOP_K * tm
    sub = SUBLANES

    def fetch(tile, slot_):
        dst = tab_smem.at[pl.ds(pl.multiple_of(slot_ * n_rows, n_rows), n_rows)]
        cp = pltpu.make_async_copy(yrow_hbm.at[tile], dst, sem_tab)
        cp.start()
        cp.wait()
        base = slot_ * n_rows
        for j in range(n_rows):
            r = tab_smem[base + j]
            pltpu.make_async_copy(
                y_hbm.at[lax.shift_right_logical(r, 3), pl.ds(lax.bitwise_and(r, sub - 1), 1)],
                ybuf_ref.at[slot_, j // sub, pl.ds(j % sub, 1)], sem_rows.at[slot_]).start()

    @pl.when(i == 0)
    def _():
        fetch(0, 0)

    @pl.when(i + 1 < n)
    def _():
        fetch(i + 1, 1 - slot)

    pltpu.make_async_copy(y_hbm.at[pl.ds(0, n_rows // sub)], ybuf_ref.at[slot],
                          sem_rows.at[slot]).wait()
    ffn = None
    for k in range(TOP_K):
        yk = ybuf_ref[slot, k * tm // sub:(k + 1) * tm // sub].reshape(tm, D_MODEL)
        term = gate_ref[:, k:k + 1] * yk
        ffn = term if ffn is None else ffn + term
    o_ref[...] = _layer_norm(DEEPNORM_ALPHA * x1_ref[...] + ffn, g_ref[...], b_ref[...])


def _final(yrow, y, x1, gates, g, b, *, tm=COMBINE_ROWS):
    s = x1.shape[0]
    n_rows = TOP_K * tm
    return pl.pallas_call(
        functools.partial(_final_kernel, tm=tm),
        out_shape=jax.ShapeDtypeStruct((s, D_MODEL), F32),
        grid=(s // tm,),
        in_specs=[pl.BlockSpec(memory_space=pl.ANY),
                  pl.BlockSpec(memory_space=pl.ANY),
                  pl.BlockSpec((tm, D_MODEL), lambda i: (i, 0)),
                  pl.BlockSpec((tm, LANES), lambda i: (i, 0)),
                  pl.BlockSpec((1, D_MODEL), lambda i: (0, 0)),
                  pl.BlockSpec((1, D_MODEL), lambda i: (0, 0))],
        out_specs=pl.BlockSpec((tm, D_MODEL), lambda i: (i, 0)),
        scratch_shapes=[pltpu.SMEM((2 * n_rows,), jnp.int32),
                        pltpu.VMEM((2, n_rows // SUBLANES, SUBLANES, D_MODEL), F32),
                        pltpu.SemaphoreType.DMA, pltpu.SemaphoreType.DMA((2,))],
        compiler_params=_params("arbitrary"),
        name="combine_final_ln",
    )(yrow, y, x1, gates, g, b)


def _layer(x, pos, w_in, g_q_a, w_q_b, g_kv_a, w_kv_b, g_out_mla, g_out_moba, w_o, ln1_g, ln1_b,
           w_router, b_router, w_gate_up, b_gate_up, w_down, b_down, ln2_g, ln2_b):
    s = x.shape[0]
    wq = w_q_b.reshape(Q_LORA, MLA_HEADS, QK_NOPE + QK_ROPE)
    wqn = wq[:, :, :QK_NOPE].reshape(Q_LORA, MLA_HEADS * LANES).astype(BF16)
    wqr = jnp.pad(wq[:, :, QK_NOPE:], ((0, 0), (0, 0), (0, LANES - QK_ROPE))).reshape(
        Q_LORA, MLA_HEADS * LANES).astype(BF16)
    wkv = w_kv_b.reshape(KV_LORA, MLA_HEADS, QK_NOPE + V_HEAD)
    wkn = wkv[:, :, :QK_NOPE].reshape(KV_LORA, MLA_HEADS * LANES).astype(BF16)
    wv = wkv[:, :, QK_NOPE:].reshape(KV_LORA, MLA_WIDTH).astype(BF16)
    row = lambda a: a.reshape(1, -1)

    cos, sin = _rope_angles(pos.reshape(s, 1))
    h = _inproj(x, w_in.T)
    q_mla, k_mla, v_mla = _mla_prep(h, cos, sin, row(g_q_a), row(g_kv_a), wqn, wqr, wkn, wv)
    k_moba, v_moba, kbar = _moba_kprep(h, cos, sin)
    q_moba = _moba_qprep(h, cos, sin, kbar)
    o_mla = _flash(q_mla, k_mla, v_mla)
    o_moba = _flash(q_moba, k_moba, v_moba)

    x1, idx_t, pos_t, gates, cnt = _outproj(o_mla, o_moba, row(g_out_mla), row(g_out_moba),
                                            w_o.astype(BF16), x, row(ln1_g), row(ln1_b),
                                            w_router.T, b_router.reshape(N_EXPERTS, 1))

    n_work = -(-s * TOP_K // MOE_ROWS) + N_EXPERTS
    counts = cnt[:, 0]
    items_per_e = (counts + MOE_ROWS - 1) // MOE_ROWS
    item_end = jnp.cumsum(items_per_e)
    item_start = item_end - items_per_e
    w_idx = jnp.arange(n_work)
    item_e = jnp.minimum(jnp.sum(w_idx[:, None] >= item_end[None, :], axis=1), N_EXPERTS - 1)
    item_rows = jnp.clip(counts[item_e] - (w_idx - item_start[item_e]) * MOE_ROWS, 0, MOE_ROWS)
    item_nb = jnp.where(w_idx < item_end[-1], (item_rows + MOE_BLOCK - 1) // MOE_BLOCK, 0)
    e_k = idx_t[:TOP_K]
    p_k = pos_t[:TOP_K]
    start_k = jnp.sum(jnp.where(e_k[..., None] == jnp.arange(N_EXPERTS), item_start, 0), axis=-1)
    item_row = ((start_k + p_k // MOE_ROWS) * MOE_ROWS + p_k % MOE_ROWS).astype(jnp.int32)
    tok = jnp.broadcast_to(jnp.arange(s, dtype=jnp.int32), (TOP_K, s))
    filler = jnp.arange(n_work * MOE_ROWS, dtype=jnp.int32) % MOE_ROWS
    tab = filler.at[item_row.reshape(-1)].set(tok.reshape(-1))
    tab = jnp.pad(tab.reshape(n_work, MOE_ROWS), ((0, 0), (0, MOE_TAB_LEN - MOE_ROWS)))

    y = _ffn(item_e.astype(jnp.int32), item_nb.astype(jnp.int32), tab, x1, w_gate_up,
             b_gate_up.reshape(N_EXPERTS, 1, 2 * D_FF), w_down, b_down.reshape(N_EXPERTS, 1, D_MODEL))
    tm = COMBINE_ROWS
    yrow = item_row.reshape(TOP_K, s // tm, tm).transpose(1, 0, 2).reshape(s // tm, TOP_K * tm)
    return _final(yrow, y, x1, gates, row(ln2_g), row(ln2_b))


def kernel(x, positions, w_in, g_q_a, w_q_b, g_kv_a, w_kv_b, g_out_mla, g_out_moba, w_o, ln1_g, ln1_b,
           w_router, b_router, w_gate_up, b_gate_up, w_down, b_down, ln2_g, ln2_b):
    b, s, d = x.shape
    assert b == 1 and s == SEQ and d == D_MODEL and w_in.shape[0] == DEPTH
    hcur = x[0]
    for l in range(DEPTH):
        hcur = _layer(hcur, positions[0], w_in[l], g_q_a[l], w_q_b[l], g_kv_a[l], w_kv_b[l],
                      g_out_mla[l], g_out_moba[l], w_o[l], ln1_g[l], ln1_b[l], w_router[l],
                      b_router[l], w_gate_up[l], b_gate_up[l], w_down[l], b_down[l], ln2_g[l],
                      ln2_b[l])
    return hcur[None]
```
